```python
import math
import jax, jax.numpy as jnp
from jax import lax
import numpy as np

D_MODEL = 1024
BATCH = 2
SEQ = 8192
DEPTH = 2

GRID_W = 64
CTX_LEN = 256
N_EVEN = (DEPTH + 1) // 2
N_ODD = DEPTH // 2
RMS_EPS = 1e-6
F32 = jnp.float32

DA_HEADS = 8
DA_HEAD_DIM = 64
DA_V_DIM = 2 * DA_HEAD_DIM
DA_QK_W = DA_HEADS * 2 * DA_HEAD_DIM
DA_V_W = DA_HEADS * DA_V_DIM
Q_BLOCK = 128
ROPE_BASE = 10000.0

SSM_D_INNER = D_MODEL
SSM_HEAD_DIM = 64
SSM_HEADS = SSM_D_INNER // SSM_HEAD_DIM
SSM_GROUPS = 2
SSM_HPG = SSM_HEADS // SSM_GROUPS
SSM_STATE = 128
SSM_CONV = 5
SSM_CHUNK = 128
SSM_BC_W = SSM_GROUPS * SSM_STATE
SSM_XB_W = SSM_D_INNER + SSM_BC_W

COL_Q = 0
COL_Z = COL_Q + DA_QK_W
COL_C = COL_Z + SSM_D_INNER
COL_K = COL_C + SSM_BC_W
COL_V = COL_K + DA_QK_W
COL_XB = COL_V + DA_V_W
COL_DT = COL_XB + SSM_XB_W
IN_W = COL_DT + 2 * SSM_HEADS
MIX_W = DA_V_W + SSM_D_INNER

FOURIER_GROUPS = 4
FOURIER_GW = D_MODEL // FOURIER_GROUPS

N_EXPERTS = 32
TOP_K = 4
D_EXPERT = D_MODEL
SWIGLU_LIMIT = 7.0
SWIGLU_ALPHA = 1.702
MOE_BLOCK = 128

kernel_name = 'hybrid_diffattn_ssd_fourier_moe_dit'


def rmsnorm(x, g):
    xf = x.astype(F32)
    xf = xf * lax.rsqrt(jnp.mean(xf * xf, axis=-1, keepdims=True) + RMS_EPS)
    return (xf * g.astype(F32)).astype(x.dtype)


def ada_mod(cvec, w, b):
    return jnp.split(jax.nn.silu(cvec) @ w + b, 6, axis=-1)


def modulate(h, shift, scale):
    return h * (1 + scale) + shift


def cols_of(p, start, width, col0=0):
    return p[..., start - col0:start - col0 + width]


def rotate_pairs(x, ang):
    h = x.shape[-1] // 2
    x1, x2 = x[..., :h], x[..., h:]
    cos, sin = jnp.cos(ang), jnp.sin(ang)
    return jnp.concatenate([x1 * cos - x2 * sin, x1 * sin + x2 * cos], axis=-1)


def axial_rope(x, rows, cols):
    half = x.shape[-1] // 2
    n_freq = half // 2
    freqs = ROPE_BASE ** (-jnp.arange(n_freq, dtype=F32) / n_freq)
    ang_r = (rows.astype(F32)[:, None] * freqs)[None, :, None, None, :]
    ang_c = (cols.astype(F32)[:, None] * freqs)[None, :, None, None, :]
    xf = x.astype(F32)
    out = jnp.concatenate([rotate_pairs(xf[..., :half], ang_r),
                           rotate_pairs(xf[..., half:], ang_c)], axis=-1)
    return out.astype(x.dtype)


def qk_heads(u, g):
    return rmsnorm(u.reshape(u.shape[:-1] + (DA_HEADS, 2, DA_HEAD_DIM)), g)


def v_heads(u):
    return u.reshape(u.shape[:-1] + (DA_HEADS, DA_V_DIM))


def diff_attention(q, k, v, lam, subln_g, lam_init):
    b, sq, h, _, d = q.shape
    nb = sq // Q_BLOCK
    qb = jnp.moveaxis(q.reshape(b, nb, Q_BLOCK, h, 2, d), 1, 0)
    scale = d ** -0.5

    def one_block(q_blk):
        sc = jnp.einsum('bqhcd,bkhcd->bhcqk', q_blk, k, preferred_element_type=F32) * scale
        p = jax.nn.softmax(sc, axis=-1)
        w = (p[:, :, 0] - lam * p[:, :, 1]).astype(v.dtype)
        return jnp.einsum('bhqk,bkhe->bqhe', w, v)

    o = jnp.moveaxis(lax.map(one_block, qb), 0, 1).reshape(b, sq, h, v.shape[-1])
    o = rmsnorm(o, subln_g) * (1.0 - lam_init)
    return o.reshape(b, sq, h * v.shape[-1])


def dwconv_silu(u, w, bias):
    pad = (SSM_CONV - 1) // 2
    y = lax.conv_general_dilated(u, w[:, None, :].astype(u.dtype), window_strides=(1,),
                                 padding=[(pad, pad)], dimension_numbers=('NWC', 'WIO', 'NWC'),
                                 feature_group_count=u.shape[-1])
    return jax.nn.silu(y + bias)


def ssm_xb_dt(pr, col0, conv_w, conv_b, dt_bias):
    xb = dwconv_silu(cols_of(pr, COL_XB, SSM_XB_W, col0), conv_w, conv_b)
    bsz, l, _ = xb.shape
    xs = xb[..., :SSM_D_INNER].reshape(bsz, l, SSM_GROUPS, SSM_HPG, SSM_HEAD_DIM)
    bm = xb[..., SSM_D_INNER:].reshape(bsz, l, SSM_GROUPS, SSM_STATE)
    dt_raw = cols_of(pr, COL_DT, 2 * SSM_HEADS, col0).astype(F32).reshape(bsz, l, 2, SSM_HEADS)
    dt = jax.nn.softplus(dt_raw + dt_bias.astype(F32)).reshape(bsz, l, 2, SSM_GROUPS, SSM_HPG)
    return xs, bm, dt


def ssm_c(pr, col0, conv_w, conv_b):
    cm = dwconv_silu(cols_of(pr, COL_C, SSM_BC_W, col0), conv_w, conv_b)
    return cm.reshape(cm.shape[:2] + (SSM_GROUPS, SSM_STATE))


def ssd_final_state(xs, dt, a, bm):
    acs = jnp.cumsum(dt * a, axis=1)
    wx = (jnp.exp(acs[:, -1:] - acs) * dt)[..., None] * xs.astype(F32)
    return jnp.einsum('blgn,blgjp->bgjpn', bm.astype(F32), wx)


def ssd_chunked(xs, dt, a, bm, cm, h0):
    b, l, g, j, p = xs.shape
    n = bm.shape[-1]
    q = SSM_CHUNK
    nc = l // q
    xdt = (xs.astype(F32) * dt[..., None]).reshape(b, nc, q, g, j, p)
    bc = bm.astype(F32).reshape(b, nc, q, g, n)
    cc = cm.astype(F32).reshape(b, nc, q, g, n)
    acs = jnp.cumsum((dt * a).reshape(b, nc, q, g, j), axis=2)
    lower = jnp.tril(jnp.ones((q, q), bool))[None, None, :, :, None, None]
    seg = acs[:, :, :, None] - acs[:, :, None, :]
    decay = jnp.exp(jnp.where(lower, seg, -jnp.inf))
    cb = jnp.einsum('bclgn,bcsgn->bclsg', cc, bc)
    y_diag = jnp.einsum('bclsgj,bcsgjp->bclgjp', cb[..., None] * decay, xdt)
    wx = jnp.exp(acs[:, :, -1:] - acs)[..., None] * xdt
    states = jnp.einsum('bcsgn,bcsgjp->bcgjpn', bc, wx)
    chunk_decay = jnp.exp(acs[:, :, -1])

    def step(h, inp):
        dec, st = inp
        return dec[..., None, None] * h + st, h

    _, h_in = lax.scan(step, h0.astype(F32),
                       (jnp.moveaxis(chunk_decay, 1, 0), jnp.moveaxis(states, 1, 0)))
    h_in = jnp.moveaxis(h_in, 0, 1)
    y_off = jnp.einsum('bclgn,bcgjpn->bclgjp', cc, h_in) * jnp.exp(acs)[..., None]
    return (y_diag + y_off).reshape(b, l, g, j, p).astype(xs.dtype)


def ssd_bidir(xs, dt, a, bm, cm, h_f, h_b, d_skip, z, norm_g):
    b, l = xs.shape[:2]
    y_f = ssd_chunked(xs, dt[:, :, 0], a[0], bm, cm, h_f)
    y_b = ssd_chunked(xs[:, ::-1], dt[:, ::-1, 1], a[1], bm[:, ::-1], cm[:, ::-1], h_b)[:, ::-1]
    y = y_f + y_b + xs * d_skip.reshape(SSM_GROUPS, SSM_HPG)[..., None].astype(xs.dtype)
    y = y.reshape(b, l, SSM_D_INNER)
    return rmsnorm(y * jax.nn.silu(z), norm_g)


def even_mixer(hx, hc, w_in, w_out, q_norm_g, k_norm_g, da_lambda, da_subln_g,
               conv_xb_w, conv_xb_b, conv_c_w, conv_c_b, dt_bias, a_log, d_skip, ssm_norm_g,
               rows, cols, lam_init, ctx_out):
    px = hx @ w_in
    col0 = 0 if ctx_out else COL_K
    pc = hc @ w_in[:, col0:]
    lam = (jnp.exp(jnp.sum(da_lambda[0] * da_lambda[1])) -
           jnp.exp(jnp.sum(da_lambda[2] * da_lambda[3]))).astype(F32) + lam_init
    a = -jnp.exp(a_log.astype(F32)).reshape(2, SSM_GROUPS, SSM_HPG)

    k_c = qk_heads(cols_of(pc, COL_K, DA_QK_W, col0), k_norm_g)
    v_c = v_heads(cols_of(pc, COL_V, DA_V_W, col0))
    xs_c, bm_c, dt_c = ssm_xb_dt(pc, col0, conv_xb_w, conv_xb_b, dt_bias)
    h_f = ssd_final_state(xs_c, dt_c[:, :, 0], a[0], bm_c)
    h_b = ssd_final_state(xs_c[:, ::-1], dt_c[:, ::-1, 1], a[1], bm_c[:, ::-1])

    q_x = axial_rope(qk_heads(cols_of(px, COL_Q, DA_QK_W), q_norm_g), rows, cols)
    k_x = axial_rope(qk_heads(cols_of(px, COL_K, DA_QK_W), k_norm_g), rows, cols)
    v_x = v_heads(cols_of(px, COL_V, DA_V_W))
    o_att = diff_attention(q_x, jnp.concatenate([k_x, k_c], axis=1),
                           jnp.concatenate([v_x, v_c], axis=1), lam, da_subln_g, lam_init)
    xs, bm, dt = ssm_xb_dt(px, 0, conv_xb_w, conv_xb_b, dt_bias)
    cm = ssm_c(px, 0, conv_c_w, conv_c_b)
    y = ssd_bidir(xs, dt, a, bm, cm, h_f, h_b, d_skip, cols_of(px, COL_Z, SSM_D_INNER), ssm_norm_g)
    out_x = jnp.concatenate([o_att, y], axis=-1) @ w_out
    if not ctx_out:
        return out_x, None

    q_c = qk_heads(cols_of(pc, COL_Q, DA_QK_W), q_norm_g)
    o_att_c = diff_attention(q_c, k_c, v_c, lam, da_subln_g, lam_init)
    cm_c = ssm_c(pc, 0, conv_c_w, conv_c_b)
    zero = jnp.zeros_like(h_f)
    y_c = ssd_bidir(xs_c, dt_c, a, bm_c, cm_c, zero, zero, d_skip,
                    cols_of(pc, COL_Z, SSM_D_INNER), ssm_norm_g)
    out_c = jnp.concatenate([o_att_c, y_c], axis=-1) @ w_out
    return out_x, out_c


def fourier_mix(h, w):
    b, l, d = h.shape
    hg = h.astype(F32).reshape(b, l, FOURIER_GROUPS, FOURIER_GW)
    f = jnp.fft.fft2(hg, axes=(1, 3), norm='ortho').real
    return f.reshape(b, l, d).astype(h.dtype) @ w


def moe(xs, w_router, b_router, w_gu, b_gu, w_dn, b_dn):
    t, d = xs.shape
    logits = (xs @ w_router + b_router).astype(F32)
    top_v, top_i = lax.top_k(logits, TOP_K)
    gates = jax.nn.softmax(top_v, axis=-1)
    n_asg = t * TOP_K
    e_flat = top_i.reshape(-1).astype(jnp.int32)
    order = jnp.argsort(e_flat)
    e_s = e_flat[order]
    tok_s = (jnp.arange(n_asg, dtype=jnp.int32) // TOP_K)[order]
    g_s = gates.reshape(-1)[order]
    counts = jnp.bincount(e_flat, length=N_EXPERTS)
    padded = (counts + MOE_BLOCK - 1) // MOE_BLOCK * MOE_BLOCK
    pend = jnp.cumsum(padded)
    pstart = pend - padded
    ustart = jnp.cumsum(counts) - counts
    dest = pstart[e_s] + jnp.arange(n_asg, dtype=jnp.int32) - ustart[e_s]
    n_rows = -(-(n_asg + N_EXPERTS * (MOE_BLOCK - 1)) // MOE_BLOCK) * MOE_BLOCK
    n_blocks = n_rows // MOE_BLOCK
    row_tok = jnp.full((n_rows,), t, jnp.int32).at[dest].set(tok_s)
    row_gate = jnp.zeros((n_rows,), F32).at[dest].set(g_s)
    blk_e = jnp.minimum(jnp.searchsorted(pend, jnp.arange(n_blocks, dtype=jnp.int32) * MOE_BLOCK,
                                         side='right'), N_EXPERTS - 1)
    xs_pad = jnp.concatenate([xs, jnp.zeros((1, d), xs.dtype)], axis=0)
    xb = xs_pad[row_tok].reshape(n_blocks, MOE_BLOCK, d)

    def expert_block(args):
        x_blk, e = args
        gu = x_blk @ w_gu[e] + b_gu[e]
        glu = jnp.minimum(gu[..., ::2], SWIGLU_LIMIT)
        lin = jnp.clip(gu[..., 1::2], -SWIGLU_LIMIT, SWIGLU_LIMIT)
        act = glu * jax.nn.sigmoid(SWIGLU_ALPHA * glu) * (lin + 1)
        return act @ w_dn[e] + b_dn[e]

    yb = lax.map(expert_block, (xb, blk_e)).reshape(n_rows, d)
    yb = yb * row_gate[:, None].astype(yb.dtype)
    return jnp.zeros((t + 1, d), yb.dtype).at[row_tok].add(yb)[:t]


def setup_inputs(seed: int = 0) -> dict:
    key = jax.random.key(seed)
    ks = jax.random.split(key, 28)
    nrm = lambda k, shp, s: jax.random.normal(k, shp, F32) * s
    D = D_MODEL
    dt0 = jnp.exp(jax.random.uniform(ks[17], (N_EVEN, 2, SSM_HEADS), F32,
                                     math.log(1e-3), math.log(1e-1)))
    return {
        'x': nrm(ks[0], (BATCH, SEQ, D), 1.0),
        'c': nrm(ks[1], (BATCH, D), 1.0),
        'ctx': nrm(ks[2], (BATCH, CTX_LEN, D), 1.0),
        'c_ctx': nrm(ks[3], (D,), 1.0),
        'ada_w': nrm(ks[4], (DEPTH, D, 6 * D), 0.5 * D ** -0.5),
        'ada_b': nrm(ks[5], (DEPTH, 6 * D), 0.02),
        'norm_g': 1.0 + nrm(ks[6], (DEPTH, 2, D), 0.02),
        'w_in': nrm(ks[7], (N_EVEN, D, IN_W), D ** -0.5),
        'w_out': nrm(ks[8], (N_EVEN, MIX_W, D), MIX_W ** -0.5),
        'q_norm_g': 1.0 + nrm(ks[9], (N_EVEN, DA_HEAD_DIM), 0.02),
        'k_norm_g': 1.0 + nrm(ks[10], (N_EVEN, DA_HEAD_DIM), 0.02),
        'da_lambda': nrm(ks[11], (N_EVEN, 4, DA_HEAD_DIM), 0.1),
        'da_subln_g': 1.0 + nrm(ks[12], (N_EVEN, DA_V_DIM), 0.02),
        'conv_xb_w': nrm(ks[13], (N_EVEN, SSM_CONV, SSM_XB_W), SSM_CONV ** -0.5),
        'conv_xb_b': nrm(ks[14], (N_EVEN, SSM_XB_W), 0.02),
        'conv_c_w': nrm(ks[15], (N_EVEN, SSM_CONV, SSM_BC_W), SSM_CONV ** -0.5),
        'conv_c_b': nrm(ks[16], (N_EVEN, SSM_BC_W), 0.02),
        'dt_bias': dt0 + jnp.log(-jnp.expm1(-dt0)),
        'a_log': jnp.log(jax.random.uniform(ks[18], (N_EVEN, 2, SSM_HEADS), F32, 1.0, 16.0)),
        'd_skip': 1.0 + nrm(ks[19], (N_EVEN, SSM_HEADS), 0.1),
        'ssm_norm_g': 1.0 + nrm(ks[20], (N_EVEN, SSM_D_INNER), 0.02),
        'w_fourier': nrm(ks[21], (N_ODD, D, D), D ** -0.5),
        'w_router': nrm(ks[22], (DEPTH, D, N_EXPERTS), D ** -0.5),
        'b_router': nrm(ks[23], (DEPTH, N_EXPERTS), 0.01),
        'w_gate_up': nrm(ks[24], (DEPTH, N_EXPERTS, D, 2 * D_EXPERT), D ** -0.5),
        'b_gate_up': nrm(ks[25], (DEPTH, N_EXPERTS, 2 * D_EXPERT), 0.02),
        'w_down': nrm(ks[26], (DEPTH, N_EXPERTS, D_EXPERT, D), D_EXPERT ** -0.5),
        'b_down': nrm(ks[27], (DEPTH, N_EXPERTS, D), 0.02),
    }


def reference(x, c, ctx, c_ctx, ada_w, ada_b, norm_g, w_in, w_out, q_norm_g, k_norm_g,
              da_lambda, da_subln_g, conv_xb_w, conv_xb_b, conv_c_w, conv_c_b, dt_bias,
              a_log, d_skip, ssm_norm_g, w_fourier, w_router, b_router, w_gate_up,
              b_gate_up, w_down, b_down):
    b, s, d = x.shape
    n_grid_rows = s // GRID_W
    rows = jnp.repeat(jnp.arange(n_grid_rows, dtype=jnp.int32), GRID_W)
    cols = jnp.arange(n_grid_rows * GRID_W, dtype=jnp.int32) % GRID_W
    for i in range(DEPTH):
        even = i % 2 == 0
        ctx_later = any(j % 2 == 0 for j in range(i + 1, DEPTH))
        sx1, cx1, gx1, sx2, cx2, gx2 = [m[:, None, :] for m in ada_mod(c, ada_w[i], ada_b[i])]
        if even or ctx_later:
            sc1, cc1, gc1, sc2, cc2, gc2 = ada_mod(c_ctx, ada_w[i], ada_b[i])
            hc = modulate(rmsnorm(ctx, norm_g[i, 0]), sc1, cc1)
        hx = modulate(rmsnorm(x, norm_g[i, 0]), sx1, cx1)
        if even:
            e = i // 2
            lam_init = 0.8 - 0.6 * math.exp(-0.3 * i)
            ox, oc = even_mixer(hx, hc, w_in[e], w_out[e], q_norm_g[e], k_norm_g[e],
                                da_lambda[e], da_subln_g[e], conv_xb_w[e], conv_xb_b[e],
                                conv_c_w[e], conv_c_b[e], dt_bias[e], a_log[e], d_skip[e],
                                ssm_norm_g[e], rows, cols, lam_init, ctx_later)
        else:
            o = i // 2
            ox = fourier_mix(hx, w_fourier[o])
            oc = fourier_mix(hc, w_fourier[o]) if ctx_later else None
        x = x + gx1 * ox
        hx2 = modulate(rmsnorm(x, norm_g[i, 1]), sx2, cx2)
        moe_args = (w_router[i], b_router[i], w_gate_up[i], b_gate_up[i], w_down[i], b_down[i])
        if ctx_later:
            ctx = ctx + gc1 * oc
            hc2 = modulate(rmsnorm(ctx, norm_g[i, 1]), sc2, cc2)
            f = moe(jnp.concatenate([hx2.reshape(-1, d), hc2.reshape(-1, d)], axis=0), *moe_args)
            x = x + gx2 * f[:b * s].reshape(b, s, d)
            ctx = ctx + gc2 * f[b * s:].reshape(ctx.shape)
        else:
            x = x + gx2 * moe(hx2.reshape(-1, d), *moe_args).reshape(b, s, d)
    return x
```

```python
import functools
import math

import jax
import jax.numpy as jnp
from jax import lax
from jax.experimental import pallas as pl
from jax.experimental.pallas import tpu as pltpu

F32 = jnp.float32
BF16 = jnp.bfloat16
I32 = jnp.int32

RMS_EPS = 1e-6
GRID_W = 64
ROPE_BASE = 10000.0
DA_HEADS = 8
DA_HEAD_DIM = 64
DA_V_DIM = 128
SSM_HEADS = 16
SSM_HEAD_DIM = 64
SSM_GROUPS = 2
SSM_HPG = 8
SSM_STATE = 128
SSM_CONV = 5
SSM_CHUNK = 128
SSM_D_INNER = 1024
SSM_BC_W = 256
SSM_XB_W = 1280
FOURIER_GROUPS = 4
FOURIER_N1 = 128
N_EXPERTS = 32
TOP_K = 4
SWIGLU_LIMIT = 7.0
SWIGLU_ALPHA = 1.702
MOE_ROWS = 256
LANES = 128
NEG_BIG = -1e30
MIB = 1024 * 1024
HIGHEST = lax.Precision.HIGHEST


def _params(sem, vmem_mib=48):
    return pltpu.CompilerParams(dimension_semantics=sem, vmem_limit_bytes=vmem_mib * MIB)


def _tile(n, pref):
    t = min(n, pref)
    while n % t:
        t //= 2
    return t


def _silu(v):
    return v * jax.nn.sigmoid(v)


def _norm_mod(xf, g, shift, scale):
    r = lax.rsqrt(jnp.mean(xf * xf, axis=-1, keepdims=True) + RMS_EPS)
    return ((xf * r) * g) * (1.0 + scale) + shift


def _ada_kernel(a_ref, w_ref, b_ref, o_ref):
    s = _silu(a_ref[...])
    o_ref[...] = jnp.dot(s.astype(BF16), w_ref[...].astype(BF16),
                         preferred_element_type=F32) + b_ref[...]


def ada_all(cvecs, ada_w, ada_b):
    depth, d, n = ada_w.shape
    tn = _tile(n, 1536)
    return pl.pallas_call(
        _ada_kernel,
        grid=(depth, n // tn),
        in_specs=[pl.BlockSpec((8, d), lambda l, j: (0, 0)),
                  pl.BlockSpec((None, d, tn), lambda l, j: (l, 0, j)),
                  pl.BlockSpec((None, 1, tn), lambda l, j: (l, 0, j))],
        out_specs=pl.BlockSpec((None, 8, tn), lambda l, j: (l, 0, j)),
        out_shape=jax.ShapeDtypeStruct((depth, 8, n), F32),
        compiler_params=_params(("parallel", "parallel")),
        name="ada_mod",
    )(cvecs, ada_w, ada_b.reshape(depth, 1, n))


def _nm_panel_kernel(n_w, x_ref, g_ref, sh_ref, sc_ref, *refs):
    w_refs, o_refs = refs[:n_w], refs[n_w:2 * n_w]
    hb = _norm_mod(x_ref[...], g_ref[...], sh_ref[...], sc_ref[...]).astype(BF16)
    for w_ref, o_ref in zip(w_refs, o_refs):
        n = w_ref.shape[1]
        cw = _tile(n, 512)
        for c0 in range(0, n, cw):
            o_ref[:, c0:c0 + cw] = jnp.dot(hb, w_ref[:, c0:c0 + cw],
                                           preferred_element_type=F32).astype(o_ref.dtype)


def nm_panel(x2d, g, shift, scale, rows_per_mod, weights, out_dtypes, tm_pref=256):
    t, d = x2d.shape
    nb = shift.shape[0]
    tm = _tile(rows_per_mod, tm_pref)
    tpm = rows_per_mod // tm
    n_w = len(weights)
    in_specs = [pl.BlockSpec((tm, d), lambda i: (i, 0)),
                pl.BlockSpec((1, d), lambda i: (0, 0)),
                pl.BlockSpec((None, 1, d), lambda i: (i // tpm, 0, 0)),
                pl.BlockSpec((None, 1, d), lambda i: (i // tpm, 0, 0))]
    in_specs += [pl.BlockSpec(w.shape, lambda i: (0, 0)) for w in weights]
    out_specs = [pl.BlockSpec((tm, w.shape[1]), lambda i: (i, 0)) for w in weights]
    out_shape = [jax.ShapeDtypeStruct((t, w.shape[1]), dt) for w, dt in zip(weights, out_dtypes)]
    return pl.pallas_call(
        functools.partial(_nm_panel_kernel, n_w),
        grid=(t // tm,),
        in_specs=in_specs, out_specs=out_specs, out_shape=out_shape,
        compiler_params=_params(("parallel",), 56),
        name="norm_mod_proj",
    )(x2d, g.reshape(1, d), shift.reshape(nb, 1, d), scale.reshape(nb, 1, d), *weights)


def _qk_prep_kernel(rope, out_scale, u_ref, g_ref, seg_ref, cos_ref, sin_ref, o_ref):
    seg = seg_ref[...]
    g = g_ref[...]
    n_heads = u_ref.shape[1] // LANES
    for c in range(n_heads):
        u = u_ref[:, c * LANES:(c + 1) * LANES].astype(F32)
        u2 = u * u
        hi = u2.astype(BF16)
        lo = (u2 - hi.astype(F32)).astype(BF16)
        ss = (jnp.dot(hi, seg, preferred_element_type=F32)
              + jnp.dot(lo, seg, preferred_element_type=F32))
        nrm = (u * lax.rsqrt(ss * (1.0 / DA_HEAD_DIM) + RMS_EPS)) * g
        if rope:
            lane = lax.broadcasted_iota(I32, nrm.shape, 1)
            first = (lane % 32) < 16
            rot = jnp.where(first, -pltpu.roll(nrm, LANES - 16, 1), pltpu.roll(nrm, 16, 1))
            nrm = nrm * cos_ref[...] + rot * sin_ref[...]
        o_ref[:, c * LANES:(c + 1) * LANES] = (nrm * out_scale).astype(o_ref.dtype)


def qk_prep(u, gain, cos_t, sin_t, seq, rope, out_scale):
    t, w = u.shape
    tm = _tile(seq, 256)
    tps = seq // tm
    seg = (jnp.arange(LANES)[:, None] // DA_HEAD_DIM == jnp.arange(LANES)[None, :] // DA_HEAD_DIM)
    return pl.pallas_call(
        functools.partial(_qk_prep_kernel, rope, out_scale),
        grid=(t // tm,),
        in_specs=[pl.BlockSpec((tm, w), lambda i: (i, 0)),
                  pl.BlockSpec((1, LANES), lambda i: (0, 0)),
                  pl.BlockSpec((LANES, LANES), lambda i: (0, 0)),
                  pl.BlockSpec((tm, LANES), lambda i: (i % tps, 0)),
                  pl.BlockSpec((tm, LANES), lambda i: (i % tps, 0))],
        out_specs=pl.BlockSpec((tm, w), lambda i: (i, 0)),
        out_shape=jax.ShapeDtypeStruct((t, w), BF16),
        compiler_params=_params(("parallel",)),
        name="qk_norm_rope",
    )(u, jnp.tile(gain, 2).reshape(1, LANES), seg.astype(BF16), cos_t, sin_t)


def _attn_kernel(tk, out_mult, lam_ref, q_ref, k_ref, v_ref, g_ref, o_ref):
    q = q_ref[...]
    tq = q.shape[0]
    lane = lax.broadcasted_iota(I32, q.shape, 1)
    zero = jnp.zeros_like(q)
    qs = (jnp.where(lane < DA_HEAD_DIM, q, zero), jnp.where(lane >= DA_HEAD_DIM, q, zero))
    n_kv = k_ref.shape[0] // tk

    def body(i, carry):
        off = pl.multiple_of(i * tk, tk)
        k = k_ref[pl.ds(off, tk), :]
        v = v_ref[pl.ds(off, tk), :]
        new = []
        for c in range(2):
            m, l, acc = carry[c]
            s = lax.dot_general(qs[c], k, (((1,), (1,)), ((), ())), preferred_element_type=F32)
            m_new = jnp.maximum(m, jnp.max(s, axis=-1, keepdims=True))
            alpha = jnp.exp(m - m_new)
            p = jnp.exp(s - m_new)
            l_new = alpha * l + jnp.sum(p, axis=-1, keepdims=True)
            acc_new = alpha * acc + jnp.dot(p.astype(BF16), v, preferred_element_type=F32)
            new.append((m_new, l_new, acc_new))
        return tuple(new)

    init = tuple((jnp.full((tq, 1), -jnp.inf, F32), jnp.zeros((tq, 1), F32),
                  jnp.zeros((tq, DA_V_DIM), F32)) for _ in range(2))
    (_, l0, a0), (_, l1, a1) = lax.fori_loop(0, n_kv, body, init)
    o = a0 / l0 - lam_ref[0] * (a1 / l1)
    o = (o * lax.rsqrt(jnp.mean(o * o, axis=-1, keepdims=True) + RMS_EPS)) * g_ref[...]
    o_ref[...] = (o * out_mult).astype(o_ref.dtype)


def diff_attention(q, k_all, v_all, lam, subln_g, lam_init, batch):
    t, w = q.shape
    s = t // batch
    sk = k_all.shape[1]
    tq = _tile(s, 512)
    tk = 384 if sk % 384 == 0 else _tile(sk, 256)
    nq = s // tq
    return pl.pallas_call(
        functools.partial(_attn_kernel, tk, 1.0 - lam_init),
        grid=(batch, DA_HEADS, nq),
        in_specs=[pl.BlockSpec(memory_space=pltpu.SMEM),
                  pl.BlockSpec((tq, LANES), lambda b, h, i: (b * nq + i, h)),
                  pl.BlockSpec((None, sk, LANES), lambda b, h, i: (b, 0, h)),
                  pl.BlockSpec((None, sk, LANES), lambda b, h, i: (b, 0, h)),
                  pl.BlockSpec((1, LANES), lambda b, h, i: (0, 0))],
        out_specs=pl.BlockSpec((tq, LANES), lambda b, h, i: (b * nq + i, h)),
        out_shape=jax.ShapeDtypeStruct((t, w), BF16),
        compiler_params=_params(("parallel", "parallel", "parallel")),
        name="diff_attention",
    )(lam.reshape(1), q, k_all, v_all, subln_g.reshape(1, LANES))


HALO = 16


def _conv_kernel(tps, p_ref, c_ref, n_ref, w_ref, b_ref, o_ref, ext_ref):
    t = pl.program_id(0) % tps
    tm = c_ref.shape[0]
    prev = p_ref[...].astype(F32)
    nxt = n_ref[...].astype(F32)
    ext_ref[0:HALO, :] = jnp.where(t == 0, 0.0, prev)
    ext_ref[HALO:HALO + tm, :] = c_ref[...].astype(F32)
    ext_ref[HALO + tm:2 * HALO + tm, :] = jnp.where(t == tps - 1, 0.0, nxt)
    width = c_ref.shape[1]
    cw = _tile(width, 256)
    pad = (SSM_CONV - 1) // 2
    for c0 in range(0, width, cw):
        acc = jnp.broadcast_to(b_ref[:, c0:c0 + cw], (tm, cw))
        for k in range(SSM_CONV):
            acc = acc + w_ref[k:k + 1, c0:c0 + cw] * ext_ref[HALO - pad + k:HALO - pad + k + tm, c0:c0 + cw]
        o_ref[:, c0:c0 + cw] = _silu(acc).astype(o_ref.dtype)


def conv_silu(u, w, b, seq):
    t, c = u.shape
    tm = _tile(seq, 512)
    tps = seq // tm
    hb = tm // HALO
    last = t // HALO - 1
    return pl.pallas_call(
        functools.partial(_conv_kernel, tps),
        grid=(t // tm,),
        in_specs=[pl.BlockSpec((HALO, c), lambda i: (jnp.maximum(i * hb - 1, 0), 0)),
                  pl.BlockSpec((tm, c), lambda i: (i, 0)),
                  pl.BlockSpec((HALO, c), lambda i: (jnp.minimum((i + 1) * hb, last), 0)),
                  pl.BlockSpec((SSM_CONV, c), lambda i: (0, 0)),
                  pl.BlockSpec((1, c), lambda i: (0, 0))],
        out_specs=pl.BlockSpec((tm, c), lambda i: (i, 0)),
        out_shape=jax.ShapeDtypeStruct((t, c), BF16),
        scratch_shapes=[pltpu.VMEM((tm + 2 * HALO, c), F32)],
        compiler_params=_params(("parallel",)),
        name="dwconv_silu",
    )(u, u, u, w, b.reshape(1, c))


def _softplus(v):
    return jnp.maximum(v, 0.0) + jnp.log1p(jnp.exp(-jnp.abs(v)))


def _pair(lane_lo, a, b):
    return jnp.where(lane_lo, a, b)


def _ssd_kernel(xf_ref, dtf_ref, dtTf_ref, xb_ref, dtb_ref, dtTb_ref, bias_r, a_r, bias_c, a_c,
                dsk_ref, h0f_ref, h0b_ref, yf_ref, yb_ref, hfo_ref, hbo_ref, hf_s, hb_s):
    c = pl.program_id(1)
    q = SSM_CHUNK
    nh = SSM_HEADS

    @pl.when(c == 0)
    def _():
        hf_s[...] = h0f_ref[...]
        hb_s[...] = h0b_ref[...]

    li = lax.broadcasted_iota(I32, (q, q), 0)
    si = lax.broadcasted_iota(I32, (q, q), 1)
    lower = li >= si
    upper = li <= si
    tri_l = lower.astype(F32)
    tri_u = upper.astype(F32)
    lane_lo = lax.broadcasted_iota(I32, (q, LANES), 1) < SSM_HEAD_DIM
    lane_lo1 = lax.broadcasted_iota(I32, (1, LANES), 1) < SSM_HEAD_DIM

    def bc(col):
        return jnp.broadcast_to(col, (q, LANES))

    def state_update(h_s, g, xs32, bg, wq, edge, base):
        parts, decs = [], []
        for pr in range(SSM_HPG // 2):
            h0 = g * SSM_HPG + 2 * pr
            wp = _pair(lane_lo, bc(wq[:, base + h0:base + h0 + 1]), bc(wq[:, base + h0 + 1:base + h0 + 2]))
            parts.append((xs32[:, h0 * 64:h0 * 64 + LANES] * wp).astype(BF16))
            decs.append(_pair(lane_lo1, jnp.broadcast_to(edge[:, base + h0:base + h0 + 1], (1, LANES)),
                              jnp.broadcast_to(edge[:, base + h0 + 1:base + h0 + 2], (1, LANES))))
        xw = jnp.concatenate(parts, axis=1)
        dec = jnp.exp(jnp.concatenate(decs, axis=1))
        upd = lax.dot_general(bg, xw, (((0,), (0,)), ((), ())), preferred_element_type=F32)
        h_s[g] = h_s[g] * dec + upd

    xbc = xf_ref[...]
    xs_b = xbc[:, :SSM_D_INNER]
    xs32 = xs_b.astype(F32)
    dt = _softplus(dtf_ref[...] + bias_r[...])
    da = dt * a_r[...]
    acs = jnp.dot(tri_l, da, precision=HIGHEST, preferred_element_type=F32)
    racs = jnp.dot(tri_u, da, precision=HIGHEST, preferred_element_type=F32)
    dt_t = _softplus(dtTf_ref[...] + bias_c[...])
    da_t = dt_t * a_c[...]
    acs_t = jnp.dot(da_t, tri_u, precision=HIGHEST, preferred_element_type=F32)
    racs_t = jnp.dot(da_t, tri_l, precision=HIGHEST, preferred_element_type=F32)
    last = acs[q - 1:q, :]
    wq = jnp.exp(last - acs) * dt
    for g in range(SSM_GROUPS):
        bg = xbc[:, SSM_D_INNER + g * SSM_STATE:SSM_D_INNER + (g + 1) * SSM_STATE]
        cg = xbc[:, SSM_D_INNER + SSM_BC_W + g * SSM_STATE:SSM_D_INNER + SSM_BC_W + (g + 1) * SSM_STATE]
        cb = lax.dot_general(cg, bg, (((1,), (1,)), ((), ())), preferred_element_type=F32)
        yoff = jnp.dot(cg, hf_s[g].astype(BF16), preferred_element_type=F32)
        for pr in range(SSM_HPG // 2):
            h0 = g * SSM_HPG + 2 * pr
            col0 = h0 * SSM_HEAD_DIM
            xp = xs_b[:, col0:col0 + LANES]
            ys = []
            for hh in (h0, h0 + 1):
                mf = jnp.exp(jnp.where(lower, acs[:, hh:hh + 1] - acs_t[hh:hh + 1, :], NEG_BIG)) \
                    * dt_t[hh:hh + 1, :]
                mb = jnp.exp(jnp.where(upper, racs[:, nh + hh:nh + hh + 1] - racs_t[nh + hh:nh + hh + 1, :],
                                       NEG_BIG)) * dt_t[nh + hh:nh + hh + 1, :]
                ys.append(jnp.dot((cb * (mf + mb)).astype(BF16), xp, preferred_element_type=F32))
            ef = jnp.exp(_pair(lane_lo, bc(acs[:, h0:h0 + 1]), bc(acs[:, h0 + 1:h0 + 2])))
            yf_ref[:, col0:col0 + LANES] = (_pair(lane_lo, ys[0], ys[1])
                                            + yoff[:, 2 * pr * 64:2 * pr * 64 + LANES] * ef
                                            + xs32[:, col0:col0 + LANES] * dsk_ref[:, col0:col0 + LANES])
        state_update(hf_s, g, xs32, bg, wq, last, 0)

    xbc2 = xb_ref[...]
    xs2 = xbc2[:, :SSM_D_INNER].astype(F32)
    dt2 = _softplus(dtb_ref[...] + bias_r[...])
    racs2 = jnp.dot(tri_u, dt2 * a_r[...], precision=HIGHEST, preferred_element_type=F32)
    first = racs2[0:1, :]
    wq2 = jnp.exp(first - racs2) * dt2
    for g in range(SSM_GROUPS):
        bg = xbc2[:, SSM_D_INNER + g * SSM_STATE:SSM_D_INNER + (g + 1) * SSM_STATE]
        cg = xbc2[:, SSM_D_INNER + SSM_BC_W + g * SSM_STATE:SSM_D_INNER + SSM_BC_W + (g + 1) * SSM_STATE]
        yoff = jnp.dot(cg, hb_s[g].astype(BF16), preferred_element_type=F32)
        for pr in range(SSM_HPG // 2):
            h0 = g * SSM_HPG + 2 * pr
            col0 = h0 * SSM_HEAD_DIM
            eb = jnp.exp(_pair(lane_lo, bc(racs2[:, nh + h0:nh + h0 + 1]), bc(racs2[:, nh + h0 + 1:nh + h0 + 2])))
            yb_ref[:, col0:col0 + LANES] = yoff[:, 2 * pr * 64:2 * pr * 64 + LANES] * eb
        state_update(hb_s, g, xs2, bg, wq2, first, nh)

    @pl.when(c == pl.num_programs(1) - 1)
    def _():
        hfo_ref[...] = hf_s[...]
        hbo_ref[...] = hb_s[...]


def ssd_bidir(xbc, dt_raw, dt_bias, a_neg, d_skip, h0f, h0b, batch):
    t = xbc.shape[0]
    l = t // batch
    q = SSM_CHUNK
    nc = l // q
    w = xbc.shape[1]
    dt_t = jnp.transpose(dt_raw[:, :2 * SSM_HEADS].reshape(batch, l, 2 * SSM_HEADS), (0, 2, 1))
    pad = LANES - 2 * SSM_HEADS
    bias_r = jnp.pad(dt_bias.reshape(1, -1), ((0, 0), (0, pad)))
    a_r = jnp.pad(a_neg.reshape(1, -1), ((0, 0), (0, pad)))
    dsk = jnp.repeat(d_skip, SSM_HEAD_DIM).reshape(1, SSM_D_INNER)
    hshape = (batch, SSM_GROUPS, SSM_STATE, SSM_HPG * SSM_HEAD_DIM)
    fwd = lambda b, c: (b * nc + c, 0)
    bwd = lambda b, c: (b * nc + nc - 1 - c, 0)
    const2 = lambda b, c: (0, 0)
    hmap = lambda b, c: (b, 0, 0, 0)
    hspec = pl.BlockSpec((None,) + hshape[1:], hmap)
    return pl.pallas_call(
        _ssd_kernel,
        grid=(batch, nc),
        in_specs=[pl.BlockSpec((q, w), fwd), pl.BlockSpec((q, LANES), fwd),
                  pl.BlockSpec((None, 2 * SSM_HEADS, q), lambda b, c: (b, 0, c)),
                  pl.BlockSpec((q, w), bwd), pl.BlockSpec((q, LANES), bwd),
                  pl.BlockSpec((None, 2 * SSM_HEADS, q), lambda b, c: (b, 0, nc - 1 - c)),
                  pl.BlockSpec((1, LANES), const2), pl.BlockSpec((1, LANES), const2),
                  pl.BlockSpec((2 * SSM_HEADS, 1), const2), pl.BlockSpec((2 * SSM_HEADS, 1), const2),
                  pl.BlockSpec((1, SSM_D_INNER), const2), hspec, hspec],
        out_specs=[pl.BlockSpec((q, SSM_D_INNER), fwd), pl.BlockSpec((q, SSM_D_INNER), bwd), hspec, hspec],
        out_shape=[jax.ShapeDtypeStruct((t, SSM_D_INNER), F32), jax.ShapeDtypeStruct((t, SSM_D_INNER), F32),
                   jax.ShapeDtypeStruct(hshape, F32), jax.ShapeDtypeStruct(hshape, F32)],
        scratch_shapes=[pltpu.VMEM(hshape[1:], F32), pltpu.VMEM(hshape[1:], F32)],
        compiler_params=_params(("parallel", "arbitrary")),
        name="ssd_bidir",
    )(xbc, dt_raw, dt_t, xbc, dt_raw, dt_t, bias_r, a_r, dt_bias.reshape(-1, 1), a_neg.reshape(-1, 1),
      dsk, h0f, h0b)


def _outproj_kernel(oa_ref, yf_ref, yb_ref, z_ref, ng_ref, w1_ref, w2_ref, x_ref, gate_ref, o_ref):
    y = yf_ref[...] + yb_ref[...]
    u = y * _silu(z_ref[...].astype(F32))
    un = (u * lax.rsqrt(jnp.mean(u * u, axis=-1, keepdims=True) + RMS_EPS)) * ng_ref[...]
    acc = (jnp.dot(oa_ref[...], w1_ref[...], preferred_element_type=F32)
           + jnp.dot(un.astype(BF16), w2_ref[...], preferred_element_type=F32))
    o_ref[...] = x_ref[...] + gate_ref[...] * acc


def out_proj(o_att, yf, yb, z, norm_g, w1, w2, x2d, gate, rows_per_mod):
    t, d = x2d.shape
    tm = _tile(rows_per_mod, 512)
    tpm = rows_per_mod // tm
    nb = gate.shape[0]
    row = lambda i: (i, 0)
    const = lambda i: (0, 0)
    return pl.pallas_call(
        _outproj_kernel,
        grid=(t // tm,),
        in_specs=[pl.BlockSpec((tm, d), row), pl.BlockSpec((tm, d), row), pl.BlockSpec((tm, d), row),
                  pl.BlockSpec((tm, d), row), pl.BlockSpec((1, d), const),
                  pl.BlockSpec(w1.shape, const), pl.BlockSpec(w2.shape, const),
                  pl.BlockSpec((tm, d), row), pl.BlockSpec((None, 1, d), lambda i: (i // tpm, 0, 0))],
        out_specs=pl.BlockSpec((tm, d), row),
        out_shape=jax.ShapeDtypeStruct((t, d), F32),
        compiler_params=_params(("parallel",)),
        name="mixer_out_proj",
    )(o_att, yf, yb, z, norm_g.reshape(1, d), w1, w2, x2d, gate.reshape(nb, 1, d))


def _mm_res_kernel(a_ref, w_ref, x_ref, gate_ref, o_ref):
    acc = jnp.dot(a_ref[...], w_ref[...], preferred_element_type=F32)
    o_ref[...] = x_ref[...] + gate_ref[...] * acc


def matmul_residual(a, w, x2d, gate, rows_per_mod):
    t, d = x2d.shape
    tm = _tile(rows_per_mod, 512)
    tpm = rows_per_mod // tm
    nb = gate.shape[0]
    return pl.pallas_call(
        _mm_res_kernel,
        grid=(t // tm,),
        in_specs=[pl.BlockSpec((tm, a.shape[1]), lambda i: (i, 0)),
                  pl.BlockSpec(w.shape, lambda i: (0, 0)),
                  pl.BlockSpec((tm, d), lambda i: (i, 0)),
                  pl.BlockSpec((None, 1, d), lambda i: (i // tpm, 0, 0))],
        out_specs=pl.BlockSpec((tm, d), lambda i: (i, 0)),
        out_shape=jax.ShapeDtypeStruct((t, d), F32),
        compiler_params=_params(("parallel",)),
        name="matmul_gated_residual",
    )(a, w, x2d, gate.reshape(nb, 1, d))


def _dft_stage1_kernel(m_ref, z_ref, o_ref):
    z = z_ref[...]
    half = z.shape[1] // 2
    zs = jnp.concatenate([z[:, :half], z[:, half:]], axis=0)
    a = jnp.dot(m_ref[...], zs, preferred_element_type=F32)
    n1 = a.shape[0] // 2
    o_ref[0] = a[:n1].astype(o_ref.dtype)
    o_ref[1] = a[n1:].astype(o_ref.dtype)


def _dft_stage2_kernel(m_ref, a_ref, o_ref):
    o_ref[...] = jnp.dot(m_ref[...], a_ref[...], preferred_element_type=F32).astype(o_ref.dtype)


def _dft_tables(l, d):
    n1 = FOURIER_N1
    n2 = l // n1
    gw = d // FOURIER_GROUPS
    two_pi = 2.0 * math.pi
    ch = jnp.arange(gw, dtype=I32)
    ph = (ch[:, None] * ch[None, :]) % gw
    ang = ph.astype(F32) * (two_pi / gw)
    cc, sc = jnp.cos(ang) * gw ** -0.5, jnp.sin(ang) * gw ** -0.5
    eye = jnp.eye(FOURIER_GROUPS, dtype=F32)
    w_cs = jnp.concatenate([jnp.kron(eye, cc), -jnp.kron(eye, sc)], axis=1)
    k1 = jnp.arange(n1, dtype=I32)
    pos = jnp.arange(n2, dtype=I32)[:, None, None] + n2 * jnp.arange(n1, dtype=I32)[None, None, :]
    ang1 = ((k1[None, :, None] * pos) % l).astype(F32) * (two_pi / l)
    gc, gs = jnp.cos(ang1) * l ** -0.5, jnp.sin(ang1) * l ** -0.5
    m1 = jnp.concatenate([jnp.concatenate([gc, gs], axis=2),
                          jnp.concatenate([-gs, gc], axis=2)], axis=1)
    k2 = jnp.arange(n2, dtype=I32)
    ang2 = ((k2[:, None] * k2[None, :]) % n2).astype(F32) * (two_pi / n2)
    m2 = jnp.concatenate([jnp.cos(ang2), jnp.sin(ang2)], axis=1)
    return w_cs.astype(BF16), m1.astype(BF16), m2.astype(BF16)


def fourier_positions(z, m1, m2, batch):
    t, d2 = z.shape
    d = d2 // 2
    l = t // batch
    n1 = FOURIER_N1
    n2 = l // n1
    a = pl.pallas_call(
        _dft_stage1_kernel,
        grid=(batch, n2),
        in_specs=[pl.BlockSpec((None, 2 * n1, 2 * n1), lambda b, j: (j, 0, 0)),
                  pl.BlockSpec((None, n1, d2), lambda b, j: (b, 0, j))],
        out_specs=pl.BlockSpec((None, 2, None, n1, d), lambda b, j: (b, 0, j, 0, 0)),
        out_shape=jax.ShapeDtypeStruct((batch, 2, n2, n1, d), BF16),
        compiler_params=_params(("parallel", "parallel")),
        name="dft_stage1",
    )(m1, z.reshape(batch, n1, n2 * d2))
    cols = n1 * d
    tn = _tile(cols, 8192)
    f = pl.pallas_call(
        _dft_stage2_kernel,
        grid=(batch, cols // tn),
        in_specs=[pl.BlockSpec((n2, 2 * n2), lambda b, j: (0, 0)),
                  pl.BlockSpec((None, 2 * n2, tn), lambda b, j: (b, 0, j))],
        out_specs=pl.BlockSpec((None, n2, tn), lambda b, j: (b, 0, j)),
        out_shape=jax.ShapeDtypeStruct((batch, n2, cols), BF16),
        compiler_params=_params(("parallel", "parallel")),
        name="dft_stage2",
    )(m2, a.reshape(batch, 2 * n2, cols))
    return f.reshape(t, d)


def _route_kernel(x_ref, g_ref, sh_ref, sc_ref, wr_ref, br_ref, h_ref, ti_ref, tg_ref):
    h = _norm_mod(x_ref[...], g_ref[...], sh_ref[...], sc_ref[...])
    h_ref[...] = h.astype(BF16)
    logits = jnp.dot(h, wr_ref[...], precision=HIGHEST, preferred_element_type=F32) + br_ref[...]
    lane = lax.broadcasted_iota(I32, logits.shape, 1)
    vals, idxs = [], []
    cur = logits
    for _ in range(TOP_K):
        m = jnp.max(cur, axis=-1, keepdims=True)
        idx = jnp.min(jnp.where(cur == m, lane, LANES), axis=-1, keepdims=True)
        vals.append(m)
        idxs.append(idx)
        cur = jnp.where(lane == idx, -jnp.inf, cur)
    es = [jnp.exp(v - vals[0]) for v in vals]
    tot = es[0] + es[1] + es[2] + es[3]
    ti = jnp.zeros(logits.shape, I32)
    tg = jnp.zeros(logits.shape, F32)
    for k in range(TOP_K):
        ti = jnp.where(lane == k, idxs[k], ti)
        tg = jnp.where(lane == k, es[k] / tot, tg)
    ti_ref[...] = ti
    tg_ref[...] = tg


def route(x2d, g, shift, scale, rows_per_mod, w_router, b_router):
    t, d = x2d.shape
    nb = shift.shape[0]
    tm = _tile(rows_per_mod, 512)
    tpm = rows_per_mod // tm
    ne = w_router.shape[1]
    wr = jnp.pad(w_router, ((0, 0), (0, LANES - ne)))
    br = jnp.pad(b_router.reshape(1, ne), ((0, 0), (0, LANES - ne)), constant_values=NEG_BIG)
    return pl.pallas_call(
        _route_kernel,
        grid=(t // tm,),
        in_specs=[pl.BlockSpec((tm, d), lambda i: (i, 0)),
                  pl.BlockSpec((1, d), lambda i: (0, 0)),
                  pl.BlockSpec((None, 1, d), lambda i: (i // tpm, 0, 0)),
                  pl.BlockSpec((None, 1, d), lambda i: (i // tpm, 0, 0)),
                  pl.BlockSpec((d, LANES), lambda i: (0, 0)),
                  pl.BlockSpec((1, LANES), lambda i: (0, 0))],
        out_specs=[pl.BlockSpec((tm, d), lambda i: (i, 0)),
                   pl.BlockSpec((tm, LANES), lambda i: (i, 0)),
                   pl.BlockSpec((tm, LANES), lambda i: (i, 0))],
        out_shape=[jax.ShapeDtypeStruct((t, d), BF16), jax.ShapeDtypeStruct((t, LANES), I32),
                   jax.ShapeDtypeStruct((t, LANES), F32)],
        compiler_params=_params(("parallel",)),
        name="moe_route",
    )(x2d, g.reshape(1, d), shift.reshape(nb, 1, d), scale.reshape(nb, 1, d), wr, br)


def _gather_rows_kernel(n_tok, tb, tok_ref, src_ref, zero_ref, o_ref, sem):
    base = pl.program_id(0) * tb

    def issue(r, carry):
        tok = tok_ref[base + r]

        @pl.when(tok < n_tok)
        def _():
            pltpu.make_async_copy(src_ref.at[tok], o_ref.at[base + r], sem).start()

        @pl.when(tok >= n_tok)
        def _():
            pltpu.make_async_copy(zero_ref.at[0], o_ref.at[base + r], sem).start()

        return carry

    lax.fori_loop(0, tb, issue, 0)

    def drain(r, carry):
        pltpu.make_async_copy(zero_ref.at[0], o_ref.at[base + r], sem).wait()
        return carry

    lax.fori_loop(0, tb, drain, 0)


def gather_rows(src, row_tok):
    t, d = src.shape
    r = row_tok.shape[0]
    sub = d // LANES
    tb = MOE_ROWS
    out = pl.pallas_call(
        functools.partial(_gather_rows_kernel, t, tb),
        grid_spec=pltpu.PrefetchScalarGridSpec(
            num_scalar_prefetch=1, grid=(r // tb,),
            in_specs=[pl.BlockSpec(memory_space=pl.ANY), pl.BlockSpec(memory_space=pl.ANY)],
            out_specs=pl.BlockSpec(memory_space=pl.ANY),
            scratch_shapes=[pltpu.SemaphoreType.DMA]),
        out_shape=jax.ShapeDtypeStruct((r, sub, LANES), src.dtype),
        compiler_params=pltpu.CompilerParams(dimension_semantics=("arbitrary",), has_side_effects=True),
        name="moe_gather_rows",
    )(row_tok, src.reshape(t, sub, LANES), jnp.zeros((1, sub, LANES), src.dtype))
    return out.reshape(r, d)


def _expert_kernel(be_ref, x_ref, wgu_ref, bgu_ref, wdn_ref, bdn_ref, rg_ref, o_ref):
    de = wdn_ref.shape[0]
    gu = jnp.dot(x_ref[...], wgu_ref[...], preferred_element_type=F32) + bgu_ref[...]
    glu = jnp.minimum(gu[:, :de], SWIGLU_LIMIT)
    lin = jnp.clip(gu[:, de:], -SWIGLU_LIMIT, SWIGLU_LIMIT)
    act = glu * jax.nn.sigmoid(SWIGLU_ALPHA * glu) * (lin + 1.0)
    y = jnp.dot(act.astype(BF16), wdn_ref[...], preferred_element_type=F32) + bdn_ref[...]
    o_ref[...] = y * rg_ref[...]


def expert_ffn(xb, blk_e, w_gu, b_gu, w_dn, b_dn, row_gate):
    r, d = xb.shape
    tb = MOE_ROWS
    ne, _, de2 = w_gu.shape
    de = de2 // 2
    return pl.pallas_call(
        _expert_kernel,
        grid_spec=pltpu.PrefetchScalarGridSpec(
            num_scalar_prefetch=1, grid=(r // tb,),
            in_specs=[pl.BlockSpec((tb, d), lambda i, be: (i, 0)),
                      pl.BlockSpec((None, d, de2), lambda i, be: (be[i], 0, 0)),
                      pl.BlockSpec((None, 1, de2), lambda i, be: (be[i], 0, 0)),
                      pl.BlockSpec((None, de, d), lambda i, be: (be[i], 0, 0)),
                      pl.BlockSpec((None, 1, d), lambda i, be: (be[i], 0, 0)),
                      pl.BlockSpec((tb, 1), lambda i, be: (i, 0))],
            out_specs=pl.BlockSpec((tb, d), lambda i, be: (i, 0))),
        out_shape=jax.ShapeDtypeStruct((r, d), F32),
        compiler_params=_params(("arbitrary",)),
        name="moe_experts",
    )(blk_e, xb, w_gu, b_gu.reshape(ne, 1, de2), w_dn, b_dn.reshape(ne, 1, d), row_gate.reshape(r, 1))


def _combine_kernel(tm, dest_ref, y_ref, x_ref, gate_ref, o_ref, buf, sem):
    base = pl.program_id(0) * tm

    def issue(r, carry):
        for k in range(TOP_K):
            pltpu.make_async_copy(y_ref.at[dest_ref[(base + r) * TOP_K + k]], buf.at[k, r], sem).start()
        return carry

    lax.fori_loop(0, tm, issue, 0)

    def drain(r, carry):
        for k in range(TOP_K):
            pltpu.make_async_copy(y_ref.at[0], buf.at[k, r], sem).wait()
        return carry

    lax.fori_loop(0, tm, drain, 0)
    s = (buf[0] + buf[1]) + (buf[2] + buf[3])
    o_ref[...] = x_ref[...] + gate_ref[...] * s


def moe_combine(yb, dest, x2d, gate, rows_per_mod):
    t, d = x2d.shape
    sub = d // LANES
    nb = gate.shape[0]
    tm = _tile(rows_per_mod, 256)
    tpm = rows_per_mod // tm
    out = pl.pallas_call(
        functools.partial(_combine_kernel, tm),
        grid_spec=pltpu.PrefetchScalarGridSpec(
            num_scalar_prefetch=1, grid=(t // tm,),
            in_specs=[pl.BlockSpec(memory_space=pl.ANY),
                      pl.BlockSpec((tm, sub, LANES), lambda i, ds: (i, 0, 0)),
                      pl.BlockSpec((None, 1, sub, LANES), lambda i, ds: (i // tpm, 0, 0, 0))],
            out_specs=pl.BlockSpec((tm, sub, LANES), lambda i, ds: (i, 0, 0)),
            scratch_shapes=[pltpu.VMEM((TOP_K, tm, sub, LANES), F32), pltpu.SemaphoreType.DMA]),
        out_shape=jax.ShapeDtypeStruct((t, sub, LANES), F32),
        compiler_params=_params(("arbitrary",)),
        name="moe_combine",
    )(dest, yb.reshape(-1, sub, LANES), x2d.reshape(t, sub, LANES), gate.reshape(nb, 1, sub, LANES))
    return out.reshape(t, d)


def _moe_plan(top_i, gates, t):
    tb = MOE_ROWS
    n_asg = t * TOP_K
    e_flat = top_i.reshape(-1)
    order = jnp.argsort(e_flat)
    e_s = e_flat[order]
    counts = jnp.bincount(e_flat, length=N_EXPERTS)
    padded = (counts + tb - 1) // tb * tb
    pend = jnp.cumsum(padded)
    pstart = pend - padded
    ustart = jnp.cumsum(counts) - counts
    dest_s = (pstart[e_s] + jnp.arange(n_asg, dtype=I32) - ustart[e_s]).astype(I32)
    dest = jnp.zeros((n_asg,), I32).at[order].set(dest_s)
    n_rows = -(-(n_asg + N_EXPERTS * (tb - 1)) // tb) * tb
    row_tok = jnp.full((n_rows,), t, I32).at[dest].set(jnp.arange(n_asg, dtype=I32) // TOP_K)
    row_gate = jnp.zeros((n_rows,), F32).at[dest].set(gates.reshape(-1))
    blk_e = jnp.minimum(jnp.searchsorted(pend, jnp.arange(n_rows // tb, dtype=I32) * tb, side='right'),
                        N_EXPERTS - 1).astype(I32)
    return dest, row_tok, row_gate, blk_e


def moe_layer(x2d, norm_g, shift, scale, gate, rows_per_mod, w_router, b_router, w_gu, b_gu, w_dn, b_dn):
    t = x2d.shape[0]
    h, ti, tg = route(x2d, norm_g, shift, scale, rows_per_mod, w_router, b_router)
    dest, row_tok, row_gate, blk_e = _moe_plan(ti[:, :TOP_K], tg[:, :TOP_K], t)
    xb = gather_rows(h, row_tok)
    yb = expert_ffn(xb, blk_e, w_gu, b_gu, w_dn, b_dn, row_gate)
    return moe_combine(yb, dest, x2d, gate, rows_per_mod)


def _rope_tables(s):
    n_freq = DA_HEAD_DIM // 4
    freqs = ROPE_BASE ** (-jnp.arange(n_freq, dtype=F32) / n_freq)
    pos = jnp.arange(s, dtype=I32)
    ang_r = (pos // GRID_W).astype(F32)[:, None] * freqs
    ang_c = (pos % GRID_W).astype(F32)[:, None] * freqs
    ang = jnp.concatenate([ang_r, ang_r, ang_c, ang_c], axis=-1)
    ang = jnp.concatenate([ang, ang], axis=-1)
    return jnp.cos(ang), jnp.sin(ang)


def _even_layer(x2d, ctx2d, mods, norm_g1, w_in, w_out, q_norm_g, k_norm_g, da_lambda, da_subln_g,
                conv_xb_w, conv_xb_b, conv_c_w, conv_c_b, dt_bias, a_log, d_skip, ssm_norm_g,
                lam_init, batch):
    sx1, cx1, gx1, sc1, cc1 = mods
    t, d = x2d.shape
    s = t // batch
    ctx_len = ctx2d.shape[0] // batch
    qkw = DA_HEADS * 2 * DA_HEAD_DIM
    vw = DA_HEADS * DA_V_DIM
    col_q, col_z = 0, qkw
    col_c = col_z + SSM_D_INNER
    col_k = col_c + SSM_BC_W
    col_v = col_k + qkw
    col_xb = col_v + vw
    col_dt = col_xb + SSM_XB_W
    wb = w_in.astype(BF16)
    w_q, w_k, w_v = wb[:, col_q:col_q + qkw], wb[:, col_k:col_k + qkw], wb[:, col_v:col_v + vw]
    w_z = wb[:, col_z:col_z + SSM_D_INNER]
    w_xbc = jnp.concatenate([wb[:, col_xb:col_xb + SSM_XB_W], wb[:, col_c:col_c + SSM_BC_W]], axis=1)
    w_dt = jnp.pad(wb[:, col_dt:col_dt + 2 * SSM_HEADS], ((0, 0), (0, LANES - 2 * SSM_HEADS)))
    weights = [w_q, w_k, w_v, w_z, w_xbc, w_dt]
    dts = [BF16, BF16, BF16, BF16, BF16, F32]
    q_u, k_u, v_x, z_x, xbc_u, dt_x = nm_panel(x2d, norm_g1, sx1, cx1, s, weights, dts)
    _, kc_u, v_c, _, xbc_cu, dt_c = nm_panel(ctx2d, norm_g1, sc1, cc1, batch * ctx_len, weights, dts)

    lam = (jnp.exp(jnp.sum(da_lambda[0] * da_lambda[1])) -
           jnp.exp(jnp.sum(da_lambda[2] * da_lambda[3]))).astype(F32) + lam_init
    a_neg = -jnp.exp(a_log.astype(F32)).reshape(-1)

    cos_t, sin_t = _rope_tables(s)
    q = qk_prep(q_u, q_norm_g, cos_t, sin_t, s, True, DA_HEAD_DIM ** -0.5)
    k_x = qk_prep(k_u, k_norm_g, cos_t, sin_t, s, True, 1.0)
    k_c = qk_prep(kc_u, k_norm_g, cos_t, sin_t, ctx_len, False, 1.0)
    k_all = jnp.concatenate([k_x.reshape(batch, s, qkw), k_c.reshape(batch, ctx_len, qkw)], axis=1)
    v_all = jnp.concatenate([v_x.reshape(batch, s, vw), v_c.reshape(batch, ctx_len, vw)], axis=1)
    o_att = diff_attention(q, k_all, v_all, lam, da_subln_g, lam_init, batch)

    conv_w = jnp.concatenate([conv_xb_w, conv_c_w], axis=1)
    conv_b = jnp.concatenate([conv_xb_b, conv_c_b], axis=0)
    xbc_x = conv_silu(xbc_u, conv_w, conv_b, s)
    xbc_c = conv_silu(xbc_cu, conv_w, conv_b, ctx_len)
    hshape = (batch, SSM_GROUPS, SSM_STATE, SSM_HPG * SSM_HEAD_DIM)
    zero_h = jnp.zeros(hshape, F32)
    dt_bias_f = dt_bias.reshape(-1).astype(F32)
    _, _, h_f, h_b = ssd_bidir(xbc_c, dt_c, dt_bias_f, a_neg, d_skip, zero_h, zero_h, batch)
    y_f, y_b, _, _ = ssd_bidir(xbc_x, dt_x, dt_bias_f, a_neg, d_skip, h_f, h_b, batch)
    wo = w_out.astype(BF16)
    return out_proj(o_att, y_f, y_b, z_x, ssm_norm_g, wo[:vw], wo[vw:], x2d, gx1, s)


def _odd_layer(x2d, mods, norm_g1, w_fourier, batch):
    sx1, cx1, gx1 = mods
    t, d = x2d.shape
    s = t // batch
    w_cs, m1, m2 = _dft_tables(s, d)
    (z,) = nm_panel(x2d, norm_g1, sx1, cx1, s, [w_cs], [BF16])
    f = fourier_positions(z, m1, m2, batch)
    return matmul_residual(f, w_fourier.astype(BF16), x2d, gx1, s)


def kernel(x, c, ctx, c_ctx, ada_w, ada_b, norm_g, w_in, w_out, q_norm_g, k_norm_g, da_lambda, da_subln_g, conv_xb_w, conv_xb_b, conv_c_w, conv_c_b, dt_bias, a_log, d_skip, ssm_norm_g, w_fourier, w_router, b_router, w_gate_up, b_gate_up, w_down, b_down):
    b, s, d = x.shape
    depth = ada_w.shape[0]
    assert b + 1 <= 8
    cvecs = jnp.concatenate([c, c_ctx[None, :], jnp.zeros((8 - b - 1, d), F32)], axis=0)
    mods = ada_all(cvecs, ada_w, ada_b)
    x2d = x.reshape(b * s, d)
    ctx2d = ctx.reshape(-1, d)
    for i in range(depth):
        m = mods[i].reshape(8, 6, d)
        sx1, cx1, gx1, sx2, cx2, gx2 = [m[:b, j] for j in range(6)]
        if i % 2 == 0:
            e = i // 2
            lam_init = 0.8 - 0.6 * math.exp(-0.3 * i)
            if any(j % 2 == 0 for j in range(i + 1, depth)):
                raise NotImplementedError("context stream output is only needed for depth > 2")
            sc1, cc1 = m[b:b + 1, 0], m[b:b + 1, 1]
            x2d = _even_layer(x2d, ctx2d, (sx1, cx1, gx1, sc1, cc1), norm_g[i, 0], w_in[e], w_out[e],
                              q_norm_g[e], k_norm_g[e], da_lambda[e], da_subln_g[e], conv_xb_w[e],
                              conv_xb_b[e], conv_c_w[e], conv_c_b[e], dt_bias[e], a_log[e], d_skip[e],
                              ssm_norm_g[e], lam_init, b)
        else:
            x2d = _odd_layer(x2d, (sx1, cx1, gx1), norm_g[i, 0], w_fourier[i // 2], b)
        wgu = w_gate_up[i]
        w_gu = jnp.concatenate([wgu[..., 0::2], wgu[..., 1::2]], axis=-1).astype(BF16)
        bgu = b_gate_up[i]
        b_gu = jnp.concatenate([bgu[..., 0::2], bgu[..., 1::2]], axis=-1)
        x2d = moe_layer(x2d, norm_g[i, 1], sx2, cx2, gx2, s, w_router[i], b_router[i],
                        w_gu, b_gu, w_down[i].astype(BF16), b_down[i])
    return x2d.reshape(b, s, d)
```

```python
import functools
import math

import jax
import jax.numpy as jnp
from jax import lax
from jax.experimental import pallas as pl
from jax.experimental.pallas import tpu as pltpu

F32 = jnp.float32
BF16 = jnp.bfloat16
I32 = jnp.int32

RMS_EPS = 1e-6
GRID_W = 64
ROPE_BASE = 10000.0
DA_HEADS = 8
DA_HEAD_DIM = 64
DA_V_DIM = 128
SSM_HEADS = 16
SSM_HEAD_DIM = 64
SSM_GROUPS = 2
SSM_HPG = 8
SSM_STATE = 128
SSM_CONV = 5
SSM_CHUNK = 128
SSM_D_INNER = 1024
SSM_BC_W = 256
SSM_XB_W = 1280
FOURIER_GROUPS = 4
FOURIER_N1 = 128
N_EXPERTS = 32
TOP_K = 4
SWIGLU_LIMIT = 7.0
SWIGLU_ALPHA = 1.702
MOE_ROWS = 256
LANES = 128
NEG_BIG = -1e30
MIB = 1024 * 1024
HIGHEST = lax.Precision.HIGHEST


def _params(sem, vmem_mib=48):
    return pltpu.CompilerParams(dimension_semantics=sem, vmem_limit_bytes=vmem_mib * MIB)


def _tile(n, pref):
    t = min(n, pref)
    while n % t:
        t //= 2
    return t


def _silu(v):
    return v * jax.nn.sigmoid(v)


def _norm_mod(xf, g, shift, scale):
    r = lax.rsqrt(jnp.mean(xf * xf, axis=-1, keepdims=True) + RMS_EPS)
    return ((xf * r) * g) * (1.0 + scale) + shift


def _ada_kernel(a_ref, w_ref, b_ref, o_ref):
    s = _silu(a_ref[...])
    o_ref[...] = jnp.dot(s.astype(BF16), w_ref[...].astype(BF16),
                         preferred_element_type=F32) + b_ref[...]


def ada_all(cvecs, ada_w, ada_b):
    depth, d, n = ada_w.shape
    tn = _tile(n, 1536)
    return pl.pallas_call(
        _ada_kernel,
        grid=(depth, n // tn),
        in_specs=[pl.BlockSpec((8, d), lambda l, j: (0, 0)),
                  pl.BlockSpec((None, d, tn), lambda l, j: (l, 0, j)),
                  pl.BlockSpec((None, 1, tn), lambda l, j: (l, 0, j))],
        out_specs=pl.BlockSpec((None, 8, tn), lambda l, j: (l, 0, j)),
        out_shape=jax.ShapeDtypeStruct((depth, 8, n), F32),
        compiler_params=_params(("parallel", "parallel")),
        name="ada_mod",
    )(cvecs, ada_w, ada_b.reshape(depth, 1, n))


def _nm_panel_kernel(n_w, x_ref, g_ref, sh_ref, sc_ref, *refs):
    w_refs, o_refs = refs[:n_w], refs[n_w:2 * n_w]
    hb = _norm_mod(x_ref[...], g_ref[...], sh_ref[...], sc_ref[...]).astype(BF16)
    for w_ref, o_ref in zip(w_refs, o_refs):
        n = w_ref.shape[1]
        cw = _tile(n, 512)
        for c0 in range(0, n, cw):
            o_ref[:, c0:c0 + cw] = jnp.dot(hb, w_ref[:, c0:c0 + cw],
                                           preferred_element_type=F32).astype(o_ref.dtype)


def nm_panel(x2d, g, shift, scale, rows_per_mod, weights, out_dtypes, tm_pref=256):
    t, d = x2d.shape
    nb = shift.shape[0]
    tm = _tile(rows_per_mod, tm_pref)
    tpm = rows_per_mod // tm
    n_w = len(weights)
    in_specs = [pl.BlockSpec((tm, d), lambda i: (i, 0)),
                pl.BlockSpec((1, d), lambda i: (0, 0)),
                pl.BlockSpec((None, 1, d), lambda i: (i // tpm, 0, 0)),
                pl.BlockSpec((None, 1, d), lambda i: (i // tpm, 0, 0))]
    in_specs += [pl.BlockSpec(w.shape, lambda i: (0, 0)) for w in weights]
    out_specs = [pl.BlockSpec((tm, w.shape[1]), lambda i: (i, 0)) for w in weights]
    out_shape = [jax.ShapeDtypeStruct((t, w.shape[1]), dt) for w, dt in zip(weights, out_dtypes)]
    return pl.pallas_call(
        functools.partial(_nm_panel_kernel, n_w),
        grid=(t // tm,),
        in_specs=in_specs, out_specs=out_specs, out_shape=out_shape,
        compiler_params=_params(("parallel",), 56),
        name="norm_mod_proj",
    )(x2d, g.reshape(1, d), shift.reshape(nb, 1, d), scale.reshape(nb, 1, d), *weights)


def _qk_prep_kernel(rope, out_scale, u_ref, g_ref, seg_ref, cos_ref, sin_ref, o_ref):
    seg = seg_ref[...]
    g = g_ref[...]
    n_heads = u_ref.shape[1] // LANES
    for c in range(n_heads):
        u = u_ref[:, c * LANES:(c + 1) * LANES].astype(F32)
        u2 = u * u
        hi = u2.astype(BF16)
        lo = (u2 - hi.astype(F32)).astype(BF16)
        ss = (jnp.dot(hi, seg, preferred_element_type=F32)
              + jnp.dot(lo, seg, preferred_element_type=F32))
        nrm = (u * lax.rsqrt(ss * (1.0 / DA_HEAD_DIM) + RMS_EPS)) * g
        if rope:
            lane = lax.broadcasted_iota(I32, nrm.shape, 1)
            first = (lane % 32) < 16
            rot = jnp.where(first, -pltpu.roll(nrm, LANES - 16, 1), pltpu.roll(nrm, 16, 1))
            nrm = nrm * cos_ref[...] + rot * sin_ref[...]
        o_ref[:, c * LANES:(c + 1) * LANES] = (nrm * out_scale).astype(o_ref.dtype)


def qk_prep(u, gain, cos_t, sin_t, seq, rope, out_scale):
    t, w = u.shape
    tm = _tile(seq, 256)
    tps = seq // tm
    seg = (jnp.arange(LANES)[:, None] // DA_HEAD_DIM == jnp.arange(LANES)[None, :] // DA_HEAD_DIM)
    return pl.pallas_call(
        functools.partial(_qk_prep_kernel, rope, out_scale),
        grid=(t // tm,),
        in_specs=[pl.BlockSpec((tm, w), lambda i: (i, 0)),
                  pl.BlockSpec((1, LANES), lambda i: (0, 0)),
                  pl.BlockSpec((LANES, LANES), lambda i: (0, 0)),
                  pl.BlockSpec((tm, LANES), lambda i: (i % tps, 0)),
                  pl.BlockSpec((tm, LANES), lambda i: (i % tps, 0))],
        out_specs=pl.BlockSpec((tm, w), lambda i: (i, 0)),
        out_shape=jax.ShapeDtypeStruct((t, w), BF16),
        compiler_params=_params(("parallel",)),
        name="qk_norm_rope",
    )(u, jnp.tile(gain, 2).reshape(1, LANES), seg.astype(BF16), cos_t, sin_t)


def _attn_kernel(tk, out_mult, lam_ref, q_ref, k_ref, v_ref, g_ref, o_ref):
    q = q_ref[...]
    tq = q.shape[0]
    lane = lax.broadcasted_iota(I32, q.shape, 1)
    zero = jnp.zeros_like(q)
    qs = (jnp.where(lane < DA_HEAD_DIM, q, zero), jnp.where(lane >= DA_HEAD_DIM, q, zero))
    n_kv = k_ref.shape[0] // tk

    def body(i, carry):
        off = pl.multiple_of(i * tk, tk)
        k = k_ref[pl.ds(off, tk), :]
        v = v_ref[pl.ds(off, tk), :]
        new = []
        for c in range(2):
            m, l, acc = carry[c]
            s = lax.dot_general(qs[c], k, (((1,), (1,)), ((), ())), preferred_element_type=F32)
            m_new = jnp.maximum(m, jnp.max(s, axis=-1, keepdims=True))
            alpha = jnp.exp(m - m_new)
            p = jnp.exp(s - m_new)
            l_new = alpha * l + jnp.sum(p, axis=-1, keepdims=True)
            acc_new = alpha * acc + jnp.dot(p.astype(BF16), v, preferred_element_type=F32)
            new.append((m_new, l_new, acc_new))
        return tuple(new)

    init = tuple((jnp.full((tq, 1), -jnp.inf, F32), jnp.zeros((tq, 1), F32),
                  jnp.zeros((tq, DA_V_DIM), F32)) for _ in range(2))
    (_, l0, a0), (_, l1, a1) = lax.fori_loop(0, n_kv, body, init)
    o = a0 / l0 - lam_ref[0] * (a1 / l1)
    o = (o * lax.rsqrt(jnp.mean(o * o, axis=-1, keepdims=True) + RMS_EPS)) * g_ref[...]
    o_ref[...] = (o * out_mult).astype(o_ref.dtype)


def diff_attention(q, k_all, v_all, lam, subln_g, lam_init, batch):
    t, w = q.shape
    s = t // batch
    sk = k_all.shape[1]
    tq = _tile(s, 512)
    tk = 384 if sk % 384 == 0 else _tile(sk, 256)
    nq = s // tq
    return pl.pallas_call(
        functools.partial(_attn_kernel, tk, 1.0 - lam_init),
        grid=(batch, DA_HEADS, nq),
        in_specs=[pl.BlockSpec(memory_space=pltpu.SMEM),
                  pl.BlockSpec((tq, LANES), lambda b, h, i: (b * nq + i, h)),
                  pl.BlockSpec((None, sk, LANES), lambda b, h, i: (b, 0, h)),
                  pl.BlockSpec((None, sk, LANES), lambda b, h, i: (b, 0, h)),
                  pl.BlockSpec((1, LANES), lambda b, h, i: (0, 0))],
        out_specs=pl.BlockSpec((tq, LANES), lambda b, h, i: (b * nq + i, h)),
        out_shape=jax.ShapeDtypeStruct((t, w), BF16),
        compiler_params=_params(("parallel", "parallel", "parallel")),
        name="diff_attention",
    )(lam.reshape(1), q, k_all, v_all, subln_g.reshape(1, LANES))


HALO = 16


def _conv_kernel(tps, p_ref, c_ref, n_ref, w_ref, b_ref, o_ref, ext_ref):
    t = pl.program_id(0) % tps
    tm = c_ref.shape[0]
    prev = p_ref[...].astype(F32)
    nxt = n_ref[...].astype(F32)
    ext_ref[0:HALO, :] = jnp.where(t == 0, 0.0, prev)
    ext_ref[HALO:HALO + tm, :] = c_ref[...].astype(F32)
    ext_ref[HALO + tm:2 * HALO + tm, :] = jnp.where(t == tps - 1, 0.0, nxt)
    width = c_ref.shape[1]
    cw = _tile(width, 256)
    pad = (SSM_CONV - 1) // 2
    for c0 in range(0, width, cw):
        acc = jnp.broadcast_to(b_ref[:, c0:c0 + cw], (tm, cw))
        for k in range(SSM_CONV):
            acc = acc + w_ref[k:k + 1, c0:c0 + cw] * ext_ref[HALO - pad + k:HALO - pad + k + tm, c0:c0 + cw]
        o_ref[:, c0:c0 + cw] = _silu(acc).astype(o_ref.dtype)


def conv_silu(u, w, b, seq):
    t, c = u.shape
    tm = _tile(seq, 512)
    tps = seq // tm
    hb = tm // HALO
    last = t // HALO - 1
    return pl.pallas_call(
        functools.partial(_conv_kernel, tps),
        grid=(t // tm,),
        in_specs=[pl.BlockSpec((HALO, c), lambda i: (jnp.maximum(i * hb - 1, 0), 0)),
                  pl.BlockSpec((tm, c), lambda i: (i, 0)),
                  pl.BlockSpec((HALO, c), lambda i: (jnp.minimum((i + 1) * hb, last), 0)),
                  pl.BlockSpec((SSM_CONV, c), lambda i: (0, 0)),
                  pl.BlockSpec((1, c), lambda i: (0, 0))],
        out_specs=pl.BlockSpec((tm, c), lambda i: (i, 0)),
        out_shape=jax.ShapeDtypeStruct((t, c), BF16),
        scratch_shapes=[pltpu.VMEM((tm + 2 * HALO, c), F32)],
        compiler_params=_params(("parallel",)),
        name="dwconv_silu",
    )(u, u, u, w, b.reshape(1, c))


def _softplus(v):
    return jnp.maximum(v, 0.0) + jnp.log1p(jnp.exp(-jnp.abs(v)))


def _pair(lane_lo, a, b):
    return jnp.where(lane_lo, a, b)


def _ssd_kernel(xf_ref, dtf_ref, dtTf_ref, xb_ref, dtb_ref, dtTb_ref, bias_r, a_r, bias_c, a_c,
                dsk_ref, h0f_ref, h0b_ref, yf_ref, yb_ref, hfo_ref, hbo_ref, hf_s, hb_s):
    c = pl.program_id(1)
    q = SSM_CHUNK
    nh = SSM_HEADS

    @pl.when(c == 0)
    def _():
        hf_s[...] = h0f_ref[...]
        hb_s[...] = h0b_ref[...]

    li = lax.broadcasted_iota(I32, (q, q), 0)
    si = lax.broadcasted_iota(I32, (q, q), 1)
    lower = li >= si
    upper = li <= si
    tri_l = lower.astype(F32)
    tri_u = upper.astype(F32)
    lane_lo = lax.broadcasted_iota(I32, (q, LANES), 1) < SSM_HEAD_DIM
    lane_lo1 = lax.broadcasted_iota(I32, (1, LANES), 1) < SSM_HEAD_DIM

    def bc(col):
        return jnp.broadcast_to(col, (q, LANES))

    def state_update(h_s, g, xs32, bg, wq, edge, base):
        parts, decs = [], []
        for pr in range(SSM_HPG // 2):
            h0 = g * SSM_HPG + 2 * pr
            wp = _pair(lane_lo, bc(wq[:, base + h0:base + h0 + 1]), bc(wq[:, base + h0 + 1:base + h0 + 2]))
            parts.append((xs32[:, h0 * 64:h0 * 64 + LANES] * wp).astype(BF16))
            decs.append(_pair(lane_lo1, jnp.broadcast_to(edge[:, base + h0:base + h0 + 1], (1, LANES)),
                              jnp.broadcast_to(edge[:, base + h0 + 1:base + h0 + 2], (1, LANES))))
        xw = jnp.concatenate(parts, axis=1)
        dec = jnp.exp(jnp.concatenate(decs, axis=1))
        upd = lax.dot_general(bg, xw, (((0,), (0,)), ((), ())), preferred_element_type=F32)
        h_s[g] = h_s[g] * dec + upd

    xbc = xf_ref[...]
    xs_b = xbc[:, :SSM_D_INNER]
    xs32 = xs_b.astype(F32)
    dt = _softplus(dtf_ref[...] + bias_r[...])
    da = dt * a_r[...]
    acs = jnp.dot(tri_l, da, precision=HIGHEST, preferred_element_type=F32)
    racs = jnp.dot(tri_u, da, precision=HIGHEST, preferred_element_type=F32)
    dt_t = _softplus(dtTf_ref[...] + bias_c[...])
    da_t = dt_t * a_c[...]
    acs_t = jnp.dot(da_t, tri_u, precision=HIGHEST, preferred_element_type=F32)
    racs_t = jnp.dot(da_t, tri_l, precision=HIGHEST, preferred_element_type=F32)
    last = acs[q - 1:q, :]
    wq = jnp.exp(last - acs) * dt
    for g in range(SSM_GROUPS):
        bg = xbc[:, SSM_D_INNER + g * SSM_STATE:SSM_D_INNER + (g + 1) * SSM_STATE]
        cg = xbc[:, SSM_D_INNER + SSM_BC_W + g * SSM_STATE:SSM_D_INNER + SSM_BC_W + (g + 1) * SSM_STATE]
        cb = lax.dot_general(cg, bg, (((1,), (1,)), ((), ())), preferred_element_type=F32)
        yoff = jnp.dot(cg, hf_s[g].astype(BF16), preferred_element_type=F32)
        for pr in range(SSM_HPG // 2):
            h0 = g * SSM_HPG + 2 * pr
            col0 = h0 * SSM_HEAD_DIM
            xp = xs_b[:, col0:col0 + LANES]
            ys = []
            for hh in (h0, h0 + 1):
                mf = jnp.exp(jnp.where(lower, acs[:, hh:hh + 1] - acs_t[hh:hh + 1, :], NEG_BIG)) \
                    * dt_t[hh:hh + 1, :]
                mb = jnp.exp(jnp.where(upper, racs[:, nh + hh:nh + hh + 1] - racs_t[nh + hh:nh + hh + 1, :],
                                       NEG_BIG)) * dt_t[nh + hh:nh + hh + 1, :]
                ys.append(jnp.dot((cb * (mf + mb)).astype(BF16), xp, preferred_element_type=F32))
            ef = jnp.exp(_pair(lane_lo, bc(acs[:, h0:h0 + 1]), bc(acs[:, h0 + 1:h0 + 2])))
            yf_ref[:, col0:col0 + LANES] = (_pair(lane_lo, ys[0], ys[1])
                                            + yoff[:, 2 * pr * 64:2 * pr * 64 + LANES] * ef
                                            + xs32[:, col0:col0 + LANES] * dsk_ref[:, col0:col0 + LANES])
        state_update(hf_s, g, xs32, bg, wq, last, 0)

    xbc2 = xb_ref[...]
    xs2 = xbc2[:, :SSM_D_INNER].astype(F32)
    dt2 = _softplus(dtb_ref[...] + bias_r[...])
    racs2 = jnp.dot(tri_u, dt2 * a_r[...], precision=HIGHEST, preferred_element_type=F32)
    first = racs2[0:1, :]
    wq2 = jnp.exp(first - racs2) * dt2
    for g in range(SSM_GROUPS):
        bg = xbc2[:, SSM_D_INNER + g * SSM_STATE:SSM_D_INNER + (g + 1) * SSM_STATE]
        cg = xbc2[:, SSM_D_INNER + SSM_BC_W + g * SSM_STATE:SSM_D_INNER + SSM_BC_W + (g + 1) * SSM_STATE]
        yoff = jnp.dot(cg, hb_s[g].astype(BF16), preferred_element_type=F32)
        for pr in range(SSM_HPG // 2):
            h0 = g * SSM_HPG + 2 * pr
            col0 = h0 * SSM_HEAD_DIM
            eb = jnp.exp(_pair(lane_lo, bc(racs2[:, nh + h0:nh + h0 + 1]), bc(racs2[:, nh + h0 + 1:nh + h0 + 2])))
            yb_ref[:, col0:col0 + LANES] = yoff[:, 2 * pr * 64:2 * pr * 64 + LANES] * eb
        state_update(hb_s, g, xs2, bg, wq2, first, nh)

    @pl.when(c == pl.num_programs(1) - 1)
    def _():
        hfo_ref[...] = hf_s[...]
        hbo_ref[...] = hb_s[...]


def ssd_bidir(xbc, dt_raw, dt_bias, a_neg, d_skip, h0f, h0b, batch):
    t = xbc.shape[0]
    l = t // batch
    q = SSM_CHUNK
    nc = l // q
    w = xbc.shape[1]
    dt_t = jnp.transpose(dt_raw[:, :2 * SSM_HEADS].reshape(batch, l, 2 * SSM_HEADS), (0, 2, 1))
    pad = LANES - 2 * SSM_HEADS
    bias_r = jnp.pad(dt_bias.reshape(1, -1), ((0, 0), (0, pad)))
    a_r = jnp.pad(a_neg.reshape(1, -1), ((0, 0), (0, pad)))
    dsk = jnp.repeat(d_skip, SSM_HEAD_DIM).reshape(1, SSM_D_INNER)
    hshape = (batch, SSM_GROUPS, SSM_STATE, SSM_HPG * SSM_HEAD_DIM)
    fwd = lambda b, c: (b * nc + c, 0)
    bwd = lambda b, c: (b * nc + nc - 1 - c, 0)
    const2 = lambda b, c: (0, 0)
    hmap = lambda b, c: (b, 0, 0, 0)
    hspec = pl.BlockSpec((None,) + hshape[1:], hmap)
    return pl.pallas_call(
        _ssd_kernel,
        grid=(batch, nc),
        in_specs=[pl.BlockSpec((q, w), fwd), pl.BlockSpec((q, LANES), fwd),
                  pl.BlockSpec((None, 2 * SSM_HEADS, q), lambda b, c: (b, 0, c)),
                  pl.BlockSpec((q, w), bwd), pl.BlockSpec((q, LANES), bwd),
                  pl.BlockSpec((None, 2 * SSM_HEADS, q), lambda b, c: (b, 0, nc - 1 - c)),
                  pl.BlockSpec((1, LANES), const2), pl.BlockSpec((1, LANES), const2),
                  pl.BlockSpec((2 * SSM_HEADS, 1), const2), pl.BlockSpec((2 * SSM_HEADS, 1), const2),
                  pl.BlockSpec((1, SSM_D_INNER), const2), hspec, hspec],
        out_specs=[pl.BlockSpec((q, SSM_D_INNER), fwd), pl.BlockSpec((q, SSM_D_INNER), bwd), hspec, hspec],
        out_shape=[jax.ShapeDtypeStruct((t, SSM_D_INNER), F32), jax.ShapeDtypeStruct((t, SSM_D_INNER), F32),
                   jax.ShapeDtypeStruct(hshape, F32), jax.ShapeDtypeStruct(hshape, F32)],
        scratch_shapes=[pltpu.VMEM(hshape[1:], F32), pltpu.VMEM(hshape[1:], F32)],
        compiler_params=_params(("parallel", "arbitrary")),
        name="ssd_bidir",
    )(xbc, dt_raw, dt_t, xbc, dt_raw, dt_t, bias_r, a_r, dt_bias.reshape(-1, 1), a_neg.reshape(-1, 1),
      dsk, h0f, h0b)


def _outproj_kernel(oa_ref, yf_ref, yb_ref, z_ref, ng_ref, w1_ref, w2_ref, x_ref, gate_ref, o_ref):
    y = yf_ref[...] + yb_ref[...]
    u = y * _silu(z_ref[...].astype(F32))
    un = (u * lax.rsqrt(jnp.mean(u * u, axis=-1, keepdims=True) + RMS_EPS)) * ng_ref[...]
    acc = (jnp.dot(oa_ref[...], w1_ref[...], preferred_element_type=F32)
           + jnp.dot(un.astype(BF16), w2_ref[...], preferred_element_type=F32))
    o_ref[...] = x_ref[...] + gate_ref[...] * acc


def out_proj(o_att, yf, yb, z, norm_g, w1, w2, x2d, gate, rows_per_mod):
    t, d = x2d.shape
    tm = _tile(rows_per_mod, 512)
    tpm = rows_per_mod // tm
    nb = gate.shape[0]
    row = lambda i: (i, 0)
    const = lambda i: (0, 0)
    return pl.pallas_call(
        _outproj_kernel,
        grid=(t // tm,),
        in_specs=[pl.BlockSpec((tm, d), row), pl.BlockSpec((tm, d), row), pl.BlockSpec((tm, d), row),
                  pl.BlockSpec((tm, d), row), pl.BlockSpec((1, d), const),
                  pl.BlockSpec(w1.shape, const), pl.BlockSpec(w2.shape, const),
                  pl.BlockSpec((tm, d), row), pl.BlockSpec((None, 1, d), lambda i: (i // tpm, 0, 0))],
        out_specs=pl.BlockSpec((tm, d), row),
        out_shape=jax.ShapeDtypeStruct((t, d), F32),
        compiler_params=_params(("parallel",)),
        name="mixer_out_proj",
    )(o_att, yf, yb, z, norm_g.reshape(1, d), w1, w2, x2d, gate.reshape(nb, 1, d))


def _mm_res_kernel(a_ref, w_ref, x_ref, gate_ref, o_ref):
    acc = jnp.dot(a_ref[...], w_ref[...], preferred_element_type=F32)
    o_ref[...] = x_ref[...] + gate_ref[...] * acc


def matmul_residual(a, w, x2d, gate, rows_per_mod):
    t, d = x2d.shape
    tm = _tile(rows_per_mod, 512)
    tpm = rows_per_mod // tm
    nb = gate.shape[0]
    return pl.pallas_call(
        _mm_res_kernel,
        grid=(t // tm,),
        in_specs=[pl.BlockSpec((tm, a.shape[1]), lambda i: (i, 0)),
                  pl.BlockSpec(w.shape, lambda i: (0, 0)),
                  pl.BlockSpec((tm, d), lambda i: (i, 0)),
                  pl.BlockSpec((None, 1, d), lambda i: (i // tpm, 0, 0))],
        out_specs=pl.BlockSpec((tm, d), lambda i: (i, 0)),
        out_shape=jax.ShapeDtypeStruct((t, d), F32),
        compiler_params=_params(("parallel",)),
        name="matmul_gated_residual",
    )(a, w, x2d, gate.reshape(nb, 1, d))


def _dft_stage1_kernel(m_ref, z_ref, o_ref):
    z = z_ref[...]
    half = z.shape[1] // 2
    zs = jnp.concatenate([z[:, :half], z[:, half:]], axis=0)
    a = jnp.dot(m_ref[...], zs, preferred_element_type=F32)
    n1 = a.shape[0] // 2
    o_ref[0] = a[:n1].astype(o_ref.dtype)
    o_ref[1] = a[n1:].astype(o_ref.dtype)


def _dft_stage2_kernel(m_ref, a_ref, o_ref):
    o_ref[...] = jnp.dot(m_ref[...], a_ref[...], preferred_element_type=F32).astype(o_ref.dtype)


def _dft_tables(l, d):
    n1 = FOURIER_N1
    n2 = l // n1
    gw = d // FOURIER_GROUPS
    two_pi = 2.0 * math.pi
    ch = jnp.arange(gw, dtype=I32)
    ph = (ch[:, None] * ch[None, :]) % gw
    ang = ph.astype(F32) * (two_pi / gw)
    cc, sc = jnp.cos(ang) * gw ** -0.5, jnp.sin(ang) * gw ** -0.5
    eye = jnp.eye(FOURIER_GROUPS, dtype=F32)
    w_cs = jnp.concatenate([jnp.kron(eye, cc), -jnp.kron(eye, sc)], axis=1)
    k1 = jnp.arange(n1, dtype=I32)
    pos = jnp.arange(n2, dtype=I32)[:, None, None] + n2 * jnp.arange(n1, dtype=I32)[None, None, :]
    ang1 = ((k1[None, :, None] * pos) % l).astype(F32) * (two_pi / l)
    gc, gs = jnp.cos(ang1) * l ** -0.5, jnp.sin(ang1) * l ** -0.5
    m1 = jnp.concatenate([jnp.concatenate([gc, gs], axis=2),
                          jnp.concatenate([-gs, gc], axis=2)], axis=1)
    k2 = jnp.arange(n2, dtype=I32)
    ang2 = ((k2[:, None] * k2[None, :]) % n2).astype(F32) * (two_pi / n2)
    m2 = jnp.concatenate([jnp.cos(ang2), jnp.sin(ang2)], axis=1)
    return w_cs.astype(BF16), m1.astype(BF16), m2.astype(BF16)


def fourier_positions(z, m1, m2, batch):
    t, d2 = z.shape
    d = d2 // 2
    l = t // batch
    n1 = FOURIER_N1
    n2 = l // n1
    a = pl.pallas_call(
        _dft_stage1_kernel,
        grid=(batch, n2),
        in_specs=[pl.BlockSpec((None, 2 * n1, 2 * n1), lambda b, j: (j, 0, 0)),
                  pl.BlockSpec((None, n1, d2), lambda b, j: (b, 0, j))],
        out_specs=pl.BlockSpec((None, 2, None, n1, d), lambda b, j: (b, 0, j, 0, 0)),
        out_shape=jax.ShapeDtypeStruct((batch, 2, n2, n1, d), BF16),
        compiler_params=_params(("parallel", "parallel")),
        name="dft_stage1",
    )(m1, z.reshape(batch, n1, n2 * d2))
    cols = n1 * d
    tn = _tile(cols, 8192)
    f = pl.pallas_call(
        _dft_stage2_kernel,
        grid=(batch, cols // tn),
        in_specs=[pl.BlockSpec((n2, 2 * n2), lambda b, j: (0, 0)),
                  pl.BlockSpec((None, 2 * n2, tn), lambda b, j: (b, 0, j))],
        out_specs=pl.BlockSpec((None, n2, tn), lambda b, j: (b, 0, j)),
        out_shape=jax.ShapeDtypeStruct((batch, n2, cols), BF16),
        compiler_params=_params(("parallel", "parallel")),
        name="dft_stage2",
    )(m2, a.reshape(batch, 2 * n2, cols))
    return f.reshape(t, d)


SUB = 8


def _to_row_tiles(ref, val):
    rows = val.shape[0]
    for j in range(SUB):
        ref[pl.ds(j, rows, stride=SUB), :] = val[:, j * LANES:(j + 1) * LANES]


def _from_row_tiles(ref, rows):
    return jnp.concatenate([ref[pl.ds(j, rows, stride=SUB), :] for j in range(SUB)], axis=1)


def _row_tile(ref, row):
    return ref.at[pl.ds(pl.multiple_of(row * SUB, SUB), SUB), :]


def _route_kernel(x_ref, g_ref, sh_ref, sc_ref, wr_ref, br_ref, h_ref, ti_ref, tg_ref):
    h = _norm_mod(x_ref[...], g_ref[...], sh_ref[...], sc_ref[...])
    _to_row_tiles(h_ref, h)
    logits = jnp.dot(h, wr_ref[...], precision=HIGHEST, preferred_element_type=F32) + br_ref[...]
    lane = lax.broadcasted_iota(I32, logits.shape, 1)
    vals, idxs = [], []
    cur = logits
    for _ in range(TOP_K):
        m = jnp.max(cur, axis=-1, keepdims=True)
        idx = jnp.min(jnp.where(cur == m, lane, LANES), axis=-1, keepdims=True)
        vals.append(m)
        idxs.append(idx)
        cur = jnp.where(lane == idx, -jnp.inf, cur)
    es = [jnp.exp(v - vals[0]) for v in vals]
    tot = es[0] + es[1] + es[2] + es[3]
    ti = jnp.zeros(logits.shape, I32)
    tg = jnp.zeros(logits.shape, F32)
    for k in range(TOP_K):
        ti = jnp.where(lane == k, idxs[k], ti)
        tg = jnp.where(lane == k, es[k] / tot, tg)
    ti_ref[...] = ti
    tg_ref[...] = tg


def route(x2d, g, shift, scale, rows_per_mod, w_router, b_router):
    t, d = x2d.shape
    assert d == SUB * LANES
    nb = shift.shape[0]
    tm = _tile(rows_per_mod, 512)
    tpm = rows_per_mod // tm
    ne = w_router.shape[1]
    wr = jnp.pad(w_router, ((0, 0), (0, LANES - ne)))
    br = jnp.pad(b_router.reshape(1, ne), ((0, 0), (0, LANES - ne)), constant_values=NEG_BIG)
    return pl.pallas_call(
        _route_kernel,
        grid=(t // tm,),
        in_specs=[pl.BlockSpec((tm, d), lambda i: (i, 0)),
                  pl.BlockSpec((1, d), lambda i: (0, 0)),
                  pl.BlockSpec((None, 1, d), lambda i: (i // tpm, 0, 0)),
                  pl.BlockSpec((None, 1, d), lambda i: (i // tpm, 0, 0)),
                  pl.BlockSpec((d, LANES), lambda i: (0, 0)),
                  pl.BlockSpec((1, LANES), lambda i: (0, 0))],
        out_specs=[pl.BlockSpec((tm * SUB, LANES), lambda i: (i, 0)),
                   pl.BlockSpec((tm, LANES), lambda i: (i, 0)),
                   pl.BlockSpec((tm, LANES), lambda i: (i, 0))],
        out_shape=[jax.ShapeDtypeStruct((t * SUB, LANES), F32), jax.ShapeDtypeStruct((t, LANES), I32),
                   jax.ShapeDtypeStruct((t, LANES), F32)],
        compiler_params=_params(("parallel",)),
        name="moe_route",
    )(x2d, g.reshape(1, d), shift.reshape(nb, 1, d), scale.reshape(nb, 1, d), wr, br)


def _expert_kernel(tb, be_ref, tok_ref, nact_ref, h_hbm, wgu_ref, bgu_ref, wdn_ref, bdn_ref, rg_ref, o_ref,
                   xbuf, sem):
    i = pl.program_id(0)
    nact = nact_ref[0]

    def gather(blk, slot):
        def body(r, carry):
            tok = tok_ref[blk * tb + r]
            pltpu.make_async_copy(_row_tile(h_hbm, tok), _row_tile(xbuf.at[slot], r), sem.at[slot]).start()
            return carry
        lax.fori_loop(0, tb, body, 0, unroll=8)

    @pl.when(i == 0)
    def _():
        gather(0, 0)

    @pl.when(i + 1 < nact)
    def _():
        gather(i + 1, (i + 1) % 2)

    @pl.when(i < nact)
    def _():
        slot = i % 2
        pltpu.make_async_copy(h_hbm.at[pl.ds(0, tb * SUB), :], xbuf.at[slot], sem.at[slot]).wait()
        x = _from_row_tiles(xbuf.at[slot], tb).astype(BF16)
        gu = jnp.dot(x, wgu_ref[...], preferred_element_type=F32) + bgu_ref[...]
        even = lax.broadcasted_iota(I32, (tb, LANES), 1) % 2 == 0
        parts = []
        for c0 in range(0, gu.shape[1], LANES):
            v = gu[:, c0:c0 + LANES]
            glu = jnp.minimum(v, SWIGLU_LIMIT)
            lin = jnp.clip(v, -SWIGLU_LIMIT, SWIGLU_LIMIT) + 1.0
            act = glu * jax.nn.sigmoid(SWIGLU_ALPHA * glu) * pltpu.roll(lin, LANES - 1, 1)
            parts.append(jnp.where(even, act, 0.0).astype(BF16))
        y = jnp.dot(jnp.concatenate(parts, axis=1), wdn_ref[...], preferred_element_type=F32) + bdn_ref[...]
        _to_row_tiles(o_ref, y * rg_ref[...])

    @pl.when(i >= nact)
    def _():
        o_ref[...] = jnp.zeros(o_ref.shape, o_ref.dtype)


def expert_ffn(h_rt, row_tok, blk_e, nact, w_gu, b_gu, w_dn2, b_dn, row_gate):
    r = row_tok.shape[0]
    tb = MOE_ROWS
    ne, d, de2 = w_gu.shape
    return pl.pallas_call(
        functools.partial(_expert_kernel, tb),
        grid_spec=pltpu.PrefetchScalarGridSpec(
            num_scalar_prefetch=3, grid=(r // tb,),
            in_specs=[pl.BlockSpec(memory_space=pl.ANY),
                      pl.BlockSpec((None, d, de2), lambda i, be, tk, na: (be[i], 0, 0)),
                      pl.BlockSpec((None, 1, de2), lambda i, be, tk, na: (be[i], 0, 0)),
                      pl.BlockSpec((None, de2, d), lambda i, be, tk, na: (be[i], 0, 0)),
                      pl.BlockSpec((None, 1, d), lambda i, be, tk, na: (be[i], 0, 0)),
                      pl.BlockSpec((tb, 1), lambda i, be, tk, na: (i, 0))],
            out_specs=pl.BlockSpec((tb * SUB, LANES), lambda i, be, tk, na: (i, 0)),
            scratch_shapes=[pltpu.VMEM((2, tb * SUB, LANES), F32), pltpu.SemaphoreType.DMA((2,))]),
        out_shape=jax.ShapeDtypeStruct((r * SUB, LANES), F32),
        compiler_params=_params(("arbitrary",)),
        name="moe_experts",
    )(blk_e, row_tok, nact, h_rt, w_gu, b_gu.reshape(ne, 1, de2), w_dn2, b_dn.reshape(ne, 1, d),
      row_gate.reshape(r, 1))


def _combine_kernel(tm, dest_ref, y_hbm, x_ref, gate_ref, o_ref, buf, sem):
    base = pl.program_id(0) * tm

    def body(r, carry):
        for k in range(TOP_K):
            row = dest_ref[(base + r) * TOP_K + k]
            pltpu.make_async_copy(_row_tile(y_hbm, row), _row_tile(buf.at[k], r), sem).start()
        return carry

    lax.fori_loop(0, tm, body, 0, unroll=4)
    for k in range(TOP_K):
        pltpu.make_async_copy(y_hbm.at[pl.ds(0, tm * SUB), :], buf.at[k], sem).wait()
    buf[0] = (buf[0] + buf[1]) + (buf[2] + buf[3])
    o_ref[...] = x_ref[...] + gate_ref[...] * _from_row_tiles(buf.at[0], tm)


def moe_combine(y_rt, dest, x2d, gate, rows_per_mod):
    t, d = x2d.shape
    nb = gate.shape[0]
    tm = _tile(rows_per_mod, 256)
    tpm = rows_per_mod // tm
    return pl.pallas_call(
        functools.partial(_combine_kernel, tm),
        grid_spec=pltpu.PrefetchScalarGridSpec(
            num_scalar_prefetch=1, grid=(t // tm,),
            in_specs=[pl.BlockSpec(memory_space=pl.ANY),
                      pl.BlockSpec((tm, d), lambda i, ds: (i, 0)),
                      pl.BlockSpec((None, 1, d), lambda i, ds: (i // tpm, 0, 0))],
            out_specs=pl.BlockSpec((tm, d), lambda i, ds: (i, 0)),
            scratch_shapes=[pltpu.VMEM((TOP_K, tm * SUB, LANES), F32), pltpu.SemaphoreType.DMA]),
        out_shape=jax.ShapeDtypeStruct((t, d), F32),
        compiler_params=_params(("arbitrary",)),
        name="moe_combine",
    )(dest, y_rt, x2d, gate.reshape(nb, 1, d))


def _moe_plan(top_i, gates, t):
    tb = MOE_ROWS
    n_asg = t * TOP_K
    e_flat = top_i.reshape(-1)
    order = jnp.argsort(e_flat)
    e_s = e_flat[order]
    counts = jnp.bincount(e_flat, length=N_EXPERTS)
    padded = (counts + tb - 1) // tb * tb
    pend = jnp.cumsum(padded)
    pstart = pend - padded
    ustart = jnp.cumsum(counts) - counts
    dest_s = (pstart[e_s] + jnp.arange(n_asg, dtype=I32) - ustart[e_s]).astype(I32)
    dest = jnp.zeros((n_asg,), I32).at[order].set(dest_s)
    n_rows = -(-(n_asg + N_EXPERTS * (tb - 1)) // tb) * tb
    row_tok = jnp.zeros((n_rows,), I32).at[dest].set(jnp.arange(n_asg, dtype=I32) // TOP_K)
    row_gate = jnp.zeros((n_rows,), F32).at[dest].set(gates.reshape(-1))
    blk_e = jnp.minimum(jnp.searchsorted(pend, jnp.arange(n_rows // tb, dtype=I32) * tb, side='right'),
                        N_EXPERTS - 1).astype(I32)
    nact = (pend[-1] // tb).astype(I32).reshape(1)
    return dest, row_tok, row_gate, blk_e, nact


def moe_layer(x2d, norm_g, shift, scale, gate, rows_per_mod, w_router, b_router, w_gu, b_gu, w_dn2, b_dn):
    t = x2d.shape[0]
    h_rt, ti, tg = route(x2d, norm_g, shift, scale, rows_per_mod, w_router, b_router)
    dest, row_tok, row_gate, blk_e, nact = _moe_plan(ti[:, :TOP_K], tg[:, :TOP_K], t)
    y_rt = expert_ffn(h_rt, row_tok, blk_e, nact, w_gu, b_gu, w_dn2, b_dn, row_gate)
    return moe_combine(y_rt, dest, x2d, gate, rows_per_mod)


def _rope_tables(s):
    n_freq = DA_HEAD_DIM // 4
    freqs = ROPE_BASE ** (-jnp.arange(n_freq, dtype=F32) / n_freq)
    pos = jnp.arange(s, dtype=I32)
    ang_r = (pos // GRID_W).astype(F32)[:, None] * freqs
    ang_c = (pos % GRID_W).astype(F32)[:, None] * freqs
    ang = jnp.concatenate([ang_r, ang_r, ang_c, ang_c], axis=-1)
    ang = jnp.concatenate([ang, ang], axis=-1)
    return jnp.cos(ang), jnp.sin(ang)


def _even_layer(x2d, ctx2d, mods, norm_g1, w_in, w_out, q_norm_g, k_norm_g, da_lambda, da_subln_g,
                conv_xb_w, conv_xb_b, conv_c_w, conv_c_b, dt_bias, a_log, d_skip, ssm_norm_g,
                lam_init, batch):
    sx1, cx1, gx1, sc1, cc1 = mods
    t, d = x2d.shape
    s = t // batch
    ctx_len = ctx2d.shape[0] // batch
    qkw = DA_HEADS * 2 * DA_HEAD_DIM
    vw = DA_HEADS * DA_V_DIM
    col_q, col_z = 0, qkw
    col_c = col_z + SSM_D_INNER
    col_k = col_c + SSM_BC_W
    col_v = col_k + qkw
    col_xb = col_v + vw
    col_dt = col_xb + SSM_XB_W
    wb = w_in.astype(BF16)
    w_q, w_k, w_v = wb[:, col_q:col_q + qkw], wb[:, col_k:col_k + qkw], wb[:, col_v:col_v + vw]
    w_z = wb[:, col_z:col_z + SSM_D_INNER]
    w_xbc = jnp.concatenate([wb[:, col_xb:col_xb + SSM_XB_W], wb[:, col_c:col_c + SSM_BC_W]], axis=1)
    w_dt = jnp.pad(wb[:, col_dt:col_dt + 2 * SSM_HEADS], ((0, 0), (0, LANES - 2 * SSM_HEADS)))
    weights = [w_q, w_k, w_v, w_z, w_xbc, w_dt]
    dts = [BF16, BF16, BF16, BF16, BF16, F32]
    q_u, k_u, v_x, z_x, xbc_u, dt_x = nm_panel(x2d, norm_g1, sx1, cx1, s, weights, dts)
    _, kc_u, v_c, _, xbc_cu, dt_c = nm_panel(ctx2d, norm_g1, sc1, cc1, batch * ctx_len, weights, dts)

    lam = (jnp.exp(jnp.sum(da_lambda[0] * da_lambda[1])) -
           jnp.exp(jnp.sum(da_lambda[2] * da_lambda[3]))).astype(F32) + lam_init
    a_neg = -jnp.exp(a_log.astype(F32)).reshape(-1)

    cos_t, sin_t = _rope_tables(s)
    q = qk_prep(q_u, q_norm_g, cos_t, sin_t, s, True, DA_HEAD_DIM ** -0.5)
    k_x = qk_prep(k_u, k_norm_g, cos_t, sin_t, s, True, 1.0)
    k_c = qk_prep(kc_u, k_norm_g, cos_t, sin_t, ctx_len, False, 1.0)
    k_all = jnp.concatenate([k_x.reshape(batch, s, qkw), k_c.reshape(batch, ctx_len, qkw)], axis=1)
    v_all = jnp.concatenate([v_x.reshape(batch, s, vw), v_c.reshape(batch, ctx_len, vw)], axis=1)
    o_att = diff_attention(q, k_all, v_all, lam, da_subln_g, lam_init, batch)

    conv_w = jnp.concatenate([conv_xb_w, conv_c_w], axis=1)
    conv_b = jnp.concatenate([conv_xb_b, conv_c_b], axis=0)
    xbc_x = conv_silu(xbc_u, conv_w, conv_b, s)
    xbc_c = conv_silu(xbc_cu, conv_w, conv_b, ctx_len)
    hshape = (batch, SSM_GROUPS, SSM_STATE, SSM_HPG * SSM_HEAD_DIM)
    zero_h = jnp.zeros(hshape, F32)
    dt_bias_f = dt_bias.reshape(-1).astype(F32)
    _, _, h_f, h_b = ssd_bidir(xbc_c, dt_c, dt_bias_f, a_neg, d_skip, zero_h, zero_h, batch)
    y_f, y_b, _, _ = ssd_bidir(xbc_x, dt_x, dt_bias_f, a_neg, d_skip, h_f, h_b, batch)
    wo = w_out.astype(BF16)
    return out_proj(o_att, y_f, y_b, z_x, ssm_norm_g, wo[:vw], wo[vw:], x2d, gx1, s)


def _odd_layer(x2d, mods, norm_g1, w_fourier, batch):
    sx1, cx1, gx1 = mods
    t, d = x2d.shape
    s = t // batch
    w_cs, m1, m2 = _dft_tables(s, d)
    (z,) = nm_panel(x2d, norm_g1, sx1, cx1, s, [w_cs], [BF16])
    f = fourier_positions(z, m1, m2, batch)
    return matmul_residual(f, w_fourier.astype(BF16), x2d, gx1, s)


def kernel(x, c, ctx, c_ctx, ada_w, ada_b, norm_g, w_in, w_out, q_norm_g, k_norm_g, da_lambda, da_subln_g, conv_xb_w, conv_xb_b, conv_c_w, conv_c_b, dt_bias, a_log, d_skip, ssm_norm_g, w_fourier, w_router, b_router, w_gate_up, b_gate_up, w_down, b_down):
    b, s, d = x.shape
    depth = ada_w.shape[0]
    assert b + 1 <= 8
    cvecs = jnp.concatenate([c, c_ctx[None, :], jnp.zeros((8 - b - 1, d), F32)], axis=0)
    mods = ada_all(cvecs, ada_w, ada_b)
    x2d = x.reshape(b * s, d)
    ctx2d = ctx.reshape(-1, d)
    for i in range(depth):
        m = mods[i].reshape(8, 6, d)
        sx1, cx1, gx1, sx2, cx2, gx2 = [m[:b, j] for j in range(6)]
        if i % 2 == 0:
            e = i // 2
            lam_init = 0.8 - 0.6 * math.exp(-0.3 * i)
            if any(j % 2 == 0 for j in range(i + 1, depth)):
                raise NotImplementedError("context stream output is only needed for depth > 2")
            sc1, cc1 = m[b:b + 1, 0], m[b:b + 1, 1]
            x2d = _even_layer(x2d, ctx2d, (sx1, cx1, gx1, sc1, cc1), norm_g[i, 0], w_in[e], w_out[e],
                              q_norm_g[e], k_norm_g[e], da_lambda[e], da_subln_g[e], conv_xb_w[e],
                              conv_xb_b[e], conv_c_w[e], conv_c_b[e], dt_bias[e], a_log[e], d_skip[e],
                              ssm_norm_g[e], lam_init, b)
        else:
            x2d = _odd_layer(x2d, (sx1, cx1, gx1), norm_g[i, 0], w_fourier[i // 2], b)
        wdn = w_down[i].astype(BF16)
        w_dn2 = jnp.stack([wdn, jnp.zeros_like(wdn)], axis=2).reshape(wdn.shape[0], -1, d)
        x2d = moe_layer(x2d, norm_g[i, 1], sx2, cx2, gx2, s, w_router[i], b_router[i],
                        w_gate_up[i].astype(BF16), b_gate_up[i], w_dn2, b_down[i])
    return x2d.reshape(b, s, d)
```

```python
import functools
import math

import jax
import jax.numpy as jnp
from jax import lax
from jax.experimental import pallas as pl
from jax.experimental.pallas import tpu as pltpu

F32 = jnp.float32
BF16 = jnp.bfloat16
I32 = jnp.int32

RMS_EPS = 1e-6
GRID_W = 64
ROPE_BASE = 10000.0
DA_HEADS = 8
DA_HEAD_DIM = 64
DA_V_DIM = 128
SSM_HEADS = 16
SSM_HEAD_DIM = 64
SSM_GROUPS = 2
SSM_HPG = 8
SSM_STATE = 128
SSM_CONV = 5
SSM_CHUNK = 128
SSM_D_INNER = 1024
SSM_BC_W = 256
SSM_XB_W = 1280
FOURIER_GROUPS = 4
FOURIER_N1 = 128
N_EXPERTS = 32
TOP_K = 4
SWIGLU_LIMIT = 7.0
SWIGLU_ALPHA = 1.702
MOE_ROWS = 256
LANES = 128
NEG_BIG = -1e30
MIB = 1024 * 1024
HIGHEST = lax.Precision.HIGHEST


def _params(sem, vmem_mib=48):
    return pltpu.CompilerParams(dimension_semantics=sem, vmem_limit_bytes=vmem_mib * MIB)


def _tile(n, pref):
    t = min(n, pref)
    while n % t:
        t //= 2
    return t


def _silu(v):
    return v * jax.nn.sigmoid(v)


def _norm_mod(xf, g, shift, scale):
    r = lax.rsqrt(jnp.mean(xf * xf, axis=-1, keepdims=True) + RMS_EPS)
    return ((xf * r) * g) * (1.0 + scale) + shift


def _ada_kernel(a_ref, w_ref, b_ref, o_ref):
    s = _silu(a_ref[...])
    o_ref[...] = jnp.dot(s.astype(BF16), w_ref[...].astype(BF16),
                         preferred_element_type=F32) + b_ref[...]


def ada_all(cvecs, ada_w, ada_b):
    depth, d, n = ada_w.shape
    tn = _tile(n, 1536)
    return pl.pallas_call(
        _ada_kernel,
        grid=(depth, n // tn),
        in_specs=[pl.BlockSpec((8, d), lambda l, j: (0, 0)),
                  pl.BlockSpec((None, d, tn), lambda l, j: (l, 0, j)),
                  pl.BlockSpec((None, 1, tn), lambda l, j: (l, 0, j))],
        out_specs=pl.BlockSpec((None, 8, tn), lambda l, j: (l, 0, j)),
        out_shape=jax.ShapeDtypeStruct((depth, 8, n), F32),
        compiler_params=_params(("parallel", "parallel")),
        name="ada_mod",
    )(cvecs, ada_w, ada_b.reshape(depth, 1, n))


def _nm_panel_kernel(n_w, x_ref, g_ref, sh_ref, sc_ref, *refs):
    w_refs, o_refs = refs[:n_w], refs[n_w:2 * n_w]
    hb = _norm_mod(x_ref[...], g_ref[...], sh_ref[...], sc_ref[...]).astype(BF16)
    for w_ref, o_ref in zip(w_refs, o_refs):
        n = w_ref.shape[1]
        cw = _tile(n, 512)
        for c0 in range(0, n, cw):
            o_ref[:, c0:c0 + cw] = jnp.dot(hb, w_ref[:, c0:c0 + cw],
                                           preferred_element_type=F32).astype(o_ref.dtype)


def nm_panel(x2d, g, shift, scale, rows_per_mod, weights, out_dtypes, tm_pref=256):
    t, d = x2d.shape
    nb = shift.shape[0]
    tm = _tile(rows_per_mod, tm_pref)
    tpm = rows_per_mod // tm
    n_w = len(weights)
    in_specs = [pl.BlockSpec((tm, d), lambda i: (i, 0)),
                pl.BlockSpec((1, d), lambda i: (0, 0)),
                pl.BlockSpec((None, 1, d), lambda i: (i // tpm, 0, 0)),
                pl.BlockSpec((None, 1, d), lambda i: (i // tpm, 0, 0))]
    in_specs += [pl.BlockSpec(w.shape, lambda i: (0, 0)) for w in weights]
    out_specs = [pl.BlockSpec((tm, w.shape[1]), lambda i: (i, 0)) for w in weights]
    out_shape = [jax.ShapeDtypeStruct((t, w.shape[1]), dt) for w, dt in zip(weights, out_dtypes)]
    return pl.pallas_call(
        functools.partial(_nm_panel_kernel, n_w),
        grid=(t // tm,),
        in_specs=in_specs, out_specs=out_specs, out_shape=out_shape,
        compiler_params=_params(("parallel",), 56),
        name="norm_mod_proj",
    )(x2d, g.reshape(1, d), shift.reshape(nb, 1, d), scale.reshape(nb, 1, d), *weights)


def _qk_prep_kernel(rope, out_scale, u_ref, g_ref, seg_ref, cos_ref, sin_ref, o_ref):
    seg = seg_ref[...]
    g = g_ref[...]
    n_heads = u_ref.shape[1] // LANES
    for c in range(n_heads):
        u = u_ref[:, c * LANES:(c + 1) * LANES].astype(F32)
        u2 = u * u
        hi = u2.astype(BF16)
        lo = (u2 - hi.astype(F32)).astype(BF16)
        ss = (jnp.dot(hi, seg, preferred_element_type=F32)
              + jnp.dot(lo, seg, preferred_element_type=F32))
        nrm = (u * lax.rsqrt(ss * (1.0 / DA_HEAD_DIM) + RMS_EPS)) * g
        if rope:
            lane = lax.broadcasted_iota(I32, nrm.shape, 1)
            first = (lane % 32) < 16
            rot = jnp.where(first, -pltpu.roll(nrm, LANES - 16, 1), pltpu.roll(nrm, 16, 1))
            nrm = nrm * cos_ref[...] + rot * sin_ref[...]
        o_ref[:, c * LANES:(c + 1) * LANES] = (nrm * out_scale).astype(o_ref.dtype)


def qk_prep(u, gain, cos_t, sin_t, seq, rope, out_scale):
    t, w = u.shape
    tm = _tile(seq, 256)
    tps = seq // tm
    seg = (jnp.arange(LANES)[:, None] // DA_HEAD_DIM == jnp.arange(LANES)[None, :] // DA_HEAD_DIM)
    return pl.pallas_call(
        functools.partial(_qk_prep_kernel, rope, out_scale),
        grid=(t // tm,),
        in_specs=[pl.BlockSpec((tm, w), lambda i: (i, 0)),
                  pl.BlockSpec((1, LANES), lambda i: (0, 0)),
                  pl.BlockSpec((LANES, LANES), lambda i: (0, 0)),
                  pl.BlockSpec((tm, LANES), lambda i: (i % tps, 0)),
                  pl.BlockSpec((tm, LANES), lambda i: (i % tps, 0))],
        out_specs=pl.BlockSpec((tm, w), lambda i: (i, 0)),
        out_shape=jax.ShapeDtypeStruct((t, w), BF16),
        compiler_params=_params(("parallel",)),
        name="qk_norm_rope",
    )(u, jnp.tile(gain, 2).reshape(1, LANES), seg.astype(BF16), cos_t, sin_t)


LOG2E = math.log2(math.e)
ATTN_SHIFT_LIMIT = 60.0


def _split_components(q):
    lane = lax.broadcasted_iota(I32, q.shape, 1)
    zero = jnp.zeros_like(q)
    return jnp.where(lane < DA_HEAD_DIM, q, zero), jnp.where(lane >= DA_HEAD_DIM, q, zero)


def _attn_finish(n0, l0, n1, l1, lam, g, out_mult, o_ref):
    o = n0 / l0 - lam * (n1 / l1)
    o = (o * lax.rsqrt(jnp.mean(o * o, axis=-1, keepdims=True) + RMS_EPS)) * g
    o_ref[...] = (o * out_mult).astype(o_ref.dtype)


def _attn_online_kernel(tk, out_mult, sc_ref, q_ref, k_ref, v_ref, g_ref, o_ref):
    qs = _split_components(q_ref[...])
    tq = q_ref.shape[0]
    n_kv = k_ref.shape[0] // tk

    def body(i, carry):
        off = pl.multiple_of(i * tk, tk)
        k = k_ref[pl.ds(off, tk), :]
        v = v_ref[pl.ds(off, tk), :]
        new = []
        for c in range(2):
            m, l, acc = carry[c]
            s = lax.dot_general(qs[c], k, (((1,), (1,)), ((), ())), preferred_element_type=F32)
            m_new = jnp.maximum(m, jnp.max(s, axis=-1, keepdims=True))
            alpha = jnp.exp2(m - m_new)
            p = jnp.exp2(s - m_new)
            l_new = alpha * l + jnp.sum(p, axis=-1, keepdims=True)
            acc_new = alpha * acc + jnp.dot(p.astype(BF16), v, preferred_element_type=F32)
            new.append((m_new, l_new, acc_new))
        return tuple(new)

    init = tuple((jnp.full((tq, 1), -jnp.inf, F32), jnp.zeros((tq, 1), F32),
                  jnp.zeros((tq, DA_V_DIM), F32)) for _ in range(2))
    (_, l0, a0), (_, l1, a1) = lax.fori_loop(0, n_kv, body, init)
    _attn_finish(a0, l0, a1, l1, sc_ref[0], g_ref[...], out_mult, o_ref)


def _attn_shift_kernel(tk, out_mult, sc_ref, q_ref, k_ref, v_ref, g_ref, o_ref):
    qs = _split_components(q_ref[...])
    tq = q_ref.shape[0]
    n_kv = k_ref.shape[0] // tk
    shift = sc_ref[1]
    ones = jnp.ones((tk, LANES), BF16)

    def body(i, carry):
        off = pl.multiple_of(i * tk, tk)
        k = k_ref[pl.ds(off, tk), :]
        va = jnp.concatenate([v_ref[pl.ds(off, tk), :], ones], axis=1)
        new = []
        for c in range(2):
            s = lax.dot_general(qs[c], k, (((1,), (1,)), ((), ())), preferred_element_type=F32)
            p = jnp.exp2(s - shift).astype(BF16)
            new.append(carry[c] + jnp.dot(p, va, preferred_element_type=F32))
        return tuple(new)

    init = (jnp.zeros((tq, 2 * LANES), F32), jnp.zeros((tq, 2 * LANES), F32))
    a0, a1 = lax.fori_loop(0, n_kv, body, init)
    _attn_finish(a0[:, :LANES], a0[:, LANES:], a1[:, :LANES], a1[:, LANES:], sc_ref[0], g_ref[...], out_mult,
                 o_ref)


def diff_attention(q, k_all, v_all, lam, score_bound, subln_g, lam_init, batch):
    t, w = q.shape
    s = t // batch
    sk = k_all.shape[1]
    tq = _tile(s, 512)
    tk = 768 if sk % 768 == 0 else _tile(sk, 256)
    nq = s // tq
    scalars = jnp.stack([lam, score_bound]).astype(F32)

    def run(body):
        return pl.pallas_call(
            functools.partial(body, tk, 1.0 - lam_init),
            grid=(batch, DA_HEADS, nq),
            in_specs=[pl.BlockSpec(memory_space=pltpu.SMEM),
                      pl.BlockSpec((tq, LANES), lambda b, h, i: (b * nq + i, h)),
                      pl.BlockSpec((None, sk, LANES), lambda b, h, i: (b, 0, h)),
                      pl.BlockSpec((None, sk, LANES), lambda b, h, i: (b, 0, h)),
                      pl.BlockSpec((1, LANES), lambda b, h, i: (0, 0))],
            out_specs=pl.BlockSpec((tq, LANES), lambda b, h, i: (b * nq + i, h)),
            out_shape=jax.ShapeDtypeStruct((t, w), BF16),
            compiler_params=_params(("parallel", "parallel", "parallel")),
            name="diff_attention",
        )(scalars, q, k_all, v_all, subln_g.reshape(1, LANES))

    return lax.cond(score_bound <= ATTN_SHIFT_LIMIT,
                    lambda: run(_attn_shift_kernel), lambda: run(_attn_online_kernel))


HALO = 16


def _conv_kernel(tps, p_ref, c_ref, n_ref, w_ref, b_ref, o_ref, ext_ref):
    t = pl.program_id(0) % tps
    tm = c_ref.shape[0]
    prev = p_ref[...].astype(F32)
    nxt = n_ref[...].astype(F32)
    ext_ref[0:HALO, :] = jnp.where(t == 0, 0.0, prev)
    ext_ref[HALO:HALO + tm, :] = c_ref[...].astype(F32)
    ext_ref[HALO + tm:2 * HALO + tm, :] = jnp.where(t == tps - 1, 0.0, nxt)
    width = c_ref.shape[1]
    cw = _tile(width, 256)
    pad = (SSM_CONV - 1) // 2
    for c0 in range(0, width, cw):
        acc = jnp.broadcast_to(b_ref[:, c0:c0 + cw], (tm, cw))
        for k in range(SSM_CONV):
            acc = acc + w_ref[k:k + 1, c0:c0 + cw] * ext_ref[HALO - pad + k:HALO - pad + k + tm, c0:c0 + cw]
        o_ref[:, c0:c0 + cw] = _silu(acc).astype(o_ref.dtype)


def conv_silu(u, w, b, seq):
    t, c = u.shape
    tm = _tile(seq, 512)
    tps = seq // tm
    hb = tm // HALO
    last = t // HALO - 1
    return pl.pallas_call(
        functools.partial(_conv_kernel, tps),
        grid=(t // tm,),
        in_specs=[pl.BlockSpec((HALO, c), lambda i: (jnp.maximum(i * hb - 1, 0), 0)),
                  pl.BlockSpec((tm, c), lambda i: (i, 0)),
                  pl.BlockSpec((HALO, c), lambda i: (jnp.minimum((i + 1) * hb, last), 0)),
                  pl.BlockSpec((SSM_CONV, c), lambda i: (0, 0)),
                  pl.BlockSpec((1, c), lambda i: (0, 0))],
        out_specs=pl.BlockSpec((tm, c), lambda i: (i, 0)),
        out_shape=jax.ShapeDtypeStruct((t, c), BF16),
        scratch_shapes=[pltpu.VMEM((tm + 2 * HALO, c), F32)],
        compiler_params=_params(("parallel",)),
        name="dwconv_silu",
    )(u, u, u, w, b.reshape(1, c))


def _softplus(v):
    return jnp.maximum(v, 0.0) + jnp.log1p(jnp.exp(-jnp.abs(v)))


def _pair(lane_lo, a, b):
    return jnp.where(lane_lo, a, b)


def _ssd_kernel(xf_ref, dtf_ref, dtTf_ref, xb_ref, dtb_ref, dtTb_ref, bias_r, a_r, bias_c, a_c,
                dsk_ref, h0f_ref, h0b_ref, yf_ref, yb_ref, hfo_ref, hbo_ref, hf_s, hb_s):
    c = pl.program_id(1)
    q = SSM_CHUNK
    nh = SSM_HEADS

    @pl.when(c == 0)
    def _():
        hf_s[...] = h0f_ref[...]
        hb_s[...] = h0b_ref[...]

    li = lax.broadcasted_iota(I32, (q, q), 0)
    si = lax.broadcasted_iota(I32, (q, q), 1)
    lower = li >= si
    upper = li <= si
    tri_l = lower.astype(F32)
    tri_u = upper.astype(F32)
    lane_lo = lax.broadcasted_iota(I32, (q, LANES), 1) < SSM_HEAD_DIM
    lane_lo1 = lax.broadcasted_iota(I32, (1, LANES), 1) < SSM_HEAD_DIM

    def bc(col):
        return jnp.broadcast_to(col, (q, LANES))

    def state_update(h_s, g, xs32, bg, wq, edge, base):
        parts, decs = [], []
        for pr in range(SSM_HPG // 2):
            h0 = g * SSM_HPG + 2 * pr
            wp = _pair(lane_lo, bc(wq[:, base + h0:base + h0 + 1]), bc(wq[:, base + h0 + 1:base + h0 + 2]))
            parts.append((xs32[:, h0 * 64:h0 * 64 + LANES] * wp).astype(BF16))
            decs.append(_pair(lane_lo1, jnp.broadcast_to(edge[:, base + h0:base + h0 + 1], (1, LANES)),
                              jnp.broadcast_to(edge[:, base + h0 + 1:base + h0 + 2], (1, LANES))))
        xw = jnp.concatenate(parts, axis=1)
        dec = jnp.exp(jnp.concatenate(decs, axis=1))
        upd = lax.dot_general(bg, xw, (((0,), (0,)), ((), ())), preferred_element_type=F32)
        h_s[g] = h_s[g] * dec + upd

    xbc = xf_ref[...]
    xs_b = xbc[:, :SSM_D_INNER]
    xs32 = xs_b.astype(F32)
    dt = _softplus(dtf_ref[...] + bias_r[...])
    da = dt * a_r[...]
    acs = jnp.dot(tri_l, da, precision=HIGHEST, preferred_element_type=F32)
    racs = jnp.dot(tri_u, da, precision=HIGHEST, preferred_element_type=F32)
    dt_t = _softplus(dtTf_ref[...] + bias_c[...])
    da_t = dt_t * a_c[...]
    acs_t = jnp.dot(da_t, tri_u, precision=HIGHEST, preferred_element_type=F32)
    racs_t = jnp.dot(da_t, tri_l, precision=HIGHEST, preferred_element_type=F32)
    last = acs[q - 1:q, :]
    wq = jnp.exp(last - acs) * dt
    for g in range(SSM_GROUPS):
        bg = xbc[:, SSM_D_INNER + g * SSM_STATE:SSM_D_INNER + (g + 1) * SSM_STATE]
        cg = xbc[:, SSM_D_INNER + SSM_BC_W + g * SSM_STATE:SSM_D_INNER + SSM_BC_W + (g + 1) * SSM_STATE]
        cb = lax.dot_general(cg, bg, (((1,), (1,)), ((), ())), preferred_element_type=F32)
        yoff = jnp.dot(cg, hf_s[g].astype(BF16), preferred_element_type=F32)
        for pr in range(SSM_HPG // 2):
            h0 = g * SSM_HPG + 2 * pr
            col0 = h0 * SSM_HEAD_DIM
            xp = xs_b[:, col0:col0 + LANES]
            ys = []
            for hh in (h0, h0 + 1):
                mf = jnp.exp(jnp.where(lower, acs[:, hh:hh + 1] - acs_t[hh:hh + 1, :], NEG_BIG)) \
                    * dt_t[hh:hh + 1, :]
                mb = jnp.exp(jnp.where(upper, racs[:, nh + hh:nh + hh + 1] - racs_t[nh + hh:nh + hh + 1, :],
                                       NEG_BIG)) * dt_t[nh + hh:nh + hh + 1, :]
                ys.append(jnp.dot((cb * (mf + mb)).astype(BF16), xp, preferred_element_type=F32))
            ef = jnp.exp(_pair(lane_lo, bc(acs[:, h0:h0 + 1]), bc(acs[:, h0 + 1:h0 + 2])))
            yf_ref[:, col0:col0 + LANES] = (_pair(lane_lo, ys[0], ys[1])
                                            + yoff[:, 2 * pr * 64:2 * pr * 64 + LANES] * ef
                                            + xs32[:, col0:col0 + LANES] * dsk_ref[:, col0:col0 + LANES])
        state_update(hf_s, g, xs32, bg, wq, last, 0)

    xbc2 = xb_ref[...]
    xs2 = xbc2[:, :SSM_D_INNER].astype(F32)
    dt2 = _softplus(dtb_ref[...] + bias_r[...])
    racs2 = jnp.dot(tri_u, dt2 * a_r[...], precision=HIGHEST, preferred_element_type=F32)
    first = racs2[0:1, :]
    wq2 = jnp.exp(first - racs2) * dt2
    for g in range(SSM_GROUPS):
        bg = xbc2[:, SSM_D_INNER + g * SSM_STATE:SSM_D_INNER + (g + 1) * SSM_STATE]
        cg = xbc2[:, SSM_D_INNER + SSM_BC_W + g * SSM_STATE:SSM_D_INNER + SSM_BC_W + (g + 1) * SSM_STATE]
        yoff = jnp.dot(cg, hb_s[g].astype(BF16), preferred_element_type=F32)
        for pr in range(SSM_HPG // 2):
            h0 = g * SSM_HPG + 2 * pr
            col0 = h0 * SSM_HEAD_DIM
            eb = jnp.exp(_pair(lane_lo, bc(racs2[:, nh + h0:nh + h0 + 1]), bc(racs2[:, nh + h0 + 1:nh + h0 + 2])))
            yb_ref[:, col0:col0 + LANES] = yoff[:, 2 * pr * 64:2 * pr * 64 + LANES] * eb
        state_update(hb_s, g, xs2, bg, wq2, first, nh)

    @pl.when(c == pl.num_programs(1) - 1)
    def _():
        hfo_ref[...] = hf_s[...]
        hbo_ref[...] = hb_s[...]


def ssd_bidir(xbc, dt_raw, dt_bias, a_neg, d_skip, h0f, h0b, batch):
    t = xbc.shape[0]
    l = t // batch
    q = SSM_CHUNK
    nc = l // q
    w = xbc.shape[1]
    dt_t = jnp.transpose(dt_raw[:, :2 * SSM_HEADS].reshape(batch, l, 2 * SSM_HEADS), (0, 2, 1))
    pad = LANES - 2 * SSM_HEADS
    bias_r = jnp.pad(dt_bias.reshape(1, -1), ((0, 0), (0, pad)))
    a_r = jnp.pad(a_neg.reshape(1, -1), ((0, 0), (0, pad)))
    dsk = jnp.repeat(d_skip, SSM_HEAD_DIM).reshape(1, SSM_D_INNER)
    hshape = (batch, SSM_GROUPS, SSM_STATE, SSM_HPG * SSM_HEAD_DIM)
    fwd = lambda b, c: (b * nc + c, 0)
    bwd = lambda b, c: (b * nc + nc - 1 - c, 0)
    const2 = lambda b, c: (0, 0)
    hmap = lambda b, c: (b, 0, 0, 0)
    hspec = pl.BlockSpec((None,) + hshape[1:], hmap)
    return pl.pallas_call(
        _ssd_kernel,
        grid=(batch, nc),
        in_specs=[pl.BlockSpec((q, w), fwd), pl.BlockSpec((q, LANES), fwd),
                  pl.BlockSpec((None, 2 * SSM_HEADS, q), lambda b, c: (b, 0, c)),
                  pl.BlockSpec((q, w), bwd), pl.BlockSpec((q, LANES), bwd),
                  pl.BlockSpec((None, 2 * SSM_HEADS, q), lambda b, c: (b, 0, nc - 1 - c)),
                  pl.BlockSpec((1, LANES), const2), pl.BlockSpec((1, LANES), const2),
                  pl.BlockSpec((2 * SSM_HEADS, 1), const2), pl.BlockSpec((2 * SSM_HEADS, 1), const2),
                  pl.BlockSpec((1, SSM_D_INNER), const2), hspec, hspec],
        out_specs=[pl.BlockSpec((q, SSM_D_INNER), fwd), pl.BlockSpec((q, SSM_D_INNER), bwd), hspec, hspec],
        out_shape=[jax.ShapeDtypeStruct((t, SSM_D_INNER), F32), jax.ShapeDtypeStruct((t, SSM_D_INNER), F32),
                   jax.ShapeDtypeStruct(hshape, F32), jax.ShapeDtypeStruct(hshape, F32)],
        scratch_shapes=[pltpu.VMEM(hshape[1:], F32), pltpu.VMEM(hshape[1:], F32)],
        compiler_params=_params(("parallel", "arbitrary")),
        name="ssd_bidir",
    )(xbc, dt_raw, dt_t, xbc, dt_raw, dt_t, bias_r, a_r, dt_bias.reshape(-1, 1), a_neg.reshape(-1, 1),
      dsk, h0f, h0b)


def _outproj_kernel(oa_ref, yf_ref, yb_ref, z_ref, ng_ref, w1_ref, w2_ref, x_ref, gate_ref, o_ref):
    y = yf_ref[...] + yb_ref[...]
    u = y * _silu(z_ref[...].astype(F32))
    un = (u * lax.rsqrt(jnp.mean(u * u, axis=-1, keepdims=True) + RMS_EPS)) * ng_ref[...]
    acc = (jnp.dot(oa_ref[...], w1_ref[...], preferred_element_type=F32)
           + jnp.dot(un.astype(BF16), w2_ref[...], preferred_element_type=F32))
    o_ref[...] = x_ref[...] + gate_ref[...] * acc


def out_proj(o_att, yf, yb, z, norm_g, w1, w2, x2d, gate, rows_per_mod):
    t, d = x2d.shape
    tm = _tile(rows_per_mod, 512)
    tpm = rows_per_mod // tm
    nb = gate.shape[0]
    row = lambda i: (i, 0)
    const = lambda i: (0, 0)
    return pl.pallas_call(
        _outproj_kernel,
        grid=(t // tm,),
        in_specs=[pl.BlockSpec((tm, d), row), pl.BlockSpec((tm, d), row), pl.BlockSpec((tm, d), row),
                  pl.BlockSpec((tm, d), row), pl.BlockSpec((1, d), const),
                  pl.BlockSpec(w1.shape, const), pl.BlockSpec(w2.shape, const),
                  pl.BlockSpec((tm, d), row), pl.BlockSpec((None, 1, d), lambda i: (i // tpm, 0, 0))],
        out_specs=pl.BlockSpec((tm, d), row),
        out_shape=jax.ShapeDtypeStruct((t, d), F32),
        compiler_params=_params(("parallel",)),
        name="mixer_out_proj",
    )(o_att, yf, yb, z, norm_g.reshape(1, d), w1, w2, x2d, gate.reshape(nb, 1, d))


def _mm_res_kernel(a_ref, w_ref, x_ref, gate_ref, o_ref):
    acc = jnp.dot(a_ref[...], w_ref[...], preferred_element_type=F32)
    o_ref[...] = x_ref[...] + gate_ref[...] * acc


def matmul_residual(a, w, x2d, gate, rows_per_mod):
    t, d = x2d.shape
    tm = _tile(rows_per_mod, 512)
    tpm = rows_per_mod // tm
    nb = gate.shape[0]
    return pl.pallas_call(
        _mm_res_kernel,
        grid=(t // tm,),
        in_specs=[pl.BlockSpec((tm, a.shape[1]), lambda i: (i, 0)),
                  pl.BlockSpec(w.shape, lambda i: (0, 0)),
                  pl.BlockSpec((tm, d), lambda i: (i, 0)),
                  pl.BlockSpec((None, 1, d), lambda i: (i // tpm, 0, 0))],
        out_specs=pl.BlockSpec((tm, d), lambda i: (i, 0)),
        out_shape=jax.ShapeDtypeStruct((t, d), F32),
        compiler_params=_params(("parallel",)),
        name="matmul_gated_residual",
    )(a, w, x2d, gate.reshape(nb, 1, d))


def _dft_stage1_kernel(m_ref, z_ref, o_ref):
    z = z_ref[...]
    half = z.shape[1] // 2
    zs = jnp.concatenate([z[:, :half], z[:, half:]], axis=0)
    a = jnp.dot(m_ref[...], zs, preferred_element_type=F32)
    n1 = a.shape[0] // 2
    o_ref[0] = a[:n1].astype(o_ref.dtype)
    o_ref[1] = a[n1:].astype(o_ref.dtype)


def _dft_stage2_kernel(m_ref, a_ref, o_ref):
    o_ref[...] = jnp.dot(m_ref[...], a_ref[...], preferred_element_type=F32).astype(o_ref.dtype)


def _dft_tables(l, d):
    n1 = FOURIER_N1
    n2 = l // n1
    gw = d // FOURIER_GROUPS
    two_pi = 2.0 * math.pi
    ch = jnp.arange(gw, dtype=I32)
    ph = (ch[:, None] * ch[None, :]) % gw
    ang = ph.astype(F32) * (two_pi / gw)
    cc, sc = jnp.cos(ang) * gw ** -0.5, jnp.sin(ang) * gw ** -0.5
    eye = jnp.eye(FOURIER_GROUPS, dtype=F32)
    w_cs = jnp.concatenate([jnp.kron(eye, cc), -jnp.kron(eye, sc)], axis=1)
    k1 = jnp.arange(n1, dtype=I32)
    pos = jnp.arange(n2, dtype=I32)[:, None, None] + n2 * jnp.arange(n1, dtype=I32)[None, None, :]
    ang1 = ((k1[None, :, None] * pos) % l).astype(F32) * (two_pi / l)
    gc, gs = jnp.cos(ang1) * l ** -0.5, jnp.sin(ang1) * l ** -0.5
    m1 = jnp.concatenate([jnp.concatenate([gc, gs], axis=2),
                          jnp.concatenate([-gs, gc], axis=2)], axis=1)
    k2 = jnp.arange(n2, dtype=I32)
    ang2 = ((k2[:, None] * k2[None, :]) % n2).astype(F32) * (two_pi / n2)
    m2 = jnp.concatenate([jnp.cos(ang2), jnp.sin(ang2)], axis=1)
    return w_cs.astype(BF16), m1.astype(BF16), m2.astype(BF16)


def fourier_positions(z, m1, m2, batch):
    t, d2 = z.shape
    d = d2 // 2
    l = t // batch
    n1 = FOURIER_N1
    n2 = l // n1
    a = pl.pallas_call(
        _dft_stage1_kernel,
        grid=(batch, n2),
        in_specs=[pl.BlockSpec((None, 2 * n1, 2 * n1), lambda b, j: (j, 0, 0)),
                  pl.BlockSpec((None, n1, d2), lambda b, j: (b, 0, j))],
        out_specs=pl.BlockSpec((None, 2, None, n1, d), lambda b, j: (b, 0, j, 0, 0)),
        out_shape=jax.ShapeDtypeStruct((batch, 2, n2, n1, d), BF16),
        compiler_params=_params(("parallel", "parallel")),
        name="dft_stage1",
    )(m1, z.reshape(batch, n1, n2 * d2))
    cols = n1 * d
    tn = _tile(cols, 8192)
    f = pl.pallas_call(
        _dft_stage2_kernel,
        grid=(batch, cols // tn),
        in_specs=[pl.BlockSpec((n2, 2 * n2), lambda b, j: (0, 0)),
                  pl.BlockSpec((None, 2 * n2, tn), lambda b, j: (b, 0, j))],
        out_specs=pl.BlockSpec((None, n2, tn), lambda b, j: (b, 0, j)),
        out_shape=jax.ShapeDtypeStruct((batch, n2, cols), BF16),
        compiler_params=_params(("parallel", "parallel")),
        name="dft_stage2",
    )(m2, a.reshape(batch, 2 * n2, cols))
    return f.reshape(t, d)


SUB = 8


def _to_row_tiles(ref, val):
    rows = val.shape[0]
    for j in range(SUB):
        ref[pl.ds(j, rows, stride=SUB), :] = val[:, j * LANES:(j + 1) * LANES]


def _from_row_tiles(ref, rows):
    return jnp.concatenate([ref[pl.ds(j, rows, stride=SUB), :] for j in range(SUB)], axis=1)


def _row_tile(ref, row):
    return ref.at[pl.ds(pl.multiple_of(row * SUB, SUB), SUB), :]


def _route_kernel(x_ref, g_ref, sh_ref, sc_ref, wr_ref, br_ref, h_ref, ti_ref, tg_ref):
    h = _norm_mod(x_ref[...], g_ref[...], sh_ref[...], sc_ref[...])
    _to_row_tiles(h_ref, h)
    logits = jnp.dot(h, wr_ref[...], precision=HIGHEST, preferred_element_type=F32) + br_ref[...]
    lane = lax.broadcasted_iota(I32, logits.shape, 1)
    vals, idxs = [], []
    cur = logits
    for _ in range(TOP_K):
        m = jnp.max(cur, axis=-1, keepdims=True)
        idx = jnp.min(jnp.where(cur == m, lane, LANES), axis=-1, keepdims=True)
        vals.append(m)
        idxs.append(idx)
        cur = jnp.where(lane == idx, -jnp.inf, cur)
    es = [jnp.exp(v - vals[0]) for v in vals]
    tot = es[0] + es[1] + es[2] + es[3]
    ti = jnp.zeros(logits.shape, I32)
    tg = jnp.zeros(logits.shape, F32)
    for k in range(TOP_K):
        ti = jnp.where(lane == k, idxs[k], ti)
        tg = jnp.where(lane == k, es[k] / tot, tg)
    ti_ref[...] = ti
    tg_ref[...] = tg


def route(x2d, g, shift, scale, rows_per_mod, w_router, b_router):
    t, d = x2d.shape
    assert d == SUB * LANES
    nb = shift.shape[0]
    tm = _tile(rows_per_mod, 512)
    tpm = rows_per_mod // tm
    ne = w_router.shape[1]
    wr = jnp.pad(w_router, ((0, 0), (0, LANES - ne)))
    br = jnp.pad(b_router.reshape(1, ne), ((0, 0), (0, LANES - ne)), constant_values=NEG_BIG)
    return pl.pallas_call(
        _route_kernel,
        grid=(t // tm,),
        in_specs=[pl.BlockSpec((tm, d), lambda i: (i, 0)),
                  pl.BlockSpec((1, d), lambda i: (0, 0)),
                  pl.BlockSpec((None, 1, d), lambda i: (i // tpm, 0, 0)),
                  pl.BlockSpec((None, 1, d), lambda i: (i // tpm, 0, 0)),
                  pl.BlockSpec((d, LANES), lambda i: (0, 0)),
                  pl.BlockSpec((1, LANES), lambda i: (0, 0))],
        out_specs=[pl.BlockSpec((tm * SUB, LANES), lambda i: (i, 0)),
                   pl.BlockSpec((tm, LANES), lambda i: (i, 0)),
                   pl.BlockSpec((tm, LANES), lambda i: (i, 0))],
        out_shape=[jax.ShapeDtypeStruct((t * SUB, LANES), F32), jax.ShapeDtypeStruct((t, LANES), I32),
                   jax.ShapeDtypeStruct((t, LANES), F32)],
        compiler_params=_params(("parallel",)),
        name="moe_route",
    )(x2d, g.reshape(1, d), shift.reshape(nb, 1, d), scale.reshape(nb, 1, d), wr, br)


def _rank_kernel(ti_ref, rank_ref, cnt_ref, carry):
    @pl.when(pl.program_id(0) == 0)
    def _():
        carry[...] = jnp.zeros(carry.shape, carry.dtype)

    ti = ti_ref[...]
    tm = ti.shape[0]
    lane = lax.broadcasted_iota(I32, ti.shape, 1)
    earlier = (lax.broadcasted_iota(I32, (tm, tm), 1) < lax.broadcasted_iota(I32, (tm, tm), 0)).astype(BF16)
    base = carry[...]
    rank = jnp.zeros(ti.shape, I32)
    for k in range(TOP_K):
        hit = lane == ti[:, k:k + 1]
        onehot = jnp.where(hit, 1.0, 0.0)
        before = jnp.dot(earlier, onehot.astype(BF16), preferred_element_type=F32)
        rk = jnp.sum(jnp.where(hit, base + before, 0.0), axis=-1, keepdims=True)
        rank = jnp.where(lane == k, rk.astype(I32), rank)
        base = base + jnp.sum(onehot, axis=0, keepdims=True)
    rank_ref[...] = rank
    carry[...] = base
    cnt_ref[...] = base


def _dest_kernel(ti_ref, rank_ref, start_ref, dest_ref):
    ti = ti_ref[...]
    lane = lax.broadcasted_iota(I32, ti.shape, 1)
    dest = jnp.zeros(ti.shape, I32)
    for k in range(TOP_K):
        start = jnp.sum(jnp.where(lane == ti[:, k:k + 1], start_ref[...], 0), axis=-1, keepdims=True)
        dest = jnp.where(lane == k, start + rank_ref[:, k:k + 1], dest)
    dest_ref[...] = dest


def moe_plan(ti, t):
    tb = MOE_ROWS
    tm = _tile(t, 512)
    rank, cnt = pl.pallas_call(
        _rank_kernel,
        grid=(t // tm,),
        in_specs=[pl.BlockSpec((tm, LANES), lambda i: (i, 0))],
        out_specs=[pl.BlockSpec((tm, LANES), lambda i: (i, 0)), pl.BlockSpec((1, LANES), lambda i: (0, 0))],
        out_shape=[jax.ShapeDtypeStruct((t, LANES), I32), jax.ShapeDtypeStruct((1, LANES), F32)],
        scratch_shapes=[pltpu.VMEM((1, LANES), F32)],
        compiler_params=_params(("arbitrary",)),
        name="moe_rank",
    )(ti)
    counts = cnt[0, :N_EXPERTS].astype(I32)
    padded = (counts + tb - 1) // tb * tb
    pend = jnp.cumsum(padded)
    pstart = jnp.pad(pend - padded, (0, LANES - N_EXPERTS)).reshape(1, LANES)
    dest = pl.pallas_call(
        _dest_kernel,
        grid=(t // tm,),
        in_specs=[pl.BlockSpec((tm, LANES), lambda i: (i, 0)), pl.BlockSpec((tm, LANES), lambda i: (i, 0)),
                  pl.BlockSpec((1, LANES), lambda i: (0, 0))],
        out_specs=pl.BlockSpec((tm, LANES), lambda i: (i, 0)),
        out_shape=jax.ShapeDtypeStruct((t, LANES), I32),
        compiler_params=_params(("parallel",)),
        name="moe_dest",
    )(ti, rank, pstart)
    dest = dest[:, :TOP_K].reshape(-1)
    n_blocks = (t * TOP_K + N_EXPERTS * (tb - 1)) // tb + 1
    row_tok = jnp.zeros((n_blocks * tb,), I32).at[dest].set(jnp.arange(t * TOP_K, dtype=I32) // TOP_K)
    blk_e = jnp.minimum(jnp.searchsorted(pend, jnp.arange(n_blocks, dtype=I32) * tb, side='right'),
                        N_EXPERTS - 1).astype(I32)
    nact = (pend[-1] // tb).astype(I32).reshape(1)
    return dest, row_tok, blk_e, nact


def _expert_kernel(tb, be_ref, tok_ref, nact_ref, h_hbm, wgu_ref, bgu_ref, wdn_ref, bdn_ref, o_ref,
                   xbuf, wgu_s, wdn_wide, wdn_s, sem):
    i = pl.program_id(0)
    nact = nact_ref[0]
    de = wdn_ref.shape[0]

    @pl.when(i == 0)
    def _():
        wdn_wide[...] = jnp.zeros(wdn_wide.shape, wdn_wide.dtype)

    @pl.when((i < nact) & ((i == 0) | (be_ref[i] != be_ref[jnp.maximum(i - 1, 0)])))
    def _():
        wgu_s[...] = wgu_ref[...].astype(BF16)
        for c in range(wdn_wide.shape[0]):
            wdn_wide[c, pl.ds(0, de, stride=2), :] = wdn_ref[:, c * LANES:(c + 1) * LANES]
            wdn_s[:, c * LANES:(c + 1) * LANES] = wdn_wide[c].astype(BF16)

    def gather(blk, slot):
        def body(r, carry):
            tok = tok_ref[blk * tb + r]
            pltpu.make_async_copy(_row_tile(h_hbm, tok), _row_tile(xbuf.at[slot], r), sem.at[slot]).start()
            return carry
        lax.fori_loop(0, tb, body, 0, unroll=8)

    @pl.when(i == 0)
    def _():
        gather(0, 0)

    @pl.when(i + 1 < nact)
    def _():
        gather(i + 1, (i + 1) % 2)

    @pl.when(i < nact)
    def _():
        slot = i % 2
        pltpu.make_async_copy(h_hbm.at[pl.ds(0, tb * SUB), :], xbuf.at[slot], sem.at[slot]).wait()
        x = _from_row_tiles(xbuf.at[slot], tb).astype(BF16)
        gu = jnp.dot(x, wgu_s[...], preferred_element_type=F32) + bgu_ref[...]
        even = lax.broadcasted_iota(I32, (tb, LANES), 1) % 2 == 0
        parts = []
        for c0 in range(0, gu.shape[1], LANES):
            v = gu[:, c0:c0 + LANES]
            glu = jnp.minimum(v, SWIGLU_LIMIT)
            lin = jnp.clip(v, -SWIGLU_LIMIT, SWIGLU_LIMIT) + 1.0
            act = glu * jax.nn.sigmoid(SWIGLU_ALPHA * glu) * pltpu.roll(lin, LANES - 1, 1)
            parts.append(jnp.where(even, act, 0.0).astype(BF16))
        y = jnp.dot(jnp.concatenate(parts, axis=1), wdn_s[...], preferred_element_type=F32) + bdn_ref[...]
        _to_row_tiles(o_ref, y)

    @pl.when(i >= nact)
    def _():
        o_ref[...] = jnp.zeros(o_ref.shape, o_ref.dtype)


def expert_ffn(h_rt, row_tok, blk_e, nact, layer, w_gu, b_gu, w_dn, b_dn):
    r = row_tok.shape[0]
    tb = MOE_ROWS
    _, ne, d, de2 = w_gu.shape
    de = de2 // 2
    b_gu = b_gu.reshape(-1, ne, 1, de2)
    b_dn = b_dn.reshape(-1, ne, 1, d)
    return pl.pallas_call(
        functools.partial(_expert_kernel, tb),
        grid_spec=pltpu.PrefetchScalarGridSpec(
            num_scalar_prefetch=3, grid=(r // tb,),
            in_specs=[pl.BlockSpec(memory_space=pl.ANY),
                      pl.BlockSpec((None, None, d, de2), lambda i, be, tk, na: (layer, be[i], 0, 0)),
                      pl.BlockSpec((None, None, 1, de2), lambda i, be, tk, na: (layer, be[i], 0, 0)),
                      pl.BlockSpec((None, None, de, d), lambda i, be, tk, na: (layer, be[i], 0, 0)),
                      pl.BlockSpec((None, None, 1, d), lambda i, be, tk, na: (layer, be[i], 0, 0))],
            out_specs=pl.BlockSpec((tb * SUB, LANES), lambda i, be, tk, na: (i, 0)),
            scratch_shapes=[pltpu.VMEM((2, tb * SUB, LANES), F32), pltpu.VMEM((d, de2), BF16),
                            pltpu.VMEM((d // LANES, de2, LANES), F32), pltpu.VMEM((de2, d), BF16),
                            pltpu.SemaphoreType.DMA((2,))]),
        out_shape=jax.ShapeDtypeStruct((r * SUB, LANES), F32),
        compiler_params=_params(("arbitrary",), 58),
        name="moe_experts",
    )(blk_e, row_tok, nact, h_rt, w_gu, b_gu, w_dn, b_dn)


def _combine_kernel(tm, dest_ref, y_hbm, x_ref, gate_ref, tg_ref, o_ref, buf, sem):
    base = pl.program_id(0) * tm

    def body(r, carry):
        for k in range(TOP_K):
            row = dest_ref[(base + r) * TOP_K + k]
            pltpu.make_async_copy(_row_tile(y_hbm, row), _row_tile(buf.at[k], r), sem).start()
        return carry

    lax.fori_loop(0, tm, body, 0, unroll=4)
    for k in range(TOP_K):
        pltpu.make_async_copy(y_hbm.at[pl.ds(0, tm * SUB), :], buf.at[k], sem).wait()
    tg = tg_ref[...]
    ys = [_from_row_tiles(buf.at[k], tm) * tg[:, k:k + 1] for k in range(TOP_K)]
    o_ref[...] = x_ref[...] + gate_ref[...] * ((ys[0] + ys[1]) + (ys[2] + ys[3]))


def moe_combine(y_rt, dest, tg, x2d, gate, rows_per_mod):
    t, d = x2d.shape
    nb = gate.shape[0]
    tm = _tile(rows_per_mod, 256)
    tpm = rows_per_mod // tm
    return pl.pallas_call(
        functools.partial(_combine_kernel, tm),
        grid_spec=pltpu.PrefetchScalarGridSpec(
            num_scalar_prefetch=1, grid=(t // tm,),
            in_specs=[pl.BlockSpec(memory_space=pl.ANY),
                      pl.BlockSpec((tm, d), lambda i, ds: (i, 0)),
                      pl.BlockSpec((None, 1, d), lambda i, ds: (i // tpm, 0, 0)),
                      pl.BlockSpec((tm, LANES), lambda i, ds: (i, 0))],
            out_specs=pl.BlockSpec((tm, d), lambda i, ds: (i, 0)),
            scratch_shapes=[pltpu.VMEM((TOP_K, tm * SUB, LANES), F32), pltpu.SemaphoreType.DMA]),
        out_shape=jax.ShapeDtypeStruct((t, d), F32),
        compiler_params=_params(("arbitrary",)),
        name="moe_combine",
    )(dest, y_rt, x2d, gate.reshape(nb, 1, d), tg)


def moe_layer(x2d, norm_g, shift, scale, gate, rows_per_mod, w_router, b_router, layer, w_gu, b_gu, w_dn, b_dn):
    t = x2d.shape[0]
    h_rt, ti, tg = route(x2d, norm_g, shift, scale, rows_per_mod, w_router, b_router)
    dest, row_tok, blk_e, nact = moe_plan(ti, t)
    y_rt = expert_ffn(h_rt, row_tok, blk_e, nact, layer, w_gu, b_gu, w_dn, b_dn)
    return moe_combine(y_rt, dest, tg, x2d, gate, rows_per_mod)


def _rope_tables(s):
    n_freq = DA_HEAD_DIM // 4
    freqs = ROPE_BASE ** (-jnp.arange(n_freq, dtype=F32) / n_freq)
    pos = jnp.arange(s, dtype=I32)
    ang_r = (pos // GRID_W).astype(F32)[:, None] * freqs
    ang_c = (pos % GRID_W).astype(F32)[:, None] * freqs
    ang = jnp.concatenate([ang_r, ang_r, ang_c, ang_c], axis=-1)
    ang = jnp.concatenate([ang, ang], axis=-1)
    return jnp.cos(ang), jnp.sin(ang)


def _even_layer(x2d, ctx2d, mods, norm_g1, w_in, w_out, q_norm_g, k_norm_g, da_lambda, da_subln_g,
                conv_xb_w, conv_xb_b, conv_c_w, conv_c_b, dt_bias, a_log, d_skip, ssm_norm_g,
                lam_init, batch):
    sx1, cx1, gx1, sc1, cc1 = mods
    t, d = x2d.shape
    s = t // batch
    ctx_len = ctx2d.shape[0] // batch
    qkw = DA_HEADS * 2 * DA_HEAD_DIM
    vw = DA_HEADS * DA_V_DIM
    col_q, col_z = 0, qkw
    col_c = col_z + SSM_D_INNER
    col_k = col_c + SSM_BC_W
    col_v = col_k + qkw
    col_xb = col_v + vw
    col_dt = col_xb + SSM_XB_W
    wb = w_in.astype(BF16)
    w_q, w_k, w_v = wb[:, col_q:col_q + qkw], wb[:, col_k:col_k + qkw], wb[:, col_v:col_v + vw]
    w_z = wb[:, col_z:col_z + SSM_D_INNER]
    w_xbc = jnp.concatenate([wb[:, col_xb:col_xb + SSM_XB_W], wb[:, col_c:col_c + SSM_BC_W]], axis=1)
    w_dt = jnp.pad(wb[:, col_dt:col_dt + 2 * SSM_HEADS], ((0, 0), (0, LANES - 2 * SSM_HEADS)))
    weights = [w_q, w_k, w_v, w_z, w_xbc, w_dt]
    dts = [BF16, BF16, BF16, BF16, BF16, F32]
    q_u, k_u, v_x, z_x, xbc_u, dt_x = nm_panel(x2d, norm_g1, sx1, cx1, s, weights, dts)
    _, kc_u, v_c, _, xbc_cu, dt_c = nm_panel(ctx2d, norm_g1, sc1, cc1, batch * ctx_len, weights, dts)

    lam = (jnp.exp(jnp.sum(da_lambda[0] * da_lambda[1])) -
           jnp.exp(jnp.sum(da_lambda[2] * da_lambda[3]))).astype(F32) + lam_init
    a_neg = -jnp.exp(a_log.astype(F32)).reshape(-1)

    cos_t, sin_t = _rope_tables(s)
    q = qk_prep(q_u, q_norm_g, cos_t, sin_t, s, True, DA_HEAD_DIM ** -0.5 * LOG2E)
    score_bound = (1.01 * DA_HEAD_DIM ** 0.5 * LOG2E) * jnp.max(jnp.abs(q_norm_g)) * jnp.max(jnp.abs(k_norm_g))
    k_x = qk_prep(k_u, k_norm_g, cos_t, sin_t, s, True, 1.0)
    k_c = qk_prep(kc_u, k_norm_g, cos_t, sin_t, ctx_len, False, 1.0)
    k_all = jnp.concatenate([k_x.reshape(batch, s, qkw), k_c.reshape(batch, ctx_len, qkw)], axis=1)
    v_all = jnp.concatenate([v_x.reshape(batch, s, vw), v_c.reshape(batch, ctx_len, vw)], axis=1)
    o_att = diff_attention(q, k_all, v_all, lam, score_bound, da_subln_g, lam_init, batch)

    conv_w = jnp.concatenate([conv_xb_w, conv_c_w], axis=1)
    conv_b = jnp.concatenate([conv_xb_b, conv_c_b], axis=0)
    xbc_x = conv_silu(xbc_u, conv_w, conv_b, s)
    xbc_c = conv_silu(xbc_cu, conv_w, conv_b, ctx_len)
    hshape = (batch, SSM_GROUPS, SSM_STATE, SSM_HPG * SSM_HEAD_DIM)
    zero_h = jnp.zeros(hshape, F32)
    dt_bias_f = dt_bias.reshape(-1).astype(F32)
    _, _, h_f, h_b = ssd_bidir(xbc_c, dt_c, dt_bias_f, a_neg, d_skip, zero_h, zero_h, batch)
    y_f, y_b, _, _ = ssd_bidir(xbc_x, dt_x, dt_bias_f, a_neg, d_skip, h_f, h_b, batch)
    wo = w_out.astype(BF16)
    return out_proj(o_att, y_f, y_b, z_x, ssm_norm_g, wo[:vw], wo[vw:], x2d, gx1, s)


def _odd_layer(x2d, mods, norm_g1, w_fourier, batch):
    sx1, cx1, gx1 = mods
    t, d = x2d.shape
    s = t // batch
    w_cs, m1, m2 = _dft_tables(s, d)
    (z,) = nm_panel(x2d, norm_g1, sx1, cx1, s, [w_cs], [BF16])
    f = fourier_positions(z, m1, m2, batch)
    return matmul_residual(f, w_fourier.astype(BF16), x2d, gx1, s)


def kernel(x, c, ctx, c_ctx, ada_w, ada_b, norm_g, w_in, w_out, q_norm_g, k_norm_g, da_lambda, da_subln_g, conv_xb_w, conv_xb_b, conv_c_w, conv_c_b, dt_bias, a_log, d_skip, ssm_norm_g, w_fourier, w_router, b_router, w_gate_up, b_gate_up, w_down, b_down):
    b, s, d = x.shape
    depth = ada_w.shape[0]
    assert b + 1 <= 8
    cvecs = jnp.concatenate([c, c_ctx[None, :], jnp.zeros((8 - b - 1, d), F32)], axis=0)
    mods = ada_all(cvecs, ada_w, ada_b)
    x2d = x.reshape(b * s, d)
    ctx2d = ctx.reshape(-1, d)
    for i in range(depth):
        m = mods[i].reshape(8, 6, d)
        sx1, cx1, gx1, sx2, cx2, gx2 = [m[:b, j] for j in range(6)]
        if i % 2 == 0:
            e = i // 2
            lam_init = 0.8 - 0.6 * math.exp(-0.3 * i)
            if any(j % 2 == 0 for j in range(i + 1, depth)):
                raise NotImplementedError("context stream output is only needed for depth > 2")
            sc1, cc1 = m[b:b + 1, 0], m[b:b + 1, 1]
            x2d = _even_layer(x2d, ctx2d, (sx1, cx1, gx1, sc1, cc1), norm_g[i, 0], w_in[e], w_out[e],
                              q_norm_g[e], k_norm_g[e], da_lambda[e], da_subln_g[e], conv_xb_w[e],
                              conv_xb_b[e], conv_c_w[e], conv_c_b[e], dt_bias[e], a_log[e], d_skip[e],
                              ssm_norm_g[e], lam_init, b)
        else:
            x2d = _odd_layer(x2d, (sx1, cx1, gx1), norm_g[i, 0], w_fourier[i // 2], b)
        x2d = moe_layer(x2d, norm_g[i, 1], sx2, cx2, gx2, s, w_router[i], b_router[i],
                        i, w_gate_up, b_gate_up, w_down, b_down)
    return x2d.reshape(b, s, d)
```

```python
import functools
import math

import jax
import jax.numpy as jnp
from jax import lax
from jax.experimental import pallas as pl
from jax.experimental.pallas import tpu as pltpu

F32 = jnp.float32
BF16 = jnp.bfloat16
I32 = jnp.int32

RMS_EPS = 1e-6
GRID_W = 64
ROPE_BASE = 10000.0
DA_HEADS = 8
DA_HEAD_DIM = 64
DA_V_DIM = 128
SSM_HEADS = 16
SSM_HEAD_DIM = 64
SSM_GROUPS = 2
SSM_HPG = 8
SSM_STATE = 128
SSM_CONV = 5
SSM_CHUNK = 128
SSM_D_INNER = 1024
SSM_BC_W = 256
SSM_XB_W = 1280
FOURIER_GROUPS = 4
FOURIER_N1 = 128
N_EXPERTS = 32
TOP_K = 4
SWIGLU_LIMIT = 7.0
SWIGLU_ALPHA = 1.702
MOE_ROWS = 256
LANES = 128
NEG_BIG = -1e30
MIB = 1024 * 1024
HIGHEST = lax.Precision.HIGHEST


def _params(sem, vmem_mib=48):
    return pltpu.CompilerParams(dimension_semantics=sem, vmem_limit_bytes=vmem_mib * MIB)


def _tile(n, pref):
    t = min(n, pref)
    while n % t:
        t //= 2
    return t


def _silu(v):
    return v * jax.nn.sigmoid(v)


def _norm_mod(xf, g, shift, scale):
    r = lax.rsqrt(jnp.mean(xf * xf, axis=-1, keepdims=True) + RMS_EPS)
    return ((xf * r) * g) * (1.0 + scale) + shift


def _ada_kernel(a_ref, w_ref, b_ref, o_ref):
    s = _silu(a_ref[...])
    o_ref[...] = jnp.dot(s.astype(BF16), w_ref[...].astype(BF16),
                         preferred_element_type=F32) + b_ref[...]


def ada_all(cvecs, ada_w, ada_b):
    depth, d, n = ada_w.shape
    tn = _tile(n, 1536)
    return pl.pallas_call(
        _ada_kernel,
        grid=(depth, n // tn),
        in_specs=[pl.BlockSpec((8, d), lambda l, j: (0, 0)),
                  pl.BlockSpec((None, d, tn), lambda l, j: (l, 0, j)),
                  pl.BlockSpec((None, 1, tn), lambda l, j: (l, 0, j))],
        out_specs=pl.BlockSpec((None, 8, tn), lambda l, j: (l, 0, j)),
        out_shape=jax.ShapeDtypeStruct((depth, 8, n), F32),
        compiler_params=_params(("parallel", "parallel")),
        name="ada_mod",
    )(cvecs, ada_w, ada_b.reshape(depth, 1, n))


def _nm_panel_kernel(n_w, x_ref, g_ref, sh_ref, sc_ref, *refs):
    w_refs, o_refs = refs[:n_w], refs[n_w:2 * n_w]
    hb = _norm_mod(x_ref[...], g_ref[...], sh_ref[...], sc_ref[...]).astype(BF16)
    for w_ref, o_ref in zip(w_refs, o_refs):
        n = w_ref.shape[1]
        cw = _tile(n, 512)
        for c0 in range(0, n, cw):
            o_ref[:, c0:c0 + cw] = jnp.dot(hb, w_ref[:, c0:c0 + cw],
                                           preferred_element_type=F32).astype(o_ref.dtype)


def nm_panel(x2d, g, shift, scale, rows_per_mod, weights, out_dtypes, tm_pref=256):
    t, d = x2d.shape
    nb = shift.shape[0]
    tm = _tile(rows_per_mod, tm_pref)
    tpm = rows_per_mod // tm
    n_w = len(weights)
    in_specs = [pl.BlockSpec((tm, d), lambda i: (i, 0)),
                pl.BlockSpec((1, d), lambda i: (0, 0)),
                pl.BlockSpec((None, 1, d), lambda i: (i // tpm, 0, 0)),
                pl.BlockSpec((None, 1, d), lambda i: (i // tpm, 0, 0))]
    in_specs += [pl.BlockSpec(w.shape, lambda i: (0, 0)) for w in weights]
    out_specs = [pl.BlockSpec((tm, w.shape[1]), lambda i: (i, 0)) for w in weights]
    out_shape = [jax.ShapeDtypeStruct((t, w.shape[1]), dt) for w, dt in zip(weights, out_dtypes)]
    return pl.pallas_call(
        functools.partial(_nm_panel_kernel, n_w),
        grid=(t // tm,),
        in_specs=in_specs, out_specs=out_specs, out_shape=out_shape,
        compiler_params=_params(("parallel",), 56),
        name="norm_mod_proj",
    )(x2d, g.reshape(1, d), shift.reshape(nb, 1, d), scale.reshape(nb, 1, d), *weights)


def _qk_prep_kernel(rope, out_scale, u_ref, g_ref, seg_ref, cos_ref, sin_ref, o_ref):
    seg = seg_ref[...]
    g = g_ref[...]
    n_heads = u_ref.shape[1] // LANES
    for c in range(n_heads):
        u = u_ref[:, c * LANES:(c + 1) * LANES].astype(F32)
        u2 = u * u
        hi = u2.astype(BF16)
        lo = (u2 - hi.astype(F32)).astype(BF16)
        ss = (jnp.dot(hi, seg, preferred_element_type=F32)
              + jnp.dot(lo, seg, preferred_element_type=F32))
        nrm = (u * lax.rsqrt(ss * (1.0 / DA_HEAD_DIM) + RMS_EPS)) * g
        if rope:
            lane = lax.broadcasted_iota(I32, nrm.shape, 1)
            first = (lane % 32) < 16
            rot = jnp.where(first, -pltpu.roll(nrm, LANES - 16, 1), pltpu.roll(nrm, 16, 1))
            nrm = nrm * cos_ref[...] + rot * sin_ref[...]
        o_ref[:, c * LANES:(c + 1) * LANES] = (nrm * out_scale).astype(o_ref.dtype)


def qk_prep(u, gain, cos_t, sin_t, seq, rope, out_scale):
    t, w = u.shape
    tm = _tile(seq, 256)
    tps = seq // tm
    seg = (jnp.arange(LANES)[:, None] // DA_HEAD_DIM == jnp.arange(LANES)[None, :] // DA_HEAD_DIM)
    return pl.pallas_call(
        functools.partial(_qk_prep_kernel, rope, out_scale),
        grid=(t // tm,),
        in_specs=[pl.BlockSpec((tm, w), lambda i: (i, 0)),
                  pl.BlockSpec((1, LANES), lambda i: (0, 0)),
                  pl.BlockSpec((LANES, LANES), lambda i: (0, 0)),
                  pl.BlockSpec((tm, LANES), lambda i: (i % tps, 0)),
                  pl.BlockSpec((tm, LANES), lambda i: (i % tps, 0))],
        out_specs=pl.BlockSpec((tm, w), lambda i: (i, 0)),
        out_shape=jax.ShapeDtypeStruct((t, w), BF16),
        compiler_params=_params(("parallel",)),
        name="qk_norm_rope",
    )(u, jnp.tile(gain, 2).reshape(1, LANES), seg.astype(BF16), cos_t, sin_t)


LOG2E = math.log2(math.e)
ATTN_SHIFT_LIMIT = 60.0


def _split_components(q):
    lane = lax.broadcasted_iota(I32, q.shape, 1)
    zero = jnp.zeros_like(q)
    return jnp.where(lane < DA_HEAD_DIM, q, zero), jnp.where(lane >= DA_HEAD_DIM, q, zero)


def _attn_finish(n0, l0, n1, l1, lam, g, out_mult, o_ref):
    o = n0 / l0 - lam * (n1 / l1)
    o = (o * lax.rsqrt(jnp.mean(o * o, axis=-1, keepdims=True) + RMS_EPS)) * g
    o_ref[...] = (o * out_mult).astype(o_ref.dtype)


def _attn_online_kernel(tk, out_mult, sc_ref, q_ref, k_ref, v_ref, g_ref, o_ref):
    qs = _split_components(q_ref[...])
    tq = q_ref.shape[0]
    n_kv = k_ref.shape[0] // tk

    def body(i, carry):
        off = pl.multiple_of(i * tk, tk)
        k = k_ref[pl.ds(off, tk), :]
        v = v_ref[pl.ds(off, tk), :]
        new = []
        for c in range(2):
            m, l, acc = carry[c]
            s = lax.dot_general(qs[c], k, (((1,), (1,)), ((), ())), preferred_element_type=F32)
            m_new = jnp.maximum(m, jnp.max(s, axis=-1, keepdims=True))
            alpha = jnp.exp2(m - m_new)
            p = jnp.exp2(s - m_new)
            l_new = alpha * l + jnp.sum(p, axis=-1, keepdims=True)
            acc_new = alpha * acc + jnp.dot(p.astype(BF16), v, preferred_element_type=F32)
            new.append((m_new, l_new, acc_new))
        return tuple(new)

    init = tuple((jnp.full((tq, 1), -jnp.inf, F32), jnp.zeros((tq, 1), F32),
                  jnp.zeros((tq, DA_V_DIM), F32)) for _ in range(2))
    (_, l0, a0), (_, l1, a1) = lax.fori_loop(0, n_kv, body, init)
    _attn_finish(a0, l0, a1, l1, sc_ref[0], g_ref[...], out_mult, o_ref)


def _attn_shift_kernel(tk, out_mult, sc_ref, q_ref, k_ref, v_ref, g_ref, o_ref):
    qs = _split_components(q_ref[...])
    tq = q_ref.shape[0]
    n_kv = k_ref.shape[0] // tk
    shift = sc_ref[1]
    ones = jnp.ones((tk, LANES), BF16)

    sub = 2 * LANES

    def body(i, acc):
        acc = list(acc)
        for j in range(tk // sub):
            off = pl.multiple_of(i * tk + j * sub, sub)
            k = k_ref[pl.ds(off, sub), :]
            va = jnp.concatenate([v_ref[pl.ds(off, sub), :], ones[:sub]], axis=1)
            for c in range(2):
                s = lax.dot_general(qs[c], k, (((1,), (1,)), ((), ())), preferred_element_type=F32)
                acc[c] = acc[c] + jnp.dot(jnp.exp2(s - shift).astype(BF16), va, preferred_element_type=F32)
        return tuple(acc)

    zero = jnp.zeros((tq, 2 * LANES), F32)
    a0, a1 = lax.fori_loop(0, n_kv, body, (zero, zero))
    _attn_finish(a0[:, :LANES], a0[:, LANES:], a1[:, :LANES], a1[:, LANES:], sc_ref[0], g_ref[...], out_mult,
                 o_ref)


def diff_attention(q, k_all, v_all, lam, score_bound, subln_g, lam_init, batch):
    t, w = q.shape
    s = t // batch
    sk = k_all.shape[1]
    tq = _tile(s, 512)
    tk = 2816 if sk % 2816 == 0 else _tile(sk, 256)
    nq = s // tq
    scalars = jnp.stack([lam, score_bound]).astype(F32)

    def run(body):
        return pl.pallas_call(
            functools.partial(body, tk, 1.0 - lam_init),
            grid=(batch, DA_HEADS, nq),
            in_specs=[pl.BlockSpec(memory_space=pltpu.SMEM),
                      pl.BlockSpec((tq, LANES), lambda b, h, i: (b * nq + i, h)),
                      pl.BlockSpec((None, sk, LANES), lambda b, h, i: (b, 0, h)),
                      pl.BlockSpec((None, sk, LANES), lambda b, h, i: (b, 0, h)),
                      pl.BlockSpec((1, LANES), lambda b, h, i: (0, 0))],
            out_specs=pl.BlockSpec((tq, LANES), lambda b, h, i: (b * nq + i, h)),
            out_shape=jax.ShapeDtypeStruct((t, w), BF16),
            compiler_params=_params(("parallel", "parallel", "parallel")),
            name="diff_attention",
        )(scalars, q, k_all, v_all, subln_g.reshape(1, LANES))

    return lax.cond(score_bound <= ATTN_SHIFT_LIMIT,
                    lambda: run(_attn_shift_kernel), lambda: run(_attn_online_kernel))


HALO = 16


def _conv_kernel(tps, p_ref, c_ref, n_ref, w_ref, b_ref, o_ref, ext_ref):
    t = pl.program_id(0) % tps
    tm = c_ref.shape[0]
    prev = p_ref[...].astype(F32)
    nxt = n_ref[...].astype(F32)
    ext_ref[0:HALO, :] = jnp.where(t == 0, 0.0, prev)
    ext_ref[HALO:HALO + tm, :] = c_ref[...].astype(F32)
    ext_ref[HALO + tm:2 * HALO + tm, :] = jnp.where(t == tps - 1, 0.0, nxt)
    width = c_ref.shape[1]
    cw = _tile(width, 256)
    pad = (SSM_CONV - 1) // 2
    for c0 in range(0, width, cw):
        acc = jnp.broadcast_to(b_ref[:, c0:c0 + cw], (tm, cw))
        for k in range(SSM_CONV):
            acc = acc + w_ref[k:k + 1, c0:c0 + cw] * ext_ref[HALO - pad + k:HALO - pad + k + tm, c0:c0 + cw]
        o_ref[:, c0:c0 + cw] = _silu(acc).astype(o_ref.dtype)


def conv_silu(u, w, b, seq):
    t, c = u.shape
    tm = _tile(seq, 512)
    tps = seq // tm
    hb = tm // HALO
    last = t // HALO - 1
    return pl.pallas_call(
        functools.partial(_conv_kernel, tps),
        grid=(t // tm,),
        in_specs=[pl.BlockSpec((HALO, c), lambda i: (jnp.maximum(i * hb - 1, 0), 0)),
                  pl.BlockSpec((tm, c), lambda i: (i, 0)),
                  pl.BlockSpec((HALO, c), lambda i: (jnp.minimum((i + 1) * hb, last), 0)),
                  pl.BlockSpec((SSM_CONV, c), lambda i: (0, 0)),
                  pl.BlockSpec((1, c), lambda i: (0, 0))],
        out_specs=pl.BlockSpec((tm, c), lambda i: (i, 0)),
        out_shape=jax.ShapeDtypeStruct((t, c), BF16),
        scratch_shapes=[pltpu.VMEM((tm + 2 * HALO, c), F32)],
        compiler_params=_params(("parallel",)),
        name="dwconv_silu",
    )(u, u, u, w, b.reshape(1, c))


def _softplus(v):
    return jnp.maximum(v, 0.0) + jnp.log1p(jnp.exp(-jnp.abs(v)))


def _pair(lane_lo, a, b):
    return jnp.where(lane_lo, a, b)


def _ssd_kernel(xf_ref, dtf_ref, dtTf_ref, xb_ref, dtb_ref, dtTb_ref, bias_r, a_r, bias_c, a_c,
                dsk_ref, h0f_ref, h0b_ref, yf_ref, yb_ref, hfo_ref, hbo_ref, hf_s, hb_s):
    c = pl.program_id(1)
    q = SSM_CHUNK
    nh = SSM_HEADS

    @pl.when(c == 0)
    def _():
        hf_s[...] = h0f_ref[...]
        hb_s[...] = h0b_ref[...]

    li = lax.broadcasted_iota(I32, (q, q), 0)
    si = lax.broadcasted_iota(I32, (q, q), 1)
    lower = li >= si
    upper = li <= si
    tri_l = lower.astype(F32)
    tri_u = upper.astype(F32)
    lane_lo = lax.broadcasted_iota(I32, (q, LANES), 1) < SSM_HEAD_DIM
    lane_lo1 = lax.broadcasted_iota(I32, (1, LANES), 1) < SSM_HEAD_DIM

    def bc(col):
        return jnp.broadcast_to(col, (q, LANES))

    def state_update(h_s, g, xs32, bg, wq, edge, base):
        parts, decs = [], []
        for pr in range(SSM_HPG // 2):
            h0 = g * SSM_HPG + 2 * pr
            wp = _pair(lane_lo, bc(wq[:, base + h0:base + h0 + 1]), bc(wq[:, base + h0 + 1:base + h0 + 2]))
            parts.append((xs32[:, h0 * 64:h0 * 64 + LANES] * wp).astype(BF16))
            decs.append(_pair(lane_lo1, jnp.broadcast_to(edge[:, base + h0:base + h0 + 1], (1, LANES)),
                              jnp.broadcast_to(edge[:, base + h0 + 1:base + h0 + 2], (1, LANES))))
        xw = jnp.concatenate(parts, axis=1)
        dec = jnp.exp(jnp.concatenate(decs, axis=1))
        upd = lax.dot_general(bg, xw, (((0,), (0,)), ((), ())), preferred_element_type=F32)
        h_s[g] = h_s[g] * dec + upd

    xbc = xf_ref[...]
    xs_b = xbc[:, :SSM_D_INNER]
    xs32 = xs_b.astype(F32)
    dt = _softplus(dtf_ref[...] + bias_r[...])
    da = dt * a_r[...]
    acs = jnp.dot(tri_l, da, precision=HIGHEST, preferred_element_type=F32)
    racs = jnp.dot(tri_u, da, precision=HIGHEST, preferred_element_type=F32)
    dt_t = _softplus(dtTf_ref[...] + bias_c[...])
    da_t = dt_t * a_c[...]
    acs_t = jnp.dot(da_t, tri_u, precision=HIGHEST, preferred_element_type=F32)
    racs_t = jnp.dot(da_t, tri_l, precision=HIGHEST, preferred_element_type=F32)
    last = acs[q - 1:q, :]
    wq = jnp.exp(last - acs) * dt
    for g in range(SSM_GROUPS):
        bg = xbc[:, SSM_D_INNER + g * SSM_STATE:SSM_D_INNER + (g + 1) * SSM_STATE]
        cg = xbc[:, SSM_D_INNER + SSM_BC_W + g * SSM_STATE:SSM_D_INNER + SSM_BC_W + (g + 1) * SSM_STATE]
        cb = lax.dot_general(cg, bg, (((1,), (1,)), ((), ())), preferred_element_type=F32)
        yoff = jnp.dot(cg, hf_s[g].astype(BF16), preferred_element_type=F32)
        for pr in range(SSM_HPG // 2):
            h0 = g * SSM_HPG + 2 * pr
            col0 = h0 * SSM_HEAD_DIM
            xp = xs_b[:, col0:col0 + LANES]
            ys = []
            for hh in (h0, h0 + 1):
                mf = jnp.exp(jnp.where(lower, acs[:, hh:hh + 1] - acs_t[hh:hh + 1, :], NEG_BIG)) \
                    * dt_t[hh:hh + 1, :]
                mb = jnp.exp(jnp.where(upper, racs[:, nh + hh:nh + hh + 1] - racs_t[nh + hh:nh + hh + 1, :],
                                       NEG_BIG)) * dt_t[nh + hh:nh + hh + 1, :]
                ys.append(jnp.dot((cb * (mf + mb)).astype(BF16), xp, preferred_element_type=F32))
            ef = jnp.exp(_pair(lane_lo, bc(acs[:, h0:h0 + 1]), bc(acs[:, h0 + 1:h0 + 2])))
            yf_ref[:, col0:col0 + LANES] = (_pair(lane_lo, ys[0], ys[1])
                                            + yoff[:, 2 * pr * 64:2 * pr * 64 + LANES] * ef
                                            + xs32[:, col0:col0 + LANES] * dsk_ref[:, col0:col0 + LANES])
        state_update(hf_s, g, xs32, bg, wq, last, 0)

    xbc2 = xb_ref[...]
    xs2 = xbc2[:, :SSM_D_INNER].astype(F32)
    dt2 = _softplus(dtb_ref[...] + bias_r[...])
    racs2 = jnp.dot(tri_u, dt2 * a_r[...], precision=HIGHEST, preferred_element_type=F32)
    first = racs2[0:1, :]
    wq2 = jnp.exp(first - racs2) * dt2
    for g in range(SSM_GROUPS):
        bg = xbc2[:, SSM_D_INNER + g * SSM_STATE:SSM_D_INNER + (g + 1) * SSM_STATE]
        cg = xbc2[:, SSM_D_INNER + SSM_BC_W + g * SSM_STATE:SSM_D_INNER + SSM_BC_W + (g + 1) * SSM_STATE]
        yoff = jnp.dot(cg, hb_s[g].astype(BF16), preferred_element_type=F32)
        for pr in range(SSM_HPG // 2):
            h0 = g * SSM_HPG + 2 * pr
            col0 = h0 * SSM_HEAD_DIM
            eb = jnp.exp(_pair(lane_lo, bc(racs2[:, nh + h0:nh + h0 + 1]), bc(racs2[:, nh + h0 + 1:nh + h0 + 2])))
            yb_ref[:, col0:col0 + LANES] = yoff[:, 2 * pr * 64:2 * pr * 64 + LANES] * eb
        state_update(hb_s, g, xs2, bg, wq2, first, nh)

    @pl.when(c == pl.num_programs(1) - 1)
    def _():
        hfo_ref[...] = hf_s[...]
        hbo_ref[...] = hb_s[...]


def ssd_bidir(xbc, dt_raw, dt_bias, a_neg, d_skip, h0f, h0b, batch):
    t = xbc.shape[0]
    l = t // batch
    q = SSM_CHUNK
    nc = l // q
    w = xbc.shape[1]
    dt_t = jnp.transpose(dt_raw[:, :2 * SSM_HEADS].reshape(batch, l, 2 * SSM_HEADS), (0, 2, 1))
    pad = LANES - 2 * SSM_HEADS
    bias_r = jnp.pad(dt_bias.reshape(1, -1), ((0, 0), (0, pad)))
    a_r = jnp.pad(a_neg.reshape(1, -1), ((0, 0), (0, pad)))
    dsk = jnp.repeat(d_skip, SSM_HEAD_DIM).reshape(1, SSM_D_INNER)
    hshape = (batch, SSM_GROUPS, SSM_STATE, SSM_HPG * SSM_HEAD_DIM)
    fwd = lambda b, c: (b * nc + c, 0)
    bwd = lambda b, c: (b * nc + nc - 1 - c, 0)
    const2 = lambda b, c: (0, 0)
    hmap = lambda b, c: (b, 0, 0, 0)
    hspec = pl.BlockSpec((None,) + hshape[1:], hmap)
    return pl.pallas_call(
        _ssd_kernel,
        grid=(batch, nc),
        in_specs=[pl.BlockSpec((q, w), fwd), pl.BlockSpec((q, LANES), fwd),
                  pl.BlockSpec((None, 2 * SSM_HEADS, q), lambda b, c: (b, 0, c)),
                  pl.BlockSpec((q, w), bwd), pl.BlockSpec((q, LANES), bwd),
                  pl.BlockSpec((None, 2 * SSM_HEADS, q), lambda b, c: (b, 0, nc - 1 - c)),
                  pl.BlockSpec((1, LANES), const2), pl.BlockSpec((1, LANES), const2),
                  pl.BlockSpec((2 * SSM_HEADS, 1), const2), pl.BlockSpec((2 * SSM_HEADS, 1), const2),
                  pl.BlockSpec((1, SSM_D_INNER), const2), hspec, hspec],
        out_specs=[pl.BlockSpec((q, SSM_D_INNER), fwd), pl.BlockSpec((q, SSM_D_INNER), bwd), hspec, hspec],
        out_shape=[jax.ShapeDtypeStruct((t, SSM_D_INNER), F32), jax.ShapeDtypeStruct((t, SSM_D_INNER), F32),
                   jax.ShapeDtypeStruct(hshape, F32), jax.ShapeDtypeStruct(hshape, F32)],
        scratch_shapes=[pltpu.VMEM(hshape[1:], F32), pltpu.VMEM(hshape[1:], F32)],
        compiler_params=_params(("parallel", "arbitrary")),
        name="ssd_bidir",
    )(xbc, dt_raw, dt_t, xbc, dt_raw, dt_t, bias_r, a_r, dt_bias.reshape(-1, 1), a_neg.reshape(-1, 1),
      dsk, h0f, h0b)


def _outproj_kernel(oa_ref, yf_ref, yb_ref, z_ref, ng_ref, w1_ref, w2_ref, x_ref, gate_ref, o_ref):
    y = yf_ref[...] + yb_ref[...]
    u = y * _silu(z_ref[...].astype(F32))
    un = (u * lax.rsqrt(jnp.mean(u * u, axis=-1, keepdims=True) + RMS_EPS)) * ng_ref[...]
    acc = (jnp.dot(oa_ref[...], w1_ref[...], preferred_element_type=F32)
           + jnp.dot(un.astype(BF16), w2_ref[...], preferred_element_type=F32))
    o_ref[...] = x_ref[...] + gate_ref[...] * acc


def out_proj(o_att, yf, yb, z, norm_g, w1, w2, x2d, gate, rows_per_mod):
    t, d = x2d.shape
    tm = _tile(rows_per_mod, 512)
    tpm = rows_per_mod // tm
    nb = gate.shape[0]
    row = lambda i: (i, 0)
    const = lambda i: (0, 0)
    return pl.pallas_call(
        _outproj_kernel,
        grid=(t // tm,),
        in_specs=[pl.BlockSpec((tm, d), row), pl.BlockSpec((tm, d), row), pl.BlockSpec((tm, d), row),
                  pl.BlockSpec((tm, d), row), pl.BlockSpec((1, d), const),
                  pl.BlockSpec(w1.shape, const), pl.BlockSpec(w2.shape, const),
                  pl.BlockSpec((tm, d), row), pl.BlockSpec((None, 1, d), lambda i: (i // tpm, 0, 0))],
        out_specs=pl.BlockSpec((tm, d), row),
        out_shape=jax.ShapeDtypeStruct((t, d), F32),
        compiler_params=_params(("parallel",)),
        name="mixer_out_proj",
    )(o_att, yf, yb, z, norm_g.reshape(1, d), w1, w2, x2d, gate.reshape(nb, 1, d))


def _mm_res_kernel(a_ref, w_ref, x_ref, gate_ref, o_ref):
    acc = jnp.dot(a_ref[...], w_ref[...], preferred_element_type=F32)
    o_ref[...] = x_ref[...] + gate_ref[...] * acc


def matmul_residual(a, w, x2d, gate, rows_per_mod):
    t, d = x2d.shape
    tm = _tile(rows_per_mod, 512)
    tpm = rows_per_mod // tm
    nb = gate.shape[0]
    return pl.pallas_call(
        _mm_res_kernel,
        grid=(t // tm,),
        in_specs=[pl.BlockSpec((tm, a.shape[1]), lambda i: (i, 0)),
                  pl.BlockSpec(w.shape, lambda i: (0, 0)),
                  pl.BlockSpec((tm, d), lambda i: (i, 0)),
                  pl.BlockSpec((None, 1, d), lambda i: (i // tpm, 0, 0))],
        out_specs=pl.BlockSpec((tm, d), lambda i: (i, 0)),
        out_shape=jax.ShapeDtypeStruct((t, d), F32),
        compiler_params=_params(("parallel",)),
        name="matmul_gated_residual",
    )(a, w, x2d, gate.reshape(nb, 1, d))


def _dft_stage1_kernel(m_ref, z_ref, o_ref):
    z = z_ref[...]
    half = z.shape[1] // 2
    zs = jnp.concatenate([z[:, :half], z[:, half:]], axis=0)
    a = jnp.dot(m_ref[...], zs, preferred_element_type=F32)
    n1 = a.shape[0] // 2
    o_ref[0] = a[:n1].astype(o_ref.dtype)
    o_ref[1] = a[n1:].astype(o_ref.dtype)


def _dft_stage2_kernel(m_ref, a_ref, o_ref):
    o_ref[...] = jnp.dot(m_ref[...], a_ref[...], preferred_element_type=F32).astype(o_ref.dtype)


def _dft_tables(l, d):
    n1 = FOURIER_N1
    n2 = l // n1
    gw = d // FOURIER_GROUPS
    two_pi = 2.0 * math.pi
    ch = jnp.arange(gw, dtype=I32)
    ph = (ch[:, None] * ch[None, :]) % gw
    ang = ph.astype(F32) * (two_pi / gw)
    cc, sc = jnp.cos(ang) * gw ** -0.5, jnp.sin(ang) * gw ** -0.5
    eye = jnp.eye(FOURIER_GROUPS, dtype=F32)
    w_cs = jnp.concatenate([jnp.kron(eye, cc), -jnp.kron(eye, sc)], axis=1)
    k1 = jnp.arange(n1, dtype=I32)
    pos = jnp.arange(n2, dtype=I32)[:, None, None] + n2 * jnp.arange(n1, dtype=I32)[None, None, :]
    ang1 = ((k1[None, :, None] * pos) % l).astype(F32) * (two_pi / l)
    gc, gs = jnp.cos(ang1) * l ** -0.5, jnp.sin(ang1) * l ** -0.5
    m1 = jnp.concatenate([jnp.concatenate([gc, gs], axis=2),
                          jnp.concatenate([-gs, gc], axis=2)], axis=1)
    k2 = jnp.arange(n2, dtype=I32)
    ang2 = ((k2[:, None] * k2[None, :]) % n2).astype(F32) * (two_pi / n2)
    m2 = jnp.concatenate([jnp.cos(ang2), jnp.sin(ang2)], axis=1)
    return w_cs.astype(BF16), m1.astype(BF16), m2.astype(BF16)


def fourier_positions(z, m1, m2, batch):
    t, d2 = z.shape
    d = d2 // 2
    l = t // batch
    n1 = FOURIER_N1
    n2 = l // n1
    a = pl.pallas_call(
        _dft_stage1_kernel,
        grid=(batch, n2),
        in_specs=[pl.BlockSpec((None, 2 * n1, 2 * n1), lambda b, j: (j, 0, 0)),
                  pl.BlockSpec((None, n1, d2), lambda b, j: (b, 0, j))],
        out_specs=pl.BlockSpec((None, 2, None, n1, d), lambda b, j: (b, 0, j, 0, 0)),
        out_shape=jax.ShapeDtypeStruct((batch, 2, n2, n1, d), BF16),
        compiler_params=_params(("parallel", "parallel")),
        name="dft_stage1",
    )(m1, z.reshape(batch, n1, n2 * d2))
    cols = n1 * d
    tn = _tile(cols, 8192)
    f = pl.pallas_call(
        _dft_stage2_kernel,
        grid=(batch, cols // tn),
        in_specs=[pl.BlockSpec((n2, 2 * n2), lambda b, j: (0, 0)),
                  pl.BlockSpec((None, 2 * n2, tn), lambda b, j: (b, 0, j))],
        out_specs=pl.BlockSpec((None, n2, tn), lambda b, j: (b, 0, j)),
        out_shape=jax.ShapeDtypeStruct((batch, n2, cols), BF16),
        compiler_params=_params(("parallel", "parallel")),
        name="dft_stage2",
    )(m2, a.reshape(batch, 2 * n2, cols))
    return f.reshape(t, d)


SUB = 8


def _to_row_tiles(ref, val):
    rows = val.shape[0]
    for j in range(SUB):
        ref[pl.ds(j, rows, stride=SUB), :] = val[:, j * LANES:(j + 1) * LANES]


def _from_row_tiles(ref, rows):
    return jnp.concatenate([ref[pl.ds(j, rows, stride=SUB), :] for j in range(SUB)], axis=1)


def _row_tile(ref, row):
    return ref.at[pl.ds(pl.multiple_of(row * SUB, SUB), SUB), :]


def _route_kernel(x_ref, g_ref, sh_ref, sc_ref, wr_ref, br_ref, h_ref, ti_ref, tg_ref):
    h = _norm_mod(x_ref[...], g_ref[...], sh_ref[...], sc_ref[...])
    _to_row_tiles(h_ref, h)
    logits = jnp.dot(h, wr_ref[...], precision=HIGHEST, preferred_element_type=F32) + br_ref[...]
    lane = lax.broadcasted_iota(I32, logits.shape, 1)
    vals, idxs = [], []
    cur = logits
    for _ in range(TOP_K):
        m = jnp.max(cur, axis=-1, keepdims=True)
        idx = jnp.min(jnp.where(cur == m, lane, LANES), axis=-1, keepdims=True)
        vals.append(m)
        idxs.append(idx)
        cur = jnp.where(lane == idx, -jnp.inf, cur)
    es = [jnp.exp(v - vals[0]) for v in vals]
    tot = es[0] + es[1] + es[2] + es[3]
    ti = jnp.zeros(logits.shape, I32)
    tg = jnp.zeros(logits.shape, F32)
    for k in range(TOP_K):
        ti = jnp.where(lane == k, idxs[k], ti)
        tg = jnp.where(lane == k, es[k] / tot, tg)
    ti_ref[...] = ti
    tg_ref[...] = tg


def route(x2d, g, shift, scale, rows_per_mod, w_router, b_router):
    t, d = x2d.shape
    assert d == SUB * LANES
    nb = shift.shape[0]
    tm = _tile(rows_per_mod, 512)
    tpm = rows_per_mod // tm
    ne = w_router.shape[1]
    wr = jnp.pad(w_router, ((0, 0), (0, LANES - ne)))
    br = jnp.pad(b_router.reshape(1, ne), ((0, 0), (0, LANES - ne)), constant_values=NEG_BIG)
    return pl.pallas_call(
        _route_kernel,
        grid=(t // tm,),
        in_specs=[pl.BlockSpec((tm, d), lambda i: (i, 0)),
                  pl.BlockSpec((1, d), lambda i: (0, 0)),
                  pl.BlockSpec((None, 1, d), lambda i: (i // tpm, 0, 0)),
                  pl.BlockSpec((None, 1, d), lambda i: (i // tpm, 0, 0)),
                  pl.BlockSpec((d, LANES), lambda i: (0, 0)),
                  pl.BlockSpec((1, LANES), lambda i: (0, 0))],
        out_specs=[pl.BlockSpec((tm * SUB, LANES), lambda i: (i, 0)),
                   pl.BlockSpec((tm, LANES), lambda i: (i, 0)),
                   pl.BlockSpec((tm, LANES), lambda i: (i, 0))],
        out_shape=[jax.ShapeDtypeStruct((t * SUB, LANES), F32), jax.ShapeDtypeStruct((t, LANES), I32),
                   jax.ShapeDtypeStruct((t, LANES), F32)],
        compiler_params=_params(("parallel",)),
        name="moe_route",
    )(x2d, g.reshape(1, d), shift.reshape(nb, 1, d), scale.reshape(nb, 1, d), wr, br)


def _rank_kernel(ti_ref, rank_ref, cnt_ref, carry):
    @pl.when(pl.program_id(0) == 0)
    def _():
        carry[...] = jnp.zeros(carry.shape, carry.dtype)

    ti = ti_ref[...]
    tm = ti.shape[0]
    lane = lax.broadcasted_iota(I32, ti.shape, 1)
    earlier = (lax.broadcasted_iota(I32, (tm, tm), 1) < lax.broadcasted_iota(I32, (tm, tm), 0)).astype(BF16)
    base = carry[...]
    rank = jnp.zeros(ti.shape, I32)
    for k in range(TOP_K):
        hit = lane == ti[:, k:k + 1]
        onehot = jnp.where(hit, 1.0, 0.0)
        before = jnp.dot(earlier, onehot.astype(BF16), preferred_element_type=F32)
        rk = jnp.sum(jnp.where(hit, base + before, 0.0), axis=-1, keepdims=True)
        rank = jnp.where(lane == k, rk.astype(I32), rank)
        base = base + jnp.sum(onehot, axis=0, keepdims=True)
    rank_ref[...] = rank
    carry[...] = base
    cnt_ref[...] = base


def _dest_kernel(ti_ref, rank_ref, start_ref, dest_ref):
    ti = ti_ref[...]
    lane = lax.broadcasted_iota(I32, ti.shape, 1)
    dest = jnp.zeros(ti.shape, I32)
    for k in range(TOP_K):
        start = jnp.sum(jnp.where(lane == ti[:, k:k + 1], start_ref[...], 0), axis=-1, keepdims=True)
        dest = jnp.where(lane == k, start + rank_ref[:, k:k + 1], dest)
    dest_ref[...] = dest


def moe_plan(ti, t):
    tb = MOE_ROWS
    tm = _tile(t, 512)
    rank, cnt = pl.pallas_call(
        _rank_kernel,
        grid=(t // tm,),
        in_specs=[pl.BlockSpec((tm, LANES), lambda i: (i, 0))],
        out_specs=[pl.BlockSpec((tm, LANES), lambda i: (i, 0)), pl.BlockSpec((1, LANES), lambda i: (0, 0))],
        out_shape=[jax.ShapeDtypeStruct((t, LANES), I32), jax.ShapeDtypeStruct((1, LANES), F32)],
        scratch_shapes=[pltpu.VMEM((1, LANES), F32)],
        compiler_params=_params(("arbitrary",)),
        name="moe_rank",
    )(ti)
    counts = cnt[0, :N_EXPERTS].astype(I32)
    padded = (counts + tb - 1) // tb * tb
    pend = jnp.cumsum(padded)
    pstart = jnp.pad(pend - padded, (0, LANES - N_EXPERTS)).reshape(1, LANES)
    dest = pl.pallas_call(
        _dest_kernel,
        grid=(t // tm,),
        in_specs=[pl.BlockSpec((tm, LANES), lambda i: (i, 0)), pl.BlockSpec((tm, LANES), lambda i: (i, 0)),
                  pl.BlockSpec((1, LANES), lambda i: (0, 0))],
        out_specs=pl.BlockSpec((tm, LANES), lambda i: (i, 0)),
        out_shape=jax.ShapeDtypeStruct((t, LANES), I32),
        compiler_params=_params(("parallel",)),
        name="moe_dest",
    )(ti, rank, pstart)
    dest = dest[:, :TOP_K].reshape(-1)
    n_blocks = (t * TOP_K + N_EXPERTS * (tb - 1)) // tb + 1
    spill = (TOP_K * t + jnp.arange(n_blocks * tb, dtype=I32) % tb) * TOP_K
    codes = spill.at[dest].set(jnp.arange(t * TOP_K, dtype=I32))
    starts = jnp.arange(n_blocks, dtype=I32) * tb
    blk_e = jnp.minimum(jnp.sum((pend[None, :] <= starts[:, None]).astype(I32), axis=1), N_EXPERTS - 1)
    nact = (pend[-1] // tb).astype(I32).reshape(1)
    return codes, blk_e, nact


def _expert_kernel(tb, n_tok, be_ref, code_ref, nact_ref, h_hbm, wgu_ref, bgu_ref, wdn_ref, bdn_ref, y_hbm,
                   xbuf, ybuf, wgu_s, wdn_wide, wdn_s, gsem, ssem):
    i = pl.program_id(0)
    n_blocks = pl.num_programs(0)
    nact = nact_ref[0]
    de = wdn_ref.shape[0]
    slot = i % 2
    other = 1 - slot
    def start_gather(blk, r, dst_slot):
        tok = code_ref[blk * tb + r] >> 2
        tok = jnp.where(tok < n_tok, tok, 0)
        pltpu.make_async_copy(_row_tile(h_hbm, tok), _row_tile(xbuf.at[dst_slot], r), gsem.at[dst_slot]).start()

    def start_scatter(blk, r, src_slot):
        code = code_ref[blk * tb + r]
        row = (code & (TOP_K - 1)) * n_tok + (code >> 2)
        pltpu.make_async_copy(_row_tile(ybuf.at[src_slot], r), _row_tile(y_hbm, row), ssem.at[src_slot]).start()

    def wait_gather(s):
        pltpu.make_async_copy(h_hbm.at[pl.ds(0, tb * SUB), :], xbuf.at[s], gsem.at[s]).wait()

    def wait_scatter(s):
        pltpu.make_async_copy(ybuf.at[s], y_hbm.at[pl.ds(0, tb * SUB), :], ssem.at[s]).wait()

    @pl.when(i == 0)
    def _():
        wdn_wide[...] = jnp.zeros(wdn_wide.shape, wdn_wide.dtype)
        ybuf[...] = jnp.zeros(ybuf.shape, ybuf.dtype)

        def first(r, carry):
            start_gather(0, r, 0)
            return carry
        lax.fori_loop(0, tb, first, 0, unroll=8)

    @pl.when((i < nact) & ((i == 0) | (be_ref[i] != be_ref[jnp.maximum(i - 1, 0)])))
    def _():
        wgu_s[...] = wgu_ref[...].astype(BF16)
        for c in range(wdn_wide.shape[0]):
            wdn_wide[c, pl.ds(0, de, stride=2), :] = wdn_ref[:, c * LANES:(c + 1) * LANES]
            wdn_s[:, c * LANES:(c + 1) * LANES] = wdn_wide[c].astype(BF16)

    @pl.when(i < nact)
    def _():
        wait_gather(slot)

        @pl.when(i >= 1)
        def _():
            wait_scatter(slot)

        prev = jnp.where(i == 0, n_blocks - 1, i - 1)
        x = _from_row_tiles(xbuf.at[slot], tb).astype(BF16)
        even = lax.broadcasted_iota(I32, (tb, LANES), 1) % 2 == 0
        cw = 2 * LANES
        n_chunks = wgu_s.shape[1] // cw
        rows_per_chunk = tb // n_chunks
        parts = []
        for c in range(n_chunks):
            gu = jnp.dot(x, wgu_s[:, c * cw:(c + 1) * cw], preferred_element_type=F32) + bgu_ref[:, c * cw:(c + 1) * cw]
            for c0 in range(0, cw, LANES):
                v = gu[:, c0:c0 + LANES]
                glu = jnp.minimum(v, SWIGLU_LIMIT)
                lin = jnp.clip(v, -SWIGLU_LIMIT, SWIGLU_LIMIT) + 1.0
                act = glu * jax.nn.sigmoid(SWIGLU_ALPHA * glu) * pltpu.roll(lin, LANES - 1, 1)
                parts.append(jnp.where(even, act, 0.0).astype(BF16))
            for r in range(c * rows_per_chunk, (c + 1) * rows_per_chunk):
                start_gather(i + 1, r, other)
                start_scatter(prev, r, other)
        y = jnp.dot(jnp.concatenate(parts, axis=1), wdn_s[...], preferred_element_type=F32) + bdn_ref[...]
        _to_row_tiles(ybuf.at[slot], y)

    @pl.when(i == nact)
    def _():
        wait_gather(slot)
        wait_scatter(slot)

        def last(r, carry):
            start_scatter(i - 1, r, other)
            return carry
        lax.fori_loop(0, tb, last, 0, unroll=8)
        wait_scatter(other)


def expert_ffn(h_rt, codes, blk_e, nact, layer, w_gu, b_gu, w_dn, b_dn):
    r = codes.shape[0]
    n_tok = h_rt.shape[0] // SUB
    tb = MOE_ROWS
    _, ne, d, de2 = w_gu.shape
    de = de2 // 2
    b_gu = b_gu.reshape(-1, ne, 1, de2)
    b_dn = b_dn.reshape(-1, ne, 1, d)
    return pl.pallas_call(
        functools.partial(_expert_kernel, tb, n_tok),
        grid_spec=pltpu.PrefetchScalarGridSpec(
            num_scalar_prefetch=3, grid=(r // tb,),
            in_specs=[pl.BlockSpec(memory_space=pl.ANY),
                      pl.BlockSpec((None, None, d, de2), lambda i, be, cd, na: (layer, be[i], 0, 0)),
                      pl.BlockSpec((None, None, 1, de2), lambda i, be, cd, na: (layer, be[i], 0, 0)),
                      pl.BlockSpec((None, None, de, d), lambda i, be, cd, na: (layer, be[i], 0, 0)),
                      pl.BlockSpec((None, None, 1, d), lambda i, be, cd, na: (layer, be[i], 0, 0))],
            out_specs=pl.BlockSpec(memory_space=pl.ANY),
            scratch_shapes=[pltpu.VMEM((2, tb * SUB, LANES), F32), pltpu.VMEM((2, tb * SUB, LANES), F32),
                            pltpu.VMEM((d, de2), BF16), pltpu.VMEM((d // LANES, de2, LANES), F32),
                            pltpu.VMEM((de2, d), BF16),
                            pltpu.SemaphoreType.DMA((2,)), pltpu.SemaphoreType.DMA((2,))]),
        out_shape=jax.ShapeDtypeStruct(((TOP_K * n_tok + tb) * SUB, LANES), F32),
        compiler_params=_params(("arbitrary",), 58),
        name="moe_experts",
    )(blk_e, codes, nact, h_rt, w_gu, b_gu, w_dn, b_dn)


def _combine_kernel(y0_ref, y1_ref, y2_ref, y3_ref, x_ref, gate_ref, tg_ref, o_ref):
    tg = tg_ref[...]
    tm = x_ref.shape[0]
    ys = [_from_row_tiles(y_ref, tm) * tg[:, k:k + 1] for k, y_ref in enumerate((y0_ref, y1_ref, y2_ref, y3_ref))]
    o_ref[...] = x_ref[...] + gate_ref[...] * ((ys[0] + ys[1]) + (ys[2] + ys[3]))


def moe_combine(y4, tg, x2d, gate, rows_per_mod):
    t, d = x2d.shape
    nb = gate.shape[0]
    tm = _tile(rows_per_mod, MOE_ROWS)
    tpm = rows_per_mod // tm
    nt = t // tm
    planes = [pl.BlockSpec((tm * SUB, LANES), functools.partial(lambda k, i: (k * nt + i, 0), k))
              for k in range(TOP_K)]
    return pl.pallas_call(
        _combine_kernel,
        grid=(nt,),
        in_specs=planes + [
                  pl.BlockSpec((tm, d), lambda i: (i, 0)),
                  pl.BlockSpec((None, 1, d), lambda i: (i // tpm, 0, 0)),
                  pl.BlockSpec((tm, LANES), lambda i: (i, 0))],
        out_specs=pl.BlockSpec((tm, d), lambda i: (i, 0)),
        out_shape=jax.ShapeDtypeStruct((t, d), F32),
        compiler_params=_params(("parallel",)),
        name="moe_combine",
    )(y4, y4, y4, y4, x2d, gate.reshape(nb, 1, d), tg)


def moe_layer(x2d, norm_g, shift, scale, gate, rows_per_mod, w_router, b_router, layer, w_gu, b_gu, w_dn, b_dn):
    t = x2d.shape[0]
    h_rt, ti, tg = route(x2d, norm_g, shift, scale, rows_per_mod, w_router, b_router)
    codes, blk_e, nact = moe_plan(ti, t)
    y4 = expert_ffn(h_rt, codes, blk_e, nact, layer, w_gu, b_gu, w_dn, b_dn)
    return moe_combine(y4, tg, x2d, gate, rows_per_mod)


def _rope_tables(s):
    n_freq = DA_HEAD_DIM // 4
    freqs = ROPE_BASE ** (-jnp.arange(n_freq, dtype=F32) / n_freq)
    pos = jnp.arange(s, dtype=I32)
    ang_r = (pos // GRID_W).astype(F32)[:, None] * freqs
    ang_c = (pos % GRID_W).astype(F32)[:, None] * freqs
    ang = jnp.concatenate([ang_r, ang_r, ang_c, ang_c], axis=-1)
    ang = jnp.concatenate([ang, ang], axis=-1)
    return jnp.cos(ang), jnp.sin(ang)


def _even_layer(x2d, ctx2d, mods, norm_g1, w_in, w_out, q_norm_g, k_norm_g, da_lambda, da_subln_g,
                conv_xb_w, conv_xb_b, conv_c_w, conv_c_b, dt_bias, a_log, d_skip, ssm_norm_g,
                lam_init, batch):
    sx1, cx1, gx1, sc1, cc1 = mods
    t, d = x2d.shape
    s = t // batch
    ctx_len = ctx2d.shape[0] // batch
    qkw = DA_HEADS * 2 * DA_HEAD_DIM
    vw = DA_HEADS * DA_V_DIM
    col_q, col_z = 0, qkw
    col_c = col_z + SSM_D_INNER
    col_k = col_c + SSM_BC_W
    col_v = col_k + qkw
    col_xb = col_v + vw
    col_dt = col_xb + SSM_XB_W
    wb = w_in.astype(BF16)
    w_q, w_k, w_v = wb[:, col_q:col_q + qkw], wb[:, col_k:col_k + qkw], wb[:, col_v:col_v + vw]
    w_z = wb[:, col_z:col_z + SSM_D_INNER]
    w_xbc = jnp.concatenate([wb[:, col_xb:col_xb + SSM_XB_W], wb[:, col_c:col_c + SSM_BC_W]], axis=1)
    w_dt = jnp.pad(wb[:, col_dt:col_dt + 2 * SSM_HEADS], ((0, 0), (0, LANES - 2 * SSM_HEADS)))
    weights = [w_q, w_k, w_v, w_z, w_xbc, w_dt]
    dts = [BF16, BF16, BF16, BF16, BF16, F32]
    q_u, k_u, v_x, z_x, xbc_u, dt_x = nm_panel(x2d, norm_g1, sx1, cx1, s, weights, dts)
    _, kc_u, v_c, _, xbc_cu, dt_c = nm_panel(ctx2d, norm_g1, sc1, cc1, batch * ctx_len, weights, dts)

    lam = (jnp.exp(jnp.sum(da_lambda[0] * da_lambda[1])) -
           jnp.exp(jnp.sum(da_lambda[2] * da_lambda[3]))).astype(F32) + lam_init
    a_neg = -jnp.exp(a_log.astype(F32)).reshape(-1)

    cos_t, sin_t = _rope_tables(s)
    q = qk_prep(q_u, q_norm_g, cos_t, sin_t, s, True, DA_HEAD_DIM ** -0.5 * LOG2E)
    score_bound = (1.01 * DA_HEAD_DIM ** 0.5 * LOG2E) * jnp.max(jnp.abs(q_norm_g)) * jnp.max(jnp.abs(k_norm_g))
    k_x = qk_prep(k_u, k_norm_g, cos_t, sin_t, s, True, 1.0)
    k_c = qk_prep(kc_u, k_norm_g, cos_t, sin_t, ctx_len, False, 1.0)
    k_all = jnp.concatenate([k_x.reshape(batch, s, qkw), k_c.reshape(batch, ctx_len, qkw)], axis=1)
    v_all = jnp.concatenate([v_x.reshape(batch, s, vw), v_c.reshape(batch, ctx_len, vw)], axis=1)
    o_att = diff_attention(q, k_all, v_all, lam, score_bound, da_subln_g, lam_init, batch)

    conv_w = jnp.concatenate([conv_xb_w, conv_c_w], axis=1)
    conv_b = jnp.concatenate([conv_xb_b, conv_c_b], axis=0)
    xbc_x = conv_silu(xbc_u, conv_w, conv_b, s)
    xbc_c = conv_silu(xbc_cu, conv_w, conv_b, ctx_len)
    hshape = (batch, SSM_GROUPS, SSM_STATE, SSM_HPG * SSM_HEAD_DIM)
    zero_h = jnp.zeros(hshape, F32)
    dt_bias_f = dt_bias.reshape(-1).astype(F32)
    _, _, h_f, h_b = ssd_bidir(xbc_c, dt_c, dt_bias_f, a_neg, d_skip, zero_h, zero_h, batch)
    y_f, y_b, _, _ = ssd_bidir(xbc_x, dt_x, dt_bias_f, a_neg, d_skip, h_f, h_b, batch)
    wo = w_out.astype(BF16)
    return out_proj(o_att, y_f, y_b, z_x, ssm_norm_g, wo[:vw], wo[vw:], x2d, gx1, s)


def _odd_layer(x2d, mods, norm_g1, w_fourier, batch):
    sx1, cx1, gx1 = mods
    t, d = x2d.shape
    s = t // batch
    w_cs, m1, m2 = _dft_tables(s, d)
    (z,) = nm_panel(x2d, norm_g1, sx1, cx1, s, [w_cs], [BF16])
    f = fourier_positions(z, m1, m2, batch)
    return matmul_residual(f, w_fourier.astype(BF16), x2d, gx1, s)


def kernel(x, c, ctx, c_ctx, ada_w, ada_b, norm_g, w_in, w_out, q_norm_g, k_norm_g, da_lambda, da_subln_g, conv_xb_w, conv_xb_b, conv_c_w, conv_c_b, dt_bias, a_log, d_skip, ssm_norm_g, w_fourier, w_router, b_router, w_gate_up, b_gate_up, w_down, b_down):
    b, s, d = x.shape
    depth = ada_w.shape[0]
    assert b + 1 <= 8
    cvecs = jnp.concatenate([c, c_ctx[None, :], jnp.zeros((8 - b - 1, d), F32)], axis=0)
    mods = ada_all(cvecs, ada_w, ada_b)
    x2d = x.reshape(b * s, d)
    ctx2d = ctx.reshape(-1, d)
    for i in range(depth):
        m = mods[i].reshape(8, 6, d)
        sx1, cx1, gx1, sx2, cx2, gx2 = [m[:b, j] for j in range(6)]
        if i % 2 == 0:
            e = i // 2
            lam_init = 0.8 - 0.6 * math.exp(-0.3 * i)
            if any(j % 2 == 0 for j in range(i + 1, depth)):
                raise NotImplementedError("context stream output is only needed for depth > 2")
            sc1, cc1 = m[b:b + 1, 0], m[b:b + 1, 1]
            x2d = _even_layer(x2d, ctx2d, (sx1, cx1, gx1, sc1, cc1), norm_g[i, 0], w_in[e], w_out[e],
                              q_norm_g[e], k_norm_g[e], da_lambda[e], da_subln_g[e], conv_xb_w[e],
                              conv_xb_b[e], conv_c_w[e], conv_c_b[e], dt_bias[e], a_log[e], d_skip[e],
                              ssm_norm_g[e], lam_init, b)
        else:
            x2d = _odd_layer(x2d, (sx1, cx1, gx1), norm_g[i, 0], w_fourier[i // 2], b)
        x2d = moe_layer(x2d, norm_g[i, 1], sx2, cx2, gx2, s, w_router[i], b_router[i],
                        i, w_gate_up, b_gate_up, w_down, b_down)
    return x2d.reshape(b, s, d)
```

```python
import functools
import math

import jax
import jax.numpy as jnp
from jax import lax
from jax.experimental import pallas as pl
from jax.experimental.pallas import tpu as pltpu

F32 = jnp.float32
BF16 = jnp.bfloat16
I32 = jnp.int32

RMS_EPS = 1e-6
GRID_W = 64
ROPE_BASE = 10000.0
DA_HEADS = 8
DA_HEAD_DIM = 64
DA_V_DIM = 128
SSM_HEADS = 16
SSM_HEAD_DIM = 64
SSM_GROUPS = 2
SSM_HPG = 8
SSM_STATE = 128
SSM_CONV = 5
SSM_CHUNK = 128
SSM_D_INNER = 1024
SSM_BC_W = 256
SSM_XB_W = 1280
FOURIER_GROUPS = 4
FOURIER_N1 = 128
N_EXPERTS = 32
TOP_K = 4
SWIGLU_LIMIT = 7.0
SWIGLU_ALPHA = 1.702
MOE_ROWS = 256
LANES = 128
NEG_BIG = -1e30
MIB = 1024 * 1024
HIGHEST = lax.Precision.HIGHEST


def _params(sem, vmem_mib=48):
    return pltpu.CompilerParams(dimension_semantics=sem, vmem_limit_bytes=vmem_mib * MIB)


def _tile(n, pref):
    t = min(n, pref)
    while n % t:
        t //= 2
    return t


def _silu(v):
    return v * jax.nn.sigmoid(v)


def _norm_mod(xf, g, shift, scale):
    r = lax.rsqrt(jnp.mean(xf * xf, axis=-1, keepdims=True) + RMS_EPS)
    return ((xf * r) * g) * (1.0 + scale) + shift


def _ada_kernel(a_ref, w_ref, b_ref, o_ref):
    s = _silu(a_ref[...])
    o_ref[...] = jnp.dot(s.astype(BF16), w_ref[...].astype(BF16),
                         preferred_element_type=F32) + b_ref[...]


def ada_all(cvecs, ada_w, ada_b):
    depth, d, n = ada_w.shape
    tn = _tile(n, 1536)
    return pl.pallas_call(
        _ada_kernel,
        grid=(depth, n // tn),
        in_specs=[pl.BlockSpec((8, d), lambda l, j: (0, 0)),
                  pl.BlockSpec((None, d, tn), lambda l, j: (l, 0, j)),
                  pl.BlockSpec((None, 1, tn), lambda l, j: (l, 0, j))],
        out_specs=pl.BlockSpec((None, 8, tn), lambda l, j: (l, 0, j)),
        out_shape=jax.ShapeDtypeStruct((depth, 8, n), F32),
        compiler_params=_params(("parallel", "parallel")),
        name="ada_mod",
    )(cvecs, ada_w, ada_b.reshape(depth, 1, n))


def _nm_panel_kernel(n_w, x_ref, g_ref, sh_ref, sc_ref, *refs):
    w_refs, o_refs = refs[:n_w], refs[n_w:2 * n_w]
    hb = _norm_mod(x_ref[...], g_ref[...], sh_ref[...], sc_ref[...]).astype(BF16)
    for w_ref, o_ref in zip(w_refs, o_refs):
        n = w_ref.shape[1]
        cw = _tile(n, 512)
        for c0 in range(0, n, cw):
            o_ref[:, c0:c0 + cw] = jnp.dot(hb, w_ref[:, c0:c0 + cw],
                                           preferred_element_type=F32).astype(o_ref.dtype)


def nm_panel(x2d, g, shift, scale, rows_per_mod, weights, out_dtypes, tm_pref=256):
    t, d = x2d.shape
    nb = shift.shape[0]
    tm = _tile(rows_per_mod, tm_pref)
    tpm = rows_per_mod // tm
    n_w = len(weights)
    in_specs = [pl.BlockSpec((tm, d), lambda i: (i, 0)),
                pl.BlockSpec((1, d), lambda i: (0, 0)),
                pl.BlockSpec((None, 1, d), lambda i: (i // tpm, 0, 0)),
                pl.BlockSpec((None, 1, d), lambda i: (i // tpm, 0, 0))]
    in_specs += [pl.BlockSpec(w.shape, lambda i: (0, 0)) for w in weights]
    out_specs = [pl.BlockSpec((tm, w.shape[1]), lambda i: (i, 0)) for w in weights]
    out_shape = [jax.ShapeDtypeStruct((t, w.shape[1]), dt) for w, dt in zip(weights, out_dtypes)]
    return pl.pallas_call(
        functools.partial(_nm_panel_kernel, n_w),
        grid=(t // tm,),
        in_specs=in_specs, out_specs=out_specs, out_shape=out_shape,
        compiler_params=_params(("parallel",), 56),
        name="norm_mod_proj",
    )(x2d, g.reshape(1, d), shift.reshape(nb, 1, d), scale.reshape(nb, 1, d), *weights)


def _qk_prep_kernel(rope, out_scale, u_ref, g_ref, seg_ref, cos_ref, sin_ref, o_ref):
    seg = seg_ref[...]
    g = g_ref[...]
    n_heads = u_ref.shape[1] // LANES
    for c in range(n_heads):
        u = u_ref[:, c * LANES:(c + 1) * LANES].astype(F32)
        u2 = u * u
        hi = u2.astype(BF16)
        lo = (u2 - hi.astype(F32)).astype(BF16)
        ss = (jnp.dot(hi, seg, preferred_element_type=F32)
              + jnp.dot(lo, seg, preferred_element_type=F32))
        nrm = (u * lax.rsqrt(ss * (1.0 / DA_HEAD_DIM) + RMS_EPS)) * g
        if rope:
            lane = lax.broadcasted_iota(I32, nrm.shape, 1)
            first = (lane % 32) < 16
            rot = jnp.where(first, -pltpu.roll(nrm, LANES - 16, 1), pltpu.roll(nrm, 16, 1))
            nrm = nrm * cos_ref[...] + rot * sin_ref[...]
        o_ref[:, c * LANES:(c + 1) * LANES] = (nrm * out_scale).astype(o_ref.dtype)


def qk_prep(u, gain, cos_t, sin_t, seq, rope, out_scale):
    t, w = u.shape
    tm = _tile(seq, 256)
    tps = seq // tm
    seg = (jnp.arange(LANES)[:, None] // DA_HEAD_DIM == jnp.arange(LANES)[None, :] // DA_HEAD_DIM)
    return pl.pallas_call(
        functools.partial(_qk_prep_kernel, rope, out_scale),
        grid=(t // tm,),
        in_specs=[pl.BlockSpec((tm, w), lambda i: (i, 0)),
                  pl.BlockSpec((1, LANES), lambda i: (0, 0)),
                  pl.BlockSpec((LANES, LANES), lambda i: (0, 0)),
                  pl.BlockSpec((tm, LANES), lambda i: (i % tps, 0)),
                  pl.BlockSpec((tm, LANES), lambda i: (i % tps, 0))],
        out_specs=pl.BlockSpec((tm, w), lambda i: (i, 0)),
        out_shape=jax.ShapeDtypeStruct((t, w), BF16),
        compiler_params=_params(("parallel",)),
        name="qk_norm_rope",
    )(u, jnp.tile(gain, 2).reshape(1, LANES), seg.astype(BF16), cos_t, sin_t)


LOG2E = math.log2(math.e)
ATTN_SHIFT_LIMIT = 60.0


def _split_components(q):
    lane = lax.broadcasted_iota(I32, q.shape, 1)
    zero = jnp.zeros_like(q)
    return jnp.where(lane < DA_HEAD_DIM, q, zero), jnp.where(lane >= DA_HEAD_DIM, q, zero)


def _attn_finish(n0, l0, n1, l1, lam, g, out_mult, o_ref):
    o = n0 / l0 - lam * (n1 / l1)
    o = (o * lax.rsqrt(jnp.mean(o * o, axis=-1, keepdims=True) + RMS_EPS)) * g
    o_ref[...] = (o * out_mult).astype(o_ref.dtype)


def _attn_online_kernel(tk, out_mult, sc_ref, q_ref, k_ref, v_ref, g_ref, o_ref):
    qs = _split_components(q_ref[...])
    tq = q_ref.shape[0]
    n_kv = k_ref.shape[0] // tk

    def body(i, carry):
        off = pl.multiple_of(i * tk, tk)
        k = k_ref[pl.ds(off, tk), :]
        v = v_ref[pl.ds(off, tk), :]
        new = []
        for c in range(2):
            m, l, acc = carry[c]
            s = lax.dot_general(qs[c], k, (((1,), (1,)), ((), ())), preferred_element_type=F32)
            m_new = jnp.maximum(m, jnp.max(s, axis=-1, keepdims=True))
            alpha = jnp.exp2(m - m_new)
            p = jnp.exp2(s - m_new)
            l_new = alpha * l + jnp.sum(p, axis=-1, keepdims=True)
            acc_new = alpha * acc + jnp.dot(p.astype(BF16), v, preferred_element_type=F32)
            new.append((m_new, l_new, acc_new))
        return tuple(new)

    init = tuple((jnp.full((tq, 1), -jnp.inf, F32), jnp.zeros((tq, 1), F32),
                  jnp.zeros((tq, DA_V_DIM), F32)) for _ in range(2))
    (_, l0, a0), (_, l1, a1) = lax.fori_loop(0, n_kv, body, init)
    _attn_finish(a0, l0, a1, l1, sc_ref[0], g_ref[...], out_mult, o_ref)


def _attn_shift_kernel(tk, out_mult, sc_ref, q_ref, k_ref, v_ref, g_ref, o_ref):
    qs = _split_components(q_ref[...])
    tq = q_ref.shape[0]
    n_kv = k_ref.shape[0] // tk
    shift = sc_ref[1]
    ones = jnp.ones((tk, LANES), BF16)

    sub = 2 * LANES

    def body(i, acc):
        acc = list(acc)
        for j in range(tk // sub):
            off = pl.multiple_of(i * tk + j * sub, sub)
            k = k_ref[pl.ds(off, sub), :]
            va = jnp.concatenate([v_ref[pl.ds(off, sub), :], ones[:sub]], axis=1)
            for c in range(2):
                s = lax.dot_general(qs[c], k, (((1,), (1,)), ((), ())), preferred_element_type=F32)
                acc[c] = acc[c] + jnp.dot(jnp.exp2(s - shift).astype(BF16), va, preferred_element_type=F32)
        return tuple(acc)

    zero = jnp.zeros((tq, 2 * LANES), F32)
    a0, a1 = lax.fori_loop(0, n_kv, body, (zero, zero))
    _attn_finish(a0[:, :LANES], a0[:, LANES:], a1[:, :LANES], a1[:, LANES:], sc_ref[0], g_ref[...], out_mult,
                 o_ref)


def diff_attention(q, k_all, v_all, lam, score_bound, subln_g, lam_init, batch):
    t, w = q.shape
    s = t // batch
    sk = k_all.shape[1]
    tq = _tile(s, 1024)
    tk = 2816 if sk % 2816 == 0 else _tile(sk, 256)
    nq = s // tq
    scalars = jnp.stack([lam, score_bound]).astype(F32)

    def run(body):
        return pl.pallas_call(
            functools.partial(body, tk, 1.0 - lam_init),
            grid=(batch, DA_HEADS, nq),
            in_specs=[pl.BlockSpec(memory_space=pltpu.SMEM),
                      pl.BlockSpec((tq, LANES), lambda b, h, i: (b * nq + i, h)),
                      pl.BlockSpec((None, sk, LANES), lambda b, h, i: (b, 0, h)),
                      pl.BlockSpec((None, sk, LANES), lambda b, h, i: (b, 0, h)),
                      pl.BlockSpec((1, LANES), lambda b, h, i: (0, 0))],
            out_specs=pl.BlockSpec((tq, LANES), lambda b, h, i: (b * nq + i, h)),
            out_shape=jax.ShapeDtypeStruct((t, w), BF16),
            compiler_params=_params(("parallel", "parallel", "parallel")),
            name="diff_attention",
        )(scalars, q, k_all, v_all, subln_g.reshape(1, LANES))

    return lax.cond(score_bound <= ATTN_SHIFT_LIMIT,
                    lambda: run(_attn_shift_kernel), lambda: run(_attn_online_kernel))


HALO = 16


def _conv_kernel(tps, p_ref, c_ref, n_ref, w_ref, b_ref, o_ref, ext_ref):
    t = pl.program_id(0) % tps
    tm = c_ref.shape[0]
    prev = p_ref[...].astype(F32)
    nxt = n_ref[...].astype(F32)
    ext_ref[0:HALO, :] = jnp.where(t == 0, 0.0, prev)
    ext_ref[HALO:HALO + tm, :] = c_ref[...].astype(F32)
    ext_ref[HALO + tm:2 * HALO + tm, :] = jnp.where(t == tps - 1, 0.0, nxt)
    width = c_ref.shape[1]
    cw = _tile(width, 256)
    pad = (SSM_CONV - 1) // 2
    for c0 in range(0, width, cw):
        acc = jnp.broadcast_to(b_ref[:, c0:c0 + cw], (tm, cw))
        for k in range(SSM_CONV):
            acc = acc + w_ref[k:k + 1, c0:c0 + cw] * ext_ref[HALO - pad + k:HALO - pad + k + tm, c0:c0 + cw]
        o_ref[:, c0:c0 + cw] = _silu(acc).astype(o_ref.dtype)


def conv_silu(u, w, b, seq):
    t, c = u.shape
    tm = _tile(seq, 512)
    tps = seq // tm
    hb = tm // HALO
    last = t // HALO - 1
    return pl.pallas_call(
        functools.partial(_conv_kernel, tps),
        grid=(t // tm,),
        in_specs=[pl.BlockSpec((HALO, c), lambda i: (jnp.maximum(i * hb - 1, 0), 0)),
                  pl.BlockSpec((tm, c), lambda i: (i, 0)),
                  pl.BlockSpec((HALO, c), lambda i: (jnp.minimum((i + 1) * hb, last), 0)),
                  pl.BlockSpec((SSM_CONV, c), lambda i: (0, 0)),
                  pl.BlockSpec((1, c), lambda i: (0, 0))],
        out_specs=pl.BlockSpec((tm, c), lambda i: (i, 0)),
        out_shape=jax.ShapeDtypeStruct((t, c), BF16),
        scratch_shapes=[pltpu.VMEM((tm + 2 * HALO, c), F32)],
        compiler_params=_params(("parallel",)),
        name="dwconv_silu",
    )(u, u, u, w, b.reshape(1, c))


def _softplus(v):
    return jnp.maximum(v, 0.0) + jnp.log1p(jnp.exp(-jnp.abs(v)))


def _pair(lane_lo, a, b):
    return jnp.where(lane_lo, a, b)


def _ssd_kernel(xf_ref, dtf_ref, dtTf_ref, xb_ref, dtb_ref, dtTb_ref, bias_r, a_r, bias_c, a_c,
                dsk_ref, h0f_ref, h0b_ref, yf_ref, yb_ref, hfo_ref, hbo_ref, hf_s, hb_s):
    c = pl.program_id(1)
    q = SSM_CHUNK
    nh = SSM_HEADS

    @pl.when(c == 0)
    def _():
        hf_s[...] = h0f_ref[...]
        hb_s[...] = h0b_ref[...]

    li = lax.broadcasted_iota(I32, (q, q), 0)
    si = lax.broadcasted_iota(I32, (q, q), 1)
    lower = li >= si
    upper = li <= si
    tri_l = lower.astype(F32)
    tri_u = upper.astype(F32)
    lane_lo = lax.broadcasted_iota(I32, (q, LANES), 1) < SSM_HEAD_DIM
    lane_lo1 = lax.broadcasted_iota(I32, (1, LANES), 1) < SSM_HEAD_DIM

    def bc(col):
        return jnp.broadcast_to(col, (q, LANES))

    def state_update(h_s, g, xs32, bg, wq, edge, base):
        parts, decs = [], []
        for pr in range(SSM_HPG // 2):
            h0 = g * SSM_HPG + 2 * pr
            wp = _pair(lane_lo, bc(wq[:, base + h0:base + h0 + 1]), bc(wq[:, base + h0 + 1:base + h0 + 2]))
            parts.append((xs32[:, h0 * 64:h0 * 64 + LANES] * wp).astype(BF16))
            decs.append(_pair(lane_lo1, jnp.broadcast_to(edge[:, base + h0:base + h0 + 1], (1, LANES)),
                              jnp.broadcast_to(edge[:, base + h0 + 1:base + h0 + 2], (1, LANES))))
        xw = jnp.concatenate(parts, axis=1)
        dec = jnp.exp(jnp.concatenate(decs, axis=1))
        upd = lax.dot_general(bg, xw, (((0,), (0,)), ((), ())), preferred_element_type=F32)
        h_s[g] = h_s[g] * dec + upd

    xbc = xf_ref[...]
    xs_b = xbc[:, :SSM_D_INNER]
    xs32 = xs_b.astype(F32)
    dt = _softplus(dtf_ref[...] + bias_r[...])
    da = dt * a_r[...]
    acs = jnp.dot(tri_l, da, precision=HIGHEST, preferred_element_type=F32)
    racs = jnp.dot(tri_u, da, precision=HIGHEST, preferred_element_type=F32)
    dt_t = _softplus(dtTf_ref[...] + bias_c[...])
    da_t = dt_t * a_c[...]
    acs_t = jnp.dot(da_t, tri_u, precision=HIGHEST, preferred_element_type=F32)
    racs_t = jnp.dot(da_t, tri_l, precision=HIGHEST, preferred_element_type=F32)
    last = acs[q - 1:q, :]
    wq = jnp.exp(last - acs) * dt
    for g in range(SSM_GROUPS):
        bg = xbc[:, SSM_D_INNER + g * SSM_STATE:SSM_D_INNER + (g + 1) * SSM_STATE]
        cg = xbc[:, SSM_D_INNER + SSM_BC_W + g * SSM_STATE:SSM_D_INNER + SSM_BC_W + (g + 1) * SSM_STATE]
        cb = lax.dot_general(cg, bg, (((1,), (1,)), ((), ())), preferred_element_type=F32)
        yoff = jnp.dot(cg, hf_s[g].astype(BF16), preferred_element_type=F32)
        for pr in range(SSM_HPG // 2):
            h0 = g * SSM_HPG + 2 * pr
            col0 = h0 * SSM_HEAD_DIM
            xp = xs_b[:, col0:col0 + LANES]
            ys = []
            for hh in (h0, h0 + 1):
                mf = jnp.exp(jnp.where(lower, acs[:, hh:hh + 1] - acs_t[hh:hh + 1, :], NEG_BIG)) \
                    * dt_t[hh:hh + 1, :]
                mb = jnp.exp(jnp.where(upper, racs[:, nh + hh:nh + hh + 1] - racs_t[nh + hh:nh + hh + 1, :],
                                       NEG_BIG)) * dt_t[nh + hh:nh + hh + 1, :]
                ys.append(jnp.dot((cb * (mf + mb)).astype(BF16), xp, preferred_element_type=F32))
            ef = jnp.exp(_pair(lane_lo, bc(acs[:, h0:h0 + 1]), bc(acs[:, h0 + 1:h0 + 2])))
            yf_ref[:, col0:col0 + LANES] = (_pair(lane_lo, ys[0], ys[1])
                                            + yoff[:, 2 * pr * 64:2 * pr * 64 + LANES] * ef
                                            + xs32[:, col0:col0 + LANES] * dsk_ref[:, col0:col0 + LANES])
        state_update(hf_s, g, xs32, bg, wq, last, 0)

    xbc2 = xb_ref[...]
    xs2 = xbc2[:, :SSM_D_INNER].astype(F32)
    dt2 = _softplus(dtb_ref[...] + bias_r[...])
    racs2 = jnp.dot(tri_u, dt2 * a_r[...], precision=HIGHEST, preferred_element_type=F32)
    first = racs2[0:1, :]
    wq2 = jnp.exp(first - racs2) * dt2
    for g in range(SSM_GROUPS):
        bg = xbc2[:, SSM_D_INNER + g * SSM_STATE:SSM_D_INNER + (g + 1) * SSM_STATE]
        cg = xbc2[:, SSM_D_INNER + SSM_BC_W + g * SSM_STATE:SSM_D_INNER + SSM_BC_W + (g + 1) * SSM_STATE]
        yoff = jnp.dot(cg, hb_s[g].astype(BF16), preferred_element_type=F32)
        for pr in range(SSM_HPG // 2):
            h0 = g * SSM_HPG + 2 * pr
            col0 = h0 * SSM_HEAD_DIM
            eb = jnp.exp(_pair(lane_lo, bc(racs2[:, nh + h0:nh + h0 + 1]), bc(racs2[:, nh + h0 + 1:nh + h0 + 2])))
            yb_ref[:, col0:col0 + LANES] = yoff[:, 2 * pr * 64:2 * pr * 64 + LANES] * eb
        state_update(hb_s, g, xs2, bg, wq2, first, nh)

    @pl.when(c == pl.num_programs(1) - 1)
    def _():
        hfo_ref[...] = hf_s[...]
        hbo_ref[...] = hb_s[...]


def ssd_bidir(xbc, dt_raw, dt_bias, a_neg, d_skip, h0f, h0b, batch):
    t = xbc.shape[0]
    l = t // batch
    q = SSM_CHUNK
    nc = l // q
    w = xbc.shape[1]
    dt_t = jnp.transpose(dt_raw[:, :2 * SSM_HEADS].reshape(batch, l, 2 * SSM_HEADS), (0, 2, 1))
    pad = LANES - 2 * SSM_HEADS
    bias_r = jnp.pad(dt_bias.reshape(1, -1), ((0, 0), (0, pad)))
    a_r = jnp.pad(a_neg.reshape(1, -1), ((0, 0), (0, pad)))
    dsk = jnp.repeat(d_skip, SSM_HEAD_DIM).reshape(1, SSM_D_INNER)
    hshape = (batch, SSM_GROUPS, SSM_STATE, SSM_HPG * SSM_HEAD_DIM)
    fwd = lambda b, c: (b * nc + c, 0)
    bwd = lambda b, c: (b * nc + nc - 1 - c, 0)
    const2 = lambda b, c: (0, 0)
    hmap = lambda b, c: (b, 0, 0, 0)
    hspec = pl.BlockSpec((None,) + hshape[1:], hmap)
    return pl.pallas_call(
        _ssd_kernel,
        grid=(batch, nc),
        in_specs=[pl.BlockSpec((q, w), fwd), pl.BlockSpec((q, LANES), fwd),
                  pl.BlockSpec((None, 2 * SSM_HEADS, q), lambda b, c: (b, 0, c)),
                  pl.BlockSpec((q, w), bwd), pl.BlockSpec((q, LANES), bwd),
                  pl.BlockSpec((None, 2 * SSM_HEADS, q), lambda b, c: (b, 0, nc - 1 - c)),
                  pl.BlockSpec((1, LANES), const2), pl.BlockSpec((1, LANES), const2),
                  pl.BlockSpec((2 * SSM_HEADS, 1), const2), pl.BlockSpec((2 * SSM_HEADS, 1), const2),
                  pl.BlockSpec((1, SSM_D_INNER), const2), hspec, hspec],
        out_specs=[pl.BlockSpec((q, SSM_D_INNER), fwd), pl.BlockSpec((q, SSM_D_INNER), bwd), hspec, hspec],
        out_shape=[jax.ShapeDtypeStruct((t, SSM_D_INNER), F32), jax.ShapeDtypeStruct((t, SSM_D_INNER), F32),
                   jax.ShapeDtypeStruct(hshape, F32), jax.ShapeDtypeStruct(hshape, F32)],
        scratch_shapes=[pltpu.VMEM(hshape[1:], F32), pltpu.VMEM(hshape[1:], F32)],
        compiler_params=_params(("parallel", "arbitrary")),
        name="ssd_bidir",
    )(xbc, dt_raw, dt_t, xbc, dt_raw, dt_t, bias_r, a_r, dt_bias.reshape(-1, 1), a_neg.reshape(-1, 1),
      dsk, h0f, h0b)


def _outproj_kernel(oa_ref, yf_ref, yb_ref, z_ref, ng_ref, w1_ref, w2_ref, x_ref, gate_ref, o_ref):
    y = yf_ref[...] + yb_ref[...]
    u = y * _silu(z_ref[...].astype(F32))
    un = (u * lax.rsqrt(jnp.mean(u * u, axis=-1, keepdims=True) + RMS_EPS)) * ng_ref[...]
    acc = (jnp.dot(oa_ref[...], w1_ref[...], preferred_element_type=F32)
           + jnp.dot(un.astype(BF16), w2_ref[...], preferred_element_type=F32))
    o_ref[...] = x_ref[...] + gate_ref[...] * acc


def out_proj(o_att, yf, yb, z, norm_g, w1, w2, x2d, gate, rows_per_mod):
    t, d = x2d.shape
    tm = _tile(rows_per_mod, 512)
    tpm = rows_per_mod // tm
    nb = gate.shape[0]
    row = lambda i: (i, 0)
    const = lambda i: (0, 0)
    return pl.pallas_call(
        _outproj_kernel,
        grid=(t // tm,),
        in_specs=[pl.BlockSpec((tm, d), row), pl.BlockSpec((tm, d), row), pl.BlockSpec((tm, d), row),
                  pl.BlockSpec((tm, d), row), pl.BlockSpec((1, d), const),
                  pl.BlockSpec(w1.shape, const), pl.BlockSpec(w2.shape, const),
                  pl.BlockSpec((tm, d), row), pl.BlockSpec((None, 1, d), lambda i: (i // tpm, 0, 0))],
        out_specs=pl.BlockSpec((tm, d), row),
        out_shape=jax.ShapeDtypeStruct((t, d), F32),
        compiler_params=_params(("parallel",)),
        name="mixer_out_proj",
    )(o_att, yf, yb, z, norm_g.reshape(1, d), w1, w2, x2d, gate.reshape(nb, 1, d))


def _mm_res_kernel(a_ref, w_ref, x_ref, gate_ref, o_ref):
    acc = jnp.dot(a_ref[...], w_ref[...], preferred_element_type=F32)
    o_ref[...] = x_ref[...] + gate_ref[...] * acc


def matmul_residual(a, w, x2d, gate, rows_per_mod):
    t, d = x2d.shape
    tm = _tile(rows_per_mod, 512)
    tpm = rows_per_mod // tm
    nb = gate.shape[0]
    return pl.pallas_call(
        _mm_res_kernel,
        grid=(t // tm,),
        in_specs=[pl.BlockSpec((tm, a.shape[1]), lambda i: (i, 0)),
                  pl.BlockSpec(w.shape, lambda i: (0, 0)),
                  pl.BlockSpec((tm, d), lambda i: (i, 0)),
                  pl.BlockSpec((None, 1, d), lambda i: (i // tpm, 0, 0))],
        out_specs=pl.BlockSpec((tm, d), lambda i: (i, 0)),
        out_shape=jax.ShapeDtypeStruct((t, d), F32),
        compiler_params=_params(("parallel",)),
        name="matmul_gated_residual",
    )(a, w, x2d, gate.reshape(nb, 1, d))


def _dft_stage1_kernel(m_ref, z_ref, o_ref):
    z = z_ref[...]
    half = z.shape[1] // 2
    zs = jnp.concatenate([z[:, :half], z[:, half:]], axis=0)
    a = jnp.dot(m_ref[...], zs, preferred_element_type=F32)
    n1 = a.shape[0] // 2
    o_ref[0] = a[:n1].astype(o_ref.dtype)
    o_ref[1] = a[n1:].astype(o_ref.dtype)


def _dft_stage2_kernel(m_ref, a_ref, o_ref):
    o_ref[...] = jnp.dot(m_ref[...], a_ref[...], preferred_element_type=F32).astype(o_ref.dtype)


def _dft_tables(l, d):
    n1 = FOURIER_N1
    n2 = l // n1
    gw = d // FOURIER_GROUPS
    two_pi = 2.0 * math.pi
    ch = jnp.arange(gw, dtype=I32)
    ph = (ch[:, None] * ch[None, :]) % gw
    ang = ph.astype(F32) * (two_pi / gw)
    cc, sc = jnp.cos(ang) * gw ** -0.5, jnp.sin(ang) * gw ** -0.5
    eye = jnp.eye(FOURIER_GROUPS, dtype=F32)
    w_cs = jnp.concatenate([jnp.kron(eye, cc), -jnp.kron(eye, sc)], axis=1)
    k1 = jnp.arange(n1, dtype=I32)
    pos = jnp.arange(n2, dtype=I32)[:, None, None] + n2 * jnp.arange(n1, dtype=I32)[None, None, :]
    ang1 = ((k1[None, :, None] * pos) % l).astype(F32) * (two_pi / l)
    gc, gs = jnp.cos(ang1) * l ** -0.5, jnp.sin(ang1) * l ** -0.5
    m1 = jnp.concatenate([jnp.concatenate([gc, gs], axis=2),
                          jnp.concatenate([-gs, gc], axis=2)], axis=1)
    k2 = jnp.arange(n2, dtype=I32)
    ang2 = ((k2[:, None] * k2[None, :]) % n2).astype(F32) * (two_pi / n2)
    m2 = jnp.concatenate([jnp.cos(ang2), jnp.sin(ang2)], axis=1)
    return w_cs.astype(BF16), m1.astype(BF16), m2.astype(BF16)


def fourier_positions(z, m1, m2, batch):
    t, d2 = z.shape
    d = d2 // 2
    l = t // batch
    n1 = FOURIER_N1
    n2 = l // n1
    a = pl.pallas_call(
        _dft_stage1_kernel,
        grid=(batch, n2),
        in_specs=[pl.BlockSpec((None, 2 * n1, 2 * n1), lambda b, j: (j, 0, 0)),
                  pl.BlockSpec((None, n1, d2), lambda b, j: (b, 0, j))],
        out_specs=pl.BlockSpec((None, 2, None, n1, d), lambda b, j: (b, 0, j, 0, 0)),
        out_shape=jax.ShapeDtypeStruct((batch, 2, n2, n1, d), BF16),
        compiler_params=_params(("parallel", "parallel")),
        name="dft_stage1",
    )(m1, z.reshape(batch, n1, n2 * d2))
    cols = n1 * d
    tn = _tile(cols, 8192)
    f = pl.pallas_call(
        _dft_stage2_kernel,
        grid=(batch, cols // tn),
        in_specs=[pl.BlockSpec((n2, 2 * n2), lambda b, j: (0, 0)),
                  pl.BlockSpec((None, 2 * n2, tn), lambda b, j: (b, 0, j))],
        out_specs=pl.BlockSpec((None, n2, tn), lambda b, j: (b, 0, j)),
        out_shape=jax.ShapeDtypeStruct((batch, n2, cols), BF16),
        compiler_params=_params(("parallel", "parallel")),
        name="dft_stage2",
    )(m2, a.reshape(batch, 2 * n2, cols))
    return f.reshape(t, d)


SUB = 8
ROW_CODE_SHIFT = 14


def _to_row_tiles(ref, val):
    rows = val.shape[0]
    for j in range(SUB):
        ref[pl.ds(j, rows, stride=SUB), :] = val[:, j * LANES:(j + 1) * LANES]


def _from_row_tiles(ref, rows):
    return jnp.concatenate([ref[pl.ds(j, rows, stride=SUB), :] for j in range(SUB)], axis=1)


def _row_tile(ref, row):
    if isinstance(row, int):
        return ref.at[pl.ds(row * SUB, SUB), :]
    return ref.at[pl.ds(pl.multiple_of(row * SUB, SUB), SUB), :]


def _route_kernel(x_ref, g_ref, sh_ref, sc_ref, wr_ref, br_ref, h_ref, ti_ref, tg_ref):
    h = _norm_mod(x_ref[...], g_ref[...], sh_ref[...], sc_ref[...])
    _to_row_tiles(h_ref, h)
    logits = jnp.dot(h, wr_ref[...], precision=HIGHEST, preferred_element_type=F32) + br_ref[...]
    lane = lax.broadcasted_iota(I32, logits.shape, 1)
    vals, idxs = [], []
    cur = logits
    for _ in range(TOP_K):
        m = jnp.max(cur, axis=-1, keepdims=True)
        idx = jnp.min(jnp.where(cur == m, lane, LANES), axis=-1, keepdims=True)
        vals.append(m)
        idxs.append(idx)
        cur = jnp.where(lane == idx, -jnp.inf, cur)
    es = [jnp.exp(v - vals[0]) for v in vals]
    tot = es[0] + es[1] + es[2] + es[3]
    ti = jnp.zeros(logits.shape, I32)
    tg = jnp.zeros(logits.shape, F32)
    for k in range(TOP_K):
        ti = jnp.where(lane == k, idxs[k], ti)
        tg = jnp.where(lane == k, es[k] / tot, tg)
    ti_ref[...] = ti
    tg_ref[...] = tg


def route(x2d, g, shift, scale, rows_per_mod, w_router, b_router):
    t, d = x2d.shape
    assert d == SUB * LANES
    nb = shift.shape[0]
    tm = _tile(rows_per_mod, 512)
    tpm = rows_per_mod // tm
    ne = w_router.shape[1]
    wr = jnp.pad(w_router, ((0, 0), (0, LANES - ne)))
    br = jnp.pad(b_router.reshape(1, ne), ((0, 0), (0, LANES - ne)), constant_values=NEG_BIG)
    return pl.pallas_call(
        _route_kernel,
        grid=(t // tm,),
        in_specs=[pl.BlockSpec((tm, d), lambda i: (i, 0)),
                  pl.BlockSpec((1, d), lambda i: (0, 0)),
                  pl.BlockSpec((None, 1, d), lambda i: (i // tpm, 0, 0)),
                  pl.BlockSpec((None, 1, d), lambda i: (i // tpm, 0, 0)),
                  pl.BlockSpec((d, LANES), lambda i: (0, 0)),
                  pl.BlockSpec((1, LANES), lambda i: (0, 0))],
        out_specs=[pl.BlockSpec((tm * SUB, LANES), lambda i: (i, 0)),
                   pl.BlockSpec((tm, LANES), lambda i: (i, 0)),
                   pl.BlockSpec((tm, LANES), lambda i: (i, 0))],
        out_shape=[jax.ShapeDtypeStruct((t * SUB, LANES), F32), jax.ShapeDtypeStruct((t, LANES), I32),
                   jax.ShapeDtypeStruct((t, LANES), F32)],
        compiler_params=_params(("parallel",)),
        name="moe_route",
    )(x2d, g.reshape(1, d), shift.reshape(nb, 1, d), scale.reshape(nb, 1, d), wr, br)


def _rank_kernel(ti_ref, rank_ref, cnt_ref, carry):
    @pl.when(pl.program_id(0) == 0)
    def _():
        carry[...] = jnp.zeros(carry.shape, carry.dtype)

    ti = ti_ref[...]
    tm = ti.shape[0]
    lane = lax.broadcasted_iota(I32, ti.shape, 1)
    earlier = (lax.broadcasted_iota(I32, (tm, tm), 1) < lax.broadcasted_iota(I32, (tm, tm), 0)).astype(BF16)
    base = carry[...]
    rank = jnp.zeros(ti.shape, I32)
    for k in range(TOP_K):
        hit = lane == ti[:, k:k + 1]
        onehot = jnp.where(hit, 1.0, 0.0)
        before = jnp.dot(earlier, onehot.astype(BF16), preferred_element_type=F32)
        rk = jnp.sum(jnp.where(hit, base + before, 0.0), axis=-1, keepdims=True)
        rank = jnp.where(lane == k, rk.astype(I32), rank)
        base = base + jnp.sum(onehot, axis=0, keepdims=True)
    rank_ref[...] = rank
    carry[...] = base
    cnt_ref[...] = base


def _dest_kernel(ti_ref, rank_ref, start_ref, dest_ref):
    ti = ti_ref[...]
    lane = lax.broadcasted_iota(I32, ti.shape, 1)
    dest = jnp.zeros(ti.shape, I32)
    for k in range(TOP_K):
        start = jnp.sum(jnp.where(lane == ti[:, k:k + 1], start_ref[...], 0), axis=-1, keepdims=True)
        dest = jnp.where(lane == k, start + rank_ref[:, k:k + 1], dest)
    dest_ref[...] = dest


def moe_plan(ti, t):
    tb = MOE_ROWS
    tm = _tile(t, 512)
    rank, cnt = pl.pallas_call(
        _rank_kernel,
        grid=(t // tm,),
        in_specs=[pl.BlockSpec((tm, LANES), lambda i: (i, 0))],
        out_specs=[pl.BlockSpec((tm, LANES), lambda i: (i, 0)), pl.BlockSpec((1, LANES), lambda i: (0, 0))],
        out_shape=[jax.ShapeDtypeStruct((t, LANES), I32), jax.ShapeDtypeStruct((1, LANES), F32)],
        scratch_shapes=[pltpu.VMEM((1, LANES), F32)],
        compiler_params=_params(("arbitrary",)),
        name="moe_rank",
    )(ti)
    counts = cnt[0, :N_EXPERTS].astype(I32)
    padded = (counts + tb - 1) // tb * tb
    pend = jnp.cumsum(padded)
    pstart = jnp.pad(pend - padded, (0, LANES - N_EXPERTS)).reshape(1, LANES)
    dest = pl.pallas_call(
        _dest_kernel,
        grid=(t // tm,),
        in_specs=[pl.BlockSpec((tm, LANES), lambda i: (i, 0)), pl.BlockSpec((tm, LANES), lambda i: (i, 0)),
                  pl.BlockSpec((1, LANES), lambda i: (0, 0))],
        out_specs=pl.BlockSpec((tm, LANES), lambda i: (i, 0)),
        out_shape=jax.ShapeDtypeStruct((t, LANES), I32),
        compiler_params=_params(("parallel",)),
        name="moe_dest",
    )(ti, rank, pstart)
    dest = dest[:, :TOP_K].reshape(-1)
    n_blocks = (t * TOP_K + N_EXPERTS * (tb - 1)) // tb + 1
    assert t <= 1 << ROW_CODE_SHIFT and (TOP_K * t + tb) << ROW_CODE_SHIFT < 1 << 31
    asg = jnp.arange(t * TOP_K, dtype=I32)
    real = (((asg % TOP_K) * t + asg // TOP_K) << ROW_CODE_SHIFT) | (asg // TOP_K)
    spill = (TOP_K * t + jnp.arange(n_blocks * tb, dtype=I32) % tb) << ROW_CODE_SHIFT
    codes = spill.at[dest].set(real)
    starts = jnp.arange(n_blocks, dtype=I32) * tb
    blk_e = jnp.minimum(jnp.sum((pend[None, :] <= starts[:, None]).astype(I32), axis=1), N_EXPERTS - 1)
    nact = (pend[-1] // tb).astype(I32).reshape(1)
    return codes, blk_e, nact


def _expert_kernel(tb, n_tok, be_ref, code_ref, nact_ref, h_hbm, wgu_ref, bgu_ref, wdn_ref, bdn_ref, y_hbm,
                   xbuf, ybuf, wgu_s, wdn_perm, wdn_s, gsem, ssem):
    i = pl.program_id(0)
    n_blocks = pl.num_programs(0)
    nact = nact_ref[0]
    de = wdn_ref.shape[0]
    slot = i % 2
    other = 1 - slot

    def start_gather(base, r, dst_slot, priority=0):
        tok = code_ref[base + r] & ((1 << ROW_CODE_SHIFT) - 1)
        pltpu.make_async_copy(_row_tile(h_hbm, tok), _row_tile(xbuf.at[dst_slot], r),
                              gsem.at[dst_slot]).start(priority)

    def start_scatter(base, r, src_slot, priority=0):
        row = code_ref[base + r] >> ROW_CODE_SHIFT
        pltpu.make_async_copy(_row_tile(ybuf.at[src_slot], r), _row_tile(y_hbm, row),
                              ssem.at[src_slot]).start(priority)

    def wait_gather(s):
        pltpu.make_async_copy(h_hbm.at[pl.ds(0, tb * SUB), :], xbuf.at[s], gsem.at[s]).wait()

    def wait_scatter(s):
        pltpu.make_async_copy(ybuf.at[s], y_hbm.at[pl.ds(0, tb * SUB), :], ssem.at[s]).wait()

    @pl.when(i == 0)
    def _():
        ybuf[...] = jnp.zeros(ybuf.shape, ybuf.dtype)

        def first(r, carry):
            start_gather(0, r, 0)
            return carry
        lax.fori_loop(0, tb, first, 0, unroll=8)

    @pl.when((i < nact) & ((i == 0) | (be_ref[i] != be_ref[jnp.maximum(i - 1, 0)])))
    def _():
        wgu_s[...] = wgu_ref[...].astype(BF16)
        half = LANES // 2
        for c in range(wdn_perm.shape[0]):
            for m in range(0, de, LANES):
                wdn_perm[c, pl.ds(m, half, stride=2), :] = wdn_ref[m:m + half, c * LANES:(c + 1) * LANES]
                wdn_perm[c, pl.ds(m + 1, half, stride=2), :] = wdn_ref[m + half:m + LANES, c * LANES:(c + 1) * LANES]
            wdn_s[:, c * LANES:(c + 1) * LANES] = wdn_perm[c].astype(BF16)

    def run_block(slot, other):
        wait_gather(slot)

        @pl.when(i >= 1)
        def _():
            wait_scatter(slot)

        prev_base = jnp.where(i == 0, n_blocks - 1, i - 1) * tb
        next_base = (i + 1) * tb
        x = _from_row_tiles(xbuf.at[slot], tb).astype(BF16)
        even = lax.broadcasted_iota(I32, (tb, LANES), 1) % 2 == 0
        cw = 2 * LANES
        n_chunks = wgu_s.shape[1] // cw
        rows_per_chunk = tb // n_chunks
        parts = []
        for c in range(n_chunks):
            gu = jnp.dot(x, wgu_s[:, c * cw:(c + 1) * cw], preferred_element_type=F32) + bgu_ref[:, c * cw:(c + 1) * cw]
            acts = []
            for c0 in range(0, cw, LANES):
                v = gu[:, c0:c0 + LANES]
                glu = jnp.minimum(v, SWIGLU_LIMIT)
                lin = jnp.clip(v, -SWIGLU_LIMIT, SWIGLU_LIMIT) + 1.0
                acts.append(glu * jax.nn.sigmoid(SWIGLU_ALPHA * glu) * pltpu.roll(lin, LANES - 1, 1))
            parts.append(jnp.where(even, acts[0], pltpu.roll(acts[1], 1, 1)).astype(BF16))
            for r in range(c * rows_per_chunk, (c + 1) * rows_per_chunk):
                start_gather(next_base, r, other, r % 2)
                start_scatter(prev_base, r, other, (r + 1) % 2)
        y = jnp.dot(jnp.concatenate(parts, axis=1), wdn_s[...], preferred_element_type=F32) + bdn_ref[...]
        _to_row_tiles(ybuf.at[slot], y)

    for parity in range(2):
        pl.when((i < nact) & (slot == parity))(functools.partial(run_block, parity, 1 - parity))

    @pl.when(i == nact)
    def _():
        wait_gather(slot)
        wait_scatter(slot)

        def last(r, carry):
            start_scatter((i - 1) * tb, r, other)
            return carry
        lax.fori_loop(0, tb, last, 0, unroll=8)
        wait_scatter(other)


def expert_ffn(h_rt, codes, blk_e, nact, layer, w_gu, b_gu, w_dn, b_dn):
    r = codes.shape[0]
    n_tok = h_rt.shape[0] // SUB
    tb = MOE_ROWS
    _, ne, d, de2 = w_gu.shape
    de = de2 // 2
    b_gu = b_gu.reshape(-1, ne, 1, de2)
    b_dn = b_dn.reshape(-1, ne, 1, d)
    return pl.pallas_call(
        functools.partial(_expert_kernel, tb, n_tok),
        grid_spec=pltpu.PrefetchScalarGridSpec(
            num_scalar_prefetch=3, grid=(r // tb,),
            in_specs=[pl.BlockSpec(memory_space=pl.ANY),
                      pl.BlockSpec((None, None, d, de2), lambda i, be, cd, na: (layer, be[i], 0, 0)),
                      pl.BlockSpec((None, None, 1, de2), lambda i, be, cd, na: (layer, be[i], 0, 0)),
                      pl.BlockSpec((None, None, de, d), lambda i, be, cd, na: (layer, be[i], 0, 0)),
                      pl.BlockSpec((None, None, 1, d), lambda i, be, cd, na: (layer, be[i], 0, 0))],
            out_specs=pl.BlockSpec(memory_space=pl.ANY),
            scratch_shapes=[pltpu.VMEM((2, tb * SUB, LANES), F32), pltpu.VMEM((2, tb * SUB, LANES), F32),
                            pltpu.VMEM((d, de2), BF16), pltpu.VMEM((d // LANES, de, LANES), F32),
                            pltpu.VMEM((de, d), BF16),
                            pltpu.SemaphoreType.DMA((2,)), pltpu.SemaphoreType.DMA((2,))]),
        out_shape=jax.ShapeDtypeStruct(((TOP_K * n_tok + tb) * SUB, LANES), F32),
        compiler_params=_params(("arbitrary",), 58),
        name="moe_experts",
    )(blk_e, codes, nact, h_rt, w_gu, b_gu, w_dn, b_dn)


def _combine_kernel(y0_ref, y1_ref, y2_ref, y3_ref, x_ref, gate_ref, tg_ref, o_ref):
    tg = tg_ref[...]
    tm = x_ref.shape[0]
    ys = [_from_row_tiles(y_ref, tm) * tg[:, k:k + 1] for k, y_ref in enumerate((y0_ref, y1_ref, y2_ref, y3_ref))]
    o_ref[...] = x_ref[...] + gate_ref[...] * ((ys[0] + ys[1]) + (ys[2] + ys[3]))


def moe_combine(y4, tg, x2d, gate, rows_per_mod):
    t, d = x2d.shape
    nb = gate.shape[0]
    tm = _tile(rows_per_mod, MOE_ROWS)
    tpm = rows_per_mod // tm
    nt = t // tm
    planes = [pl.BlockSpec((tm * SUB, LANES), functools.partial(lambda k, i: (k * nt + i, 0), k))
              for k in range(TOP_K)]
    return pl.pallas_call(
        _combine_kernel,
        grid=(nt,),
        in_specs=planes + [
                  pl.BlockSpec((tm, d), lambda i: (i, 0)),
                  pl.BlockSpec((None, 1, d), lambda i: (i // tpm, 0, 0)),
                  pl.BlockSpec((tm, LANES), lambda i: (i, 0))],
        out_specs=pl.BlockSpec((tm, d), lambda i: (i, 0)),
        out_shape=jax.ShapeDtypeStruct((t, d), F32),
        compiler_params=_params(("parallel",)),
        name="moe_combine",
    )(y4, y4, y4, y4, x2d, gate.reshape(nb, 1, d), tg)


def moe_layer(x2d, norm_g, shift, scale, gate, rows_per_mod, w_router, b_router, layer, w_gu, b_gu, w_dn, b_dn):
    t = x2d.shape[0]
    h_rt, ti, tg = route(x2d, norm_g, shift, scale, rows_per_mod, w_router, b_router)
    codes, blk_e, nact = moe_plan(ti, t)
    y4 = expert_ffn(h_rt, codes, blk_e, nact, layer, w_gu, b_gu, w_dn, b_dn)
    return moe_combine(y4, tg, x2d, gate, rows_per_mod)


def _rope_tables(s):
    n_freq = DA_HEAD_DIM // 4
    freqs = ROPE_BASE ** (-jnp.arange(n_freq, dtype=F32) / n_freq)
    pos = jnp.arange(s, dtype=I32)
    ang_r = (pos // GRID_W).astype(F32)[:, None] * freqs
    ang_c = (pos % GRID_W).astype(F32)[:, None] * freqs
    ang = jnp.concatenate([ang_r, ang_r, ang_c, ang_c], axis=-1)
    ang = jnp.concatenate([ang, ang], axis=-1)
    return jnp.cos(ang), jnp.sin(ang)


def _even_layer(x2d, ctx2d, mods, norm_g1, w_in, w_out, q_norm_g, k_norm_g, da_lambda, da_subln_g,
                conv_xb_w, conv_xb_b, conv_c_w, conv_c_b, dt_bias, a_log, d_skip, ssm_norm_g,
                lam_init, batch):
    sx1, cx1, gx1, sc1, cc1 = mods
    t, d = x2d.shape
    s = t // batch
    ctx_len = ctx2d.shape[0] // batch
    qkw = DA_HEADS * 2 * DA_HEAD_DIM
    vw = DA_HEADS * DA_V_DIM
    col_q, col_z = 0, qkw
    col_c = col_z + SSM_D_INNER
    col_k = col_c + SSM_BC_W
    col_v = col_k + qkw
    col_xb = col_v + vw
    col_dt = col_xb + SSM_XB_W
    wb = w_in.astype(BF16)
    w_q, w_k, w_v = wb[:, col_q:col_q + qkw], wb[:, col_k:col_k + qkw], wb[:, col_v:col_v + vw]
    w_z = wb[:, col_z:col_z + SSM_D_INNER]
    w_xbc = jnp.concatenate([wb[:, col_xb:col_xb + SSM_XB_W], wb[:, col_c:col_c + SSM_BC_W]], axis=1)
    w_dt = jnp.pad(wb[:, col_dt:col_dt + 2 * SSM_HEADS], ((0, 0), (0, LANES - 2 * SSM_HEADS)))
    weights = [w_q, w_k, w_v, w_z, w_xbc, w_dt]
    dts = [BF16, BF16, BF16, BF16, BF16, F32]
    q_u, k_u, v_x, z_x, xbc_u, dt_x = nm_panel(x2d, norm_g1, sx1, cx1, s, weights, dts)
    _, kc_u, v_c, _, xbc_cu, dt_c = nm_panel(ctx2d, norm_g1, sc1, cc1, batch * ctx_len, weights, dts)

    lam = (jnp.exp(jnp.sum(da_lambda[0] * da_lambda[1])) -
           jnp.exp(jnp.sum(da_lambda[2] * da_lambda[3]))).astype(F32) + lam_init
    a_neg = -jnp.exp(a_log.astype(F32)).reshape(-1)

    cos_t, sin_t = _rope_tables(s)
    q = qk_prep(q_u, q_norm_g, cos_t, sin_t, s, True, DA_HEAD_DIM ** -0.5 * LOG2E)
    score_bound = (1.01 * DA_HEAD_DIM ** 0.5 * LOG2E) * jnp.max(jnp.abs(q_norm_g)) * jnp.max(jnp.abs(k_norm_g))
    k_x = qk_prep(k_u, k_norm_g, cos_t, sin_t, s, True, 1.0)
    k_c = qk_prep(kc_u, k_norm_g, cos_t, sin_t, ctx_len, False, 1.0)
    k_all = jnp.concatenate([k_x.reshape(batch, s, qkw), k_c.reshape(batch, ctx_len, qkw)], axis=1)
    v_all = jnp.concatenate([v_x.reshape(batch, s, vw), v_c.reshape(batch, ctx_len, vw)], axis=1)
    o_att = diff_attention(q, k_all, v_all, lam, score_bound, da_subln_g, lam_init, batch)

    conv_w = jnp.concatenate([conv_xb_w, conv_c_w], axis=1)
    conv_b = jnp.concatenate([conv_xb_b, conv_c_b], axis=0)
    xbc_x = conv_silu(xbc_u, conv_w, conv_b, s)
    xbc_c = conv_silu(xbc_cu, conv_w, conv_b, ctx_len)
    hshape = (batch, SSM_GROUPS, SSM_STATE, SSM_HPG * SSM_HEAD_DIM)
    zero_h = jnp.zeros(hshape, F32)
    dt_bias_f = dt_bias.reshape(-1).astype(F32)
    _, _, h_f, h_b = ssd_bidir(xbc_c, dt_c, dt_bias_f, a_neg, d_skip, zero_h, zero_h, batch)
    y_f, y_b, _, _ = ssd_bidir(xbc_x, dt_x, dt_bias_f, a_neg, d_skip, h_f, h_b, batch)
    wo = w_out.astype(BF16)
    return out_proj(o_att, y_f, y_b, z_x, ssm_norm_g, wo[:vw], wo[vw:], x2d, gx1, s)


def _odd_layer(x2d, mods, norm_g1, w_fourier, batch):
    sx1, cx1, gx1 = mods
    t, d = x2d.shape
    s = t // batch
    w_cs, m1, m2 = _dft_tables(s, d)
    (z,) = nm_panel(x2d, norm_g1, sx1, cx1, s, [w_cs], [BF16])
    f = fourier_positions(z, m1, m2, batch)
    return matmul_residual(f, w_fourier.astype(BF16), x2d, gx1, s)


def kernel(x, c, ctx, c_ctx, ada_w, ada_b, norm_g, w_in, w_out, q_norm_g, k_norm_g, da_lambda, da_subln_g, conv_xb_w, conv_xb_b, conv_c_w, conv_c_b, dt_bias, a_log, d_skip, ssm_norm_g, w_fourier, w_router, b_router, w_gate_up, b_gate_up, w_down, b_down):
    b, s, d = x.shape
    depth = ada_w.shape[0]
    assert b + 1 <= 8
    cvecs = jnp.concatenate([c, c_ctx[None, :], jnp.zeros((8 - b - 1, d), F32)], axis=0)
    mods = ada_all(cvecs, ada_w, ada_b)
    x2d = x.reshape(b * s, d)
    ctx2d = ctx.reshape(-1, d)
    for i in range(depth):
        m = mods[i].reshape(8, 6, d)
        sx1, cx1, gx1, sx2, cx2, gx2 = [m[:b, j] for j in range(6)]
        if i % 2 == 0:
            e = i // 2
            lam_init = 0.8 - 0.6 * math.exp(-0.3 * i)
            if any(j % 2 == 0 for j in range(i + 1, depth)):
                raise NotImplementedError("context stream output is only needed for depth > 2")
            sc1, cc1 = m[b:b + 1, 0], m[b:b + 1, 1]
            x2d = _even_layer(x2d, ctx2d, (sx1, cx1, gx1, sc1, cc1), norm_g[i, 0], w_in[e], w_out[e],
                              q_norm_g[e], k_norm_g[e], da_lambda[e], da_subln_g[e], conv_xb_w[e],
                              conv_xb_b[e], conv_c_w[e], conv_c_b[e], dt_bias[e], a_log[e], d_skip[e],
                              ssm_norm_g[e], lam_init, b)
        else:
            x2d = _odd_layer(x2d, (sx1, cx1, gx1), norm_g[i, 0], w_fourier[i // 2], b)
        x2d = moe_layer(x2d, norm_g[i, 1], sx2, cx2, gx2, s, w_router[i], b_router[i],
                        i, w_gate_up, b_gate_up, w_down, b_down)
    return x2d.reshape(b, s, d)
```

```python
import functools
import math

import jax
import jax.numpy as jnp
from jax import lax
from jax.experimental import pallas as pl
from jax.experimental.pallas import tpu as pltpu

F32 = jnp.float32
BF16 = jnp.bfloat16
I32 = jnp.int32

RMS_EPS = 1e-6
GRID_W = 64
ROPE_BASE = 10000.0
DA_HEADS = 8
DA_HEAD_DIM = 64
DA_V_DIM = 128
SSM_HEADS = 16
SSM_HEAD_DIM = 64
SSM_GROUPS = 2
SSM_HPG = 8
SSM_STATE = 128
SSM_CONV = 5
SSM_CHUNK = 128
SSM_D_INNER = 1024
SSM_BC_W = 256
SSM_XB_W = 1280
FOURIER_GROUPS = 4
FOURIER_N1 = 128
N_EXPERTS = 32
TOP_K = 4
SWIGLU_LIMIT = 7.0
SWIGLU_ALPHA = 1.702
MOE_ROWS = 256
LANES = 128
NEG_BIG = -1e30
MIB = 1024 * 1024
HIGHEST = lax.Precision.HIGHEST


def _params(sem, vmem_mib=48):
    return pltpu.CompilerParams(dimension_semantics=sem, vmem_limit_bytes=vmem_mib * MIB)


def _tile(n, pref):
    t = min(n, pref)
    while n % t:
        t //= 2
    return t


def _silu(v):
    return v * jax.nn.sigmoid(v)


def _norm_mod(xf, g, shift, scale):
    r = lax.rsqrt(jnp.mean(xf * xf, axis=-1, keepdims=True) + RMS_EPS)
    return ((xf * r) * g) * (1.0 + scale) + shift


def _ada_kernel(a_ref, w_ref, b_ref, o_ref):
    s = _silu(a_ref[...])
    o_ref[...] = jnp.dot(s.astype(BF16), w_ref[...].astype(BF16),
                         preferred_element_type=F32) + b_ref[...]


def ada_all(cvecs, ada_w, ada_b):
    depth, d, n = ada_w.shape
    tn = _tile(n, 1536)
    return pl.pallas_call(
        _ada_kernel,
        grid=(depth, n // tn),
        in_specs=[pl.BlockSpec((8, d), lambda l, j: (0, 0)),
                  pl.BlockSpec((None, d, tn), lambda l, j: (l, 0, j)),
                  pl.BlockSpec((None, 1, tn), lambda l, j: (l, 0, j))],
        out_specs=pl.BlockSpec((None, 8, tn), lambda l, j: (l, 0, j)),
        out_shape=jax.ShapeDtypeStruct((depth, 8, n), F32),
        compiler_params=_params(("parallel", "parallel")),
        name="ada_mod",
    )(cvecs, ada_w, ada_b.reshape(depth, 1, n))


def _nm_panel_kernel(n_w, x_ref, g_ref, sh_ref, sc_ref, *refs):
    w_refs, o_refs = refs[:n_w], refs[n_w:2 * n_w]
    hb = _norm_mod(x_ref[...], g_ref[...], sh_ref[...], sc_ref[...]).astype(BF16)
    for w_ref, o_ref in zip(w_refs, o_refs):
        n = w_ref.shape[1]
        cw = _tile(n, 512)
        for c0 in range(0, n, cw):
            o_ref[:, c0:c0 + cw] = jnp.dot(hb, w_ref[:, c0:c0 + cw],
                                           preferred_element_type=F32).astype(o_ref.dtype)


def nm_panel(x2d, g, shift, scale, rows_per_mod, weights, out_dtypes, tm_pref=256):
    t, d = x2d.shape
    nb = shift.shape[0]
    tm = _tile(rows_per_mod, tm_pref)
    tpm = rows_per_mod // tm
    n_w = len(weights)
    in_specs = [pl.BlockSpec((tm, d), lambda i: (i, 0)),
                pl.BlockSpec((1, d), lambda i: (0, 0)),
                pl.BlockSpec((None, 1, d), lambda i: (i // tpm, 0, 0)),
                pl.BlockSpec((None, 1, d), lambda i: (i // tpm, 0, 0))]
    in_specs += [pl.BlockSpec(w.shape, lambda i: (0, 0)) for w in weights]
    out_specs = [pl.BlockSpec((tm, w.shape[1]), lambda i: (i, 0)) for w in weights]
    out_shape = [jax.ShapeDtypeStruct((t, w.shape[1]), dt) for w, dt in zip(weights, out_dtypes)]
    return pl.pallas_call(
        functools.partial(_nm_panel_kernel, n_w),
        grid=(t // tm,),
        in_specs=in_specs, out_specs=out_specs, out_shape=out_shape,
        compiler_params=_params(("parallel",), 56),
        name="norm_mod_proj",
    )(x2d, g.reshape(1, d), shift.reshape(nb, 1, d), scale.reshape(nb, 1, d), *weights)


def _qk_prep_kernel(rope, out_scale, u_ref, g_ref, seg_ref, cos_ref, sin_ref, o_ref):
    seg = seg_ref[...]
    g = g_ref[...]
    n_heads = u_ref.shape[1] // LANES
    for c in range(n_heads):
        u = u_ref[:, c * LANES:(c + 1) * LANES].astype(F32)
        u2 = u * u
        hi = u2.astype(BF16)
        lo = (u2 - hi.astype(F32)).astype(BF16)
        ss = (jnp.dot(hi, seg, preferred_element_type=F32)
              + jnp.dot(lo, seg, preferred_element_type=F32))
        nrm = (u * lax.rsqrt(ss * (1.0 / DA_HEAD_DIM) + RMS_EPS)) * g
        if rope:
            lane = lax.broadcasted_iota(I32, nrm.shape, 1)
            first = (lane % 32) < 16
            rot = jnp.where(first, -pltpu.roll(nrm, LANES - 16, 1), pltpu.roll(nrm, 16, 1))
            nrm = nrm * cos_ref[...] + rot * sin_ref[...]
        o_ref[:, c * LANES:(c + 1) * LANES] = (nrm * out_scale).astype(o_ref.dtype)


def qk_prep(u, gain, cos_t, sin_t, seq, rope, out_scale):
    t, w = u.shape
    tm = _tile(seq, 256)
    tps = seq // tm
    seg = (jnp.arange(LANES)[:, None] // DA_HEAD_DIM == jnp.arange(LANES)[None, :] // DA_HEAD_DIM)
    return pl.pallas_call(
        functools.partial(_qk_prep_kernel, rope, out_scale),
        grid=(t // tm,),
        in_specs=[pl.BlockSpec((tm, w), lambda i: (i, 0)),
                  pl.BlockSpec((1, LANES), lambda i: (0, 0)),
                  pl.BlockSpec((LANES, LANES), lambda i: (0, 0)),
                  pl.BlockSpec((tm, LANES), lambda i: (i % tps, 0)),
                  pl.BlockSpec((tm, LANES), lambda i: (i % tps, 0))],
        out_specs=pl.BlockSpec((tm, w), lambda i: (i, 0)),
        out_shape=jax.ShapeDtypeStruct((t, w), BF16),
        compiler_params=_params(("parallel",)),
        name="qk_norm_rope",
    )(u, jnp.tile(gain, 2).reshape(1, LANES), seg.astype(BF16), cos_t, sin_t)


LOG2E = math.log2(math.e)
ATTN_SHIFT_LIMIT = 60.0


def _split_components(q):
    lane = lax.broadcasted_iota(I32, q.shape, 1)
    zero = jnp.zeros_like(q)
    return jnp.where(lane < DA_HEAD_DIM, q, zero), jnp.where(lane >= DA_HEAD_DIM, q, zero)


def _attn_finish(n0, l0, n1, l1, lam, g, out_mult, o_ref):
    o = n0 / l0 - lam * (n1 / l1)
    o = (o * lax.rsqrt(jnp.mean(o * o, axis=-1, keepdims=True) + RMS_EPS)) * g
    o_ref[...] = (o * out_mult).astype(o_ref.dtype)


def _attn_online_kernel(tk, out_mult, sc_ref, q_ref, k_ref, v_ref, g_ref, o_ref):
    qs = _split_components(q_ref[...])
    tq = q_ref.shape[0]
    n_kv = k_ref.shape[0] // tk

    def body(i, carry):
        off = pl.multiple_of(i * tk, tk)
        k = k_ref[pl.ds(off, tk), :]
        v = v_ref[pl.ds(off, tk), :]
        new = []
        for c in range(2):
            m, l, acc = carry[c]
            s = lax.dot_general(qs[c], k, (((1,), (1,)), ((), ())), preferred_element_type=F32)
            m_new = jnp.maximum(m, jnp.max(s, axis=-1, keepdims=True))
            alpha = jnp.exp2(m - m_new)
            p = jnp.exp2(s - m_new)
            l_new = alpha * l + jnp.sum(p, axis=-1, keepdims=True)
            acc_new = alpha * acc + jnp.dot(p.astype(BF16), v, preferred_element_type=F32)
            new.append((m_new, l_new, acc_new))
        return tuple(new)

    init = tuple((jnp.full((tq, 1), -jnp.inf, F32), jnp.zeros((tq, 1), F32),
                  jnp.zeros((tq, DA_V_DIM), F32)) for _ in range(2))
    (_, l0, a0), (_, l1, a1) = lax.fori_loop(0, n_kv, body, init)
    _attn_finish(a0, l0, a1, l1, sc_ref[0], g_ref[...], out_mult, o_ref)


def _attn_shift_kernel(tk, out_mult, sc_ref, q_ref, k_ref, v_ref, g_ref, o_ref):
    qs = _split_components(q_ref[...])
    tq = q_ref.shape[0]
    n_kv = k_ref.shape[0] // tk
    shift = sc_ref[1]
    ones = jnp.ones((tk, LANES), BF16)

    sub = 2 * LANES

    def body(i, acc):
        acc = list(acc)
        for j in range(tk // sub):
            off = pl.multiple_of(i * tk + j * sub, sub)
            k = k_ref[pl.ds(off, sub), :]
            va = jnp.concatenate([v_ref[pl.ds(off, sub), :], ones[:sub]], axis=1)
            for c in range(2):
                s = lax.dot_general(qs[c], k, (((1,), (1,)), ((), ())), preferred_element_type=F32)
                acc[c] = acc[c] + jnp.dot(jnp.exp2(s - shift).astype(BF16), va, preferred_element_type=F32)
        return tuple(acc)

    zero = jnp.zeros((tq, 2 * LANES), F32)
    a0, a1 = lax.fori_loop(0, n_kv, body, (zero, zero))
    _attn_finish(a0[:, :LANES], a0[:, LANES:], a1[:, :LANES], a1[:, LANES:], sc_ref[0], g_ref[...], out_mult,
                 o_ref)


def diff_attention(q, k_all, v_all, lam, score_bound, subln_g, lam_init, batch):
    t, w = q.shape
    s = t // batch
    sk = k_all.shape[1]
    tq = _tile(s, 1024)
    tk = 2816 if sk % 2816 == 0 else _tile(sk, 256)
    nq = s // tq
    scalars = jnp.stack([lam, score_bound]).astype(F32)

    def run(body):
        return pl.pallas_call(
            functools.partial(body, tk, 1.0 - lam_init),
            grid=(batch, DA_HEADS, nq),
            in_specs=[pl.BlockSpec(memory_space=pltpu.SMEM),
                      pl.BlockSpec((tq, LANES), lambda b, h, i: (b * nq + i, h)),
                      pl.BlockSpec((None, sk, LANES), lambda b, h, i: (b, 0, h)),
                      pl.BlockSpec((None, sk, LANES), lambda b, h, i: (b, 0, h)),
                      pl.BlockSpec((1, LANES), lambda b, h, i: (0, 0))],
            out_specs=pl.BlockSpec((tq, LANES), lambda b, h, i: (b * nq + i, h)),
            out_shape=jax.ShapeDtypeStruct((t, w), BF16),
            compiler_params=_params(("parallel", "parallel", "parallel")),
            name="diff_attention",
        )(scalars, q, k_all, v_all, subln_g.reshape(1, LANES))

    return lax.cond(score_bound <= ATTN_SHIFT_LIMIT,
                    lambda: run(_attn_shift_kernel), lambda: run(_attn_online_kernel))


HALO = 16


def _conv_kernel(tps, p_ref, c_ref, n_ref, w_ref, b_ref, o_ref, ext_ref):
    t = pl.program_id(0) % tps
    tm = c_ref.shape[0]
    prev = p_ref[...].astype(F32)
    nxt = n_ref[...].astype(F32)
    ext_ref[0:HALO, :] = jnp.where(t == 0, 0.0, prev)
    ext_ref[HALO:HALO + tm, :] = c_ref[...].astype(F32)
    ext_ref[HALO + tm:2 * HALO + tm, :] = jnp.where(t == tps - 1, 0.0, nxt)
    width = c_ref.shape[1]
    cw = _tile(width, 256)
    pad = (SSM_CONV - 1) // 2
    for c0 in range(0, width, cw):
        acc = jnp.broadcast_to(b_ref[:, c0:c0 + cw], (tm, cw))
        for k in range(SSM_CONV):
            acc = acc + w_ref[k:k + 1, c0:c0 + cw] * ext_ref[HALO - pad + k:HALO - pad + k + tm, c0:c0 + cw]
        o_ref[:, c0:c0 + cw] = _silu(acc).astype(o_ref.dtype)


def conv_silu(u, w, b, seq):
    t, c = u.shape
    tm = _tile(seq, 512)
    tps = seq // tm
    hb = tm // HALO
    last = t // HALO - 1
    return pl.pallas_call(
        functools.partial(_conv_kernel, tps),
        grid=(t // tm,),
        in_specs=[pl.BlockSpec((HALO, c), lambda i: (jnp.maximum(i * hb - 1, 0), 0)),
                  pl.BlockSpec((tm, c), lambda i: (i, 0)),
                  pl.BlockSpec((HALO, c), lambda i: (jnp.minimum((i + 1) * hb, last), 0)),
                  pl.BlockSpec((SSM_CONV, c), lambda i: (0, 0)),
                  pl.BlockSpec((1, c), lambda i: (0, 0))],
        out_specs=pl.BlockSpec((tm, c), lambda i: (i, 0)),
        out_shape=jax.ShapeDtypeStruct((t, c), BF16),
        scratch_shapes=[pltpu.VMEM((tm + 2 * HALO, c), F32)],
        compiler_params=_params(("parallel",)),
        name="dwconv_silu",
    )(u, u, u, w, b.reshape(1, c))


def _softplus(v):
    return jnp.maximum(v, 0.0) + jnp.log1p(jnp.exp(-jnp.abs(v)))


def _pair(lane_lo, a, b):
    return jnp.where(lane_lo, a, b)


def _ssd_kernel(xf_ref, dtf_ref, dtTf_ref, xb_ref, dtb_ref, dtTb_ref, bias_r, a_r, bias_c, a_c,
                dsk_ref, h0f_ref, h0b_ref, yf_ref, yb_ref, hfo_ref, hbo_ref, hf_s, hb_s):
    c = pl.program_id(1)
    q = SSM_CHUNK
    nh = SSM_HEADS

    @pl.when(c == 0)
    def _():
        hf_s[...] = h0f_ref[...]
        hb_s[...] = h0b_ref[...]

    li = lax.broadcasted_iota(I32, (q, q), 0)
    si = lax.broadcasted_iota(I32, (q, q), 1)
    lower = li >= si
    upper = li <= si
    tri_l = lower.astype(F32)
    tri_u = upper.astype(F32)
    lane_lo = lax.broadcasted_iota(I32, (q, LANES), 1) < SSM_HEAD_DIM
    lane_lo1 = lax.broadcasted_iota(I32, (1, LANES), 1) < SSM_HEAD_DIM

    def bc(col):
        return jnp.broadcast_to(col, (q, LANES))

    def state_update(h_s, g, xs32, bg, wq, edge, base):
        parts, decs = [], []
        for pr in range(SSM_HPG // 2):
            h0 = g * SSM_HPG + 2 * pr
            wp = _pair(lane_lo, bc(wq[:, base + h0:base + h0 + 1]), bc(wq[:, base + h0 + 1:base + h0 + 2]))
            parts.append((xs32[:, h0 * 64:h0 * 64 + LANES] * wp).astype(BF16))
            decs.append(_pair(lane_lo1, jnp.broadcast_to(edge[:, base + h0:base + h0 + 1], (1, LANES)),
                              jnp.broadcast_to(edge[:, base + h0 + 1:base + h0 + 2], (1, LANES))))
        xw = jnp.concatenate(parts, axis=1)
        dec = jnp.exp(jnp.concatenate(decs, axis=1))
        upd = lax.dot_general(bg, xw, (((0,), (0,)), ((), ())), preferred_element_type=F32)
        h_s[g] = h_s[g] * dec + upd

    xbc = xf_ref[...]
    xs_b = xbc[:, :SSM_D_INNER]
    xs32 = xs_b.astype(F32)
    dt = _softplus(dtf_ref[...] + bias_r[...])
    da = dt * a_r[...]
    acs = jnp.dot(tri_l, da, precision=HIGHEST, preferred_element_type=F32)
    racs = jnp.dot(tri_u, da, precision=HIGHEST, preferred_element_type=F32)
    dt_t = _softplus(dtTf_ref[...] + bias_c[...])
    da_t = dt_t * a_c[...]
    acs_t = jnp.dot(da_t, tri_u, precision=HIGHEST, preferred_element_type=F32)
    racs_t = jnp.dot(da_t, tri_l, precision=HIGHEST, preferred_element_type=F32)
    last = acs[q - 1:q, :]
    wq = jnp.exp(last - acs) * dt
    for g in range(SSM_GROUPS):
        bg = xbc[:, SSM_D_INNER + g * SSM_STATE:SSM_D_INNER + (g + 1) * SSM_STATE]
        cg = xbc[:, SSM_D_INNER + SSM_BC_W + g * SSM_STATE:SSM_D_INNER + SSM_BC_W + (g + 1) * SSM_STATE]
        cb = lax.dot_general(cg, bg, (((1,), (1,)), ((), ())), preferred_element_type=F32)
        yoff = jnp.dot(cg, hf_s[g].astype(BF16), preferred_element_type=F32)
        for pr in range(SSM_HPG // 2):
            h0 = g * SSM_HPG + 2 * pr
            col0 = h0 * SSM_HEAD_DIM
            xp = xs_b[:, col0:col0 + LANES]
            ys = []
            for hh in (h0, h0 + 1):
                mf = jnp.exp(jnp.where(lower, acs[:, hh:hh + 1] - acs_t[hh:hh + 1, :], NEG_BIG)) \
                    * dt_t[hh:hh + 1, :]
                mb = jnp.exp(jnp.where(upper, racs[:, nh + hh:nh + hh + 1] - racs_t[nh + hh:nh + hh + 1, :],
                                       NEG_BIG)) * dt_t[nh + hh:nh + hh + 1, :]
                ys.append(jnp.dot((cb * (mf + mb)).astype(BF16), xp, preferred_element_type=F32))
            ef = jnp.exp(_pair(lane_lo, bc(acs[:, h0:h0 + 1]), bc(acs[:, h0 + 1:h0 + 2])))
            yf_ref[:, col0:col0 + LANES] = (_pair(lane_lo, ys[0], ys[1])
                                            + yoff[:, 2 * pr * 64:2 * pr * 64 + LANES] * ef
                                            + xs32[:, col0:col0 + LANES] * dsk_ref[:, col0:col0 + LANES])
        state_update(hf_s, g, xs32, bg, wq, last, 0)

    xbc2 = xb_ref[...]
    xs2 = xbc2[:, :SSM_D_INNER].astype(F32)
    dt2 = _softplus(dtb_ref[...] + bias_r[...])
    racs2 = jnp.dot(tri_u, dt2 * a_r[...], precision=HIGHEST, preferred_element_type=F32)
    first = racs2[0:1, :]
    wq2 = jnp.exp(first - racs2) * dt2
    for g in range(SSM_GROUPS):
        bg = xbc2[:, SSM_D_INNER + g * SSM_STATE:SSM_D_INNER + (g + 1) * SSM_STATE]
        cg = xbc2[:, SSM_D_INNER + SSM_BC_W + g * SSM_STATE:SSM_D_INNER + SSM_BC_W + (g + 1) * SSM_STATE]
        yoff = jnp.dot(cg, hb_s[g].astype(BF16), preferred_element_type=F32)
        for pr in range(SSM_HPG // 2):
            h0 = g * SSM_HPG + 2 * pr
            col0 = h0 * SSM_HEAD_DIM
            eb = jnp.exp(_pair(lane_lo, bc(racs2[:, nh + h0:nh + h0 + 1]), bc(racs2[:, nh + h0 + 1:nh + h0 + 2])))
            yb_ref[:, col0:col0 + LANES] = yoff[:, 2 * pr * 64:2 * pr * 64 + LANES] * eb
        state_update(hb_s, g, xs2, bg, wq2, first, nh)

    @pl.when(c == pl.num_programs(1) - 1)
    def _():
        hfo_ref[...] = hf_s[...]
        hbo_ref[...] = hb_s[...]


def ssd_bidir(xbc, dt_raw, dt_bias, a_neg, d_skip, h0f, h0b, batch):
    t = xbc.shape[0]
    l = t // batch
    q = SSM_CHUNK
    nc = l // q
    w = xbc.shape[1]
    dt_t = jnp.transpose(dt_raw[:, :2 * SSM_HEADS].reshape(batch, l, 2 * SSM_HEADS), (0, 2, 1))
    pad = LANES - 2 * SSM_HEADS
    bias_r = jnp.pad(dt_bias.reshape(1, -1), ((0, 0), (0, pad)))
    a_r = jnp.pad(a_neg.reshape(1, -1), ((0, 0), (0, pad)))
    dsk = jnp.repeat(d_skip, SSM_HEAD_DIM).reshape(1, SSM_D_INNER)
    hshape = (batch, SSM_GROUPS, SSM_STATE, SSM_HPG * SSM_HEAD_DIM)
    fwd = lambda b, c: (b * nc + c, 0)
    bwd = lambda b, c: (b * nc + nc - 1 - c, 0)
    const2 = lambda b, c: (0, 0)
    hmap = lambda b, c: (b, 0, 0, 0)
    hspec = pl.BlockSpec((None,) + hshape[1:], hmap)
    return pl.pallas_call(
        _ssd_kernel,
        grid=(batch, nc),
        in_specs=[pl.BlockSpec((q, w), fwd), pl.BlockSpec((q, LANES), fwd),
                  pl.BlockSpec((None, 2 * SSM_HEADS, q), lambda b, c: (b, 0, c)),
                  pl.BlockSpec((q, w), bwd), pl.BlockSpec((q, LANES), bwd),
                  pl.BlockSpec((None, 2 * SSM_HEADS, q), lambda b, c: (b, 0, nc - 1 - c)),
                  pl.BlockSpec((1, LANES), const2), pl.BlockSpec((1, LANES), const2),
                  pl.BlockSpec((2 * SSM_HEADS, 1), const2), pl.BlockSpec((2 * SSM_HEADS, 1), const2),
                  pl.BlockSpec((1, SSM_D_INNER), const2), hspec, hspec],
        out_specs=[pl.BlockSpec((q, SSM_D_INNER), fwd), pl.BlockSpec((q, SSM_D_INNER), bwd), hspec, hspec],
        out_shape=[jax.ShapeDtypeStruct((t, SSM_D_INNER), F32), jax.ShapeDtypeStruct((t, SSM_D_INNER), F32),
                   jax.ShapeDtypeStruct(hshape, F32), jax.ShapeDtypeStruct(hshape, F32)],
        scratch_shapes=[pltpu.VMEM(hshape[1:], F32), pltpu.VMEM(hshape[1:], F32)],
        compiler_params=_params(("parallel", "arbitrary")),
        name="ssd_bidir",
    )(xbc, dt_raw, dt_t, xbc, dt_raw, dt_t, bias_r, a_r, dt_bias.reshape(-1, 1), a_neg.reshape(-1, 1),
      dsk, h0f, h0b)


def _outproj_kernel(oa_ref, yf_ref, yb_ref, z_ref, ng_ref, w1_ref, w2_ref, x_ref, gate_ref, o_ref):
    y = yf_ref[...] + yb_ref[...]
    u = y * _silu(z_ref[...].astype(F32))
    un = (u * lax.rsqrt(jnp.mean(u * u, axis=-1, keepdims=True) + RMS_EPS)) * ng_ref[...]
    acc = (jnp.dot(oa_ref[...], w1_ref[...], preferred_element_type=F32)
           + jnp.dot(un.astype(BF16), w2_ref[...], preferred_element_type=F32))
    o_ref[...] = x_ref[...] + gate_ref[...] * acc


def out_proj(o_att, yf, yb, z, norm_g, w1, w2, x2d, gate, rows_per_mod):
    t, d = x2d.shape
    tm = _tile(rows_per_mod, 512)
    tpm = rows_per_mod // tm
    nb = gate.shape[0]
    row = lambda i: (i, 0)
    const = lambda i: (0, 0)
    return pl.pallas_call(
        _outproj_kernel,
        grid=(t // tm,),
        in_specs=[pl.BlockSpec((tm, d), row), pl.BlockSpec((tm, d), row), pl.BlockSpec((tm, d), row),
                  pl.BlockSpec((tm, d), row), pl.BlockSpec((1, d), const),
                  pl.BlockSpec(w1.shape, const), pl.BlockSpec(w2.shape, const),
                  pl.BlockSpec((tm, d), row), pl.BlockSpec((None, 1, d), lambda i: (i // tpm, 0, 0))],
        out_specs=pl.BlockSpec((tm, d), row),
        out_shape=jax.ShapeDtypeStruct((t, d), F32),
        compiler_params=_params(("parallel",)),
        name="mixer_out_proj",
    )(o_att, yf, yb, z, norm_g.reshape(1, d), w1, w2, x2d, gate.reshape(nb, 1, d))


def _mm_res_kernel(a_ref, w_ref, x_ref, gate_ref, o_ref):
    acc = jnp.dot(a_ref[...], w_ref[...], preferred_element_type=F32)
    o_ref[...] = x_ref[...] + gate_ref[...] * acc


def matmul_residual(a, w, x2d, gate, rows_per_mod):
    t, d = x2d.shape
    tm = _tile(rows_per_mod, 512)
    tpm = rows_per_mod // tm
    nb = gate.shape[0]
    return pl.pallas_call(
        _mm_res_kernel,
        grid=(t // tm,),
        in_specs=[pl.BlockSpec((tm, a.shape[1]), lambda i: (i, 0)),
                  pl.BlockSpec(w.shape, lambda i: (0, 0)),
                  pl.BlockSpec((tm, d), lambda i: (i, 0)),
                  pl.BlockSpec((None, 1, d), lambda i: (i // tpm, 0, 0))],
        out_specs=pl.BlockSpec((tm, d), lambda i: (i, 0)),
        out_shape=jax.ShapeDtypeStruct((t, d), F32),
        compiler_params=_params(("parallel",)),
        name="matmul_gated_residual",
    )(a, w, x2d, gate.reshape(nb, 1, d))


def _dft_stage1_kernel(m_ref, z_ref, o_ref):
    z = z_ref[...]
    half = z.shape[1] // 2
    zs = jnp.concatenate([z[:, :half], z[:, half:]], axis=0)
    a = jnp.dot(m_ref[...], zs, preferred_element_type=F32)
    n1 = a.shape[0] // 2
    o_ref[0] = a[:n1].astype(o_ref.dtype)
    o_ref[1] = a[n1:].astype(o_ref.dtype)


def _dft_stage2_kernel(m_ref, a_ref, o_ref):
    o_ref[...] = jnp.dot(m_ref[...], a_ref[...], preferred_element_type=F32).astype(o_ref.dtype)


def _dft_tables(l, d):
    n1 = FOURIER_N1
    n2 = l // n1
    gw = d // FOURIER_GROUPS
    two_pi = 2.0 * math.pi
    ch = jnp.arange(gw, dtype=I32)
    ph = (ch[:, None] * ch[None, :]) % gw
    ang = ph.astype(F32) * (two_pi / gw)
    cc, sc = jnp.cos(ang) * gw ** -0.5, jnp.sin(ang) * gw ** -0.5
    eye = jnp.eye(FOURIER_GROUPS, dtype=F32)
    w_cs = jnp.concatenate([jnp.kron(eye, cc), -jnp.kron(eye, sc)], axis=1)
    k1 = jnp.arange(n1, dtype=I32)
    pos = jnp.arange(n2, dtype=I32)[:, None, None] + n2 * jnp.arange(n1, dtype=I32)[None, None, :]
    ang1 = ((k1[None, :, None] * pos) % l).astype(F32) * (two_pi / l)
    gc, gs = jnp.cos(ang1) * l ** -0.5, jnp.sin(ang1) * l ** -0.5
    m1 = jnp.concatenate([jnp.concatenate([gc, gs], axis=2),
                          jnp.concatenate([-gs, gc], axis=2)], axis=1)
    k2 = jnp.arange(n2, dtype=I32)
    ang2 = ((k2[:, None] * k2[None, :]) % n2).astype(F32) * (two_pi / n2)
    m2 = jnp.concatenate([jnp.cos(ang2), jnp.sin(ang2)], axis=1)
    return w_cs.astype(BF16), m1.astype(BF16), m2.astype(BF16)


def fourier_positions(z, m1, m2, batch):
    t, d2 = z.shape
    d = d2 // 2
    l = t // batch
    n1 = FOURIER_N1
    n2 = l // n1
    a = pl.pallas_call(
        _dft_stage1_kernel,
        grid=(batch, n2),
        in_specs=[pl.BlockSpec((None, 2 * n1, 2 * n1), lambda b, j: (j, 0, 0)),
                  pl.BlockSpec((None, n1, d2), lambda b, j: (b, 0, j))],
        out_specs=pl.BlockSpec((None, 2, None, n1, d), lambda b, j: (b, 0, j, 0, 0)),
        out_shape=jax.ShapeDtypeStruct((batch, 2, n2, n1, d), BF16),
        compiler_params=_params(("parallel", "parallel")),
        name="dft_stage1",
    )(m1, z.reshape(batch, n1, n2 * d2))
    cols = n1 * d
    tn = _tile(cols, 8192)
    f = pl.pallas_call(
        _dft_stage2_kernel,
        grid=(batch, cols // tn),
        in_specs=[pl.BlockSpec((n2, 2 * n2), lambda b, j: (0, 0)),
                  pl.BlockSpec((None, 2 * n2, tn), lambda b, j: (b, 0, j))],
        out_specs=pl.BlockSpec((None, n2, tn), lambda b, j: (b, 0, j)),
        out_shape=jax.ShapeDtypeStruct((batch, n2, cols), BF16),
        compiler_params=_params(("parallel", "parallel")),
        name="dft_stage2",
    )(m2, a.reshape(batch, 2 * n2, cols))
    return f.reshape(t, d)


SUB = 8
ROW_CODE_SHIFT = 14


def _to_row_tiles(ref, val):
    rows = val.shape[0]
    for j in range(SUB):
        ref[pl.ds(j, rows, stride=SUB), :] = val[:, j * LANES:(j + 1) * LANES]


def _from_row_tiles(ref, rows):
    return jnp.concatenate([ref[pl.ds(j, rows, stride=SUB), :] for j in range(SUB)], axis=1)


def _row_tile(ref, row):
    if isinstance(row, int):
        return ref.at[pl.ds(row * SUB, SUB), :]
    return ref.at[pl.ds(pl.multiple_of(row * SUB, SUB), SUB), :]


def _route_kernel(x_ref, g_ref, sh_ref, sc_ref, wr_ref, br_ref, h_ref, ti_ref, tg_ref):
    h = _norm_mod(x_ref[...], g_ref[...], sh_ref[...], sc_ref[...])
    _to_row_tiles(h_ref, h)
    logits = jnp.dot(h, wr_ref[...], precision=HIGHEST, preferred_element_type=F32) + br_ref[...]
    lane = lax.broadcasted_iota(I32, logits.shape, 1)
    vals, idxs = [], []
    cur = logits
    for _ in range(TOP_K):
        m = jnp.max(cur, axis=-1, keepdims=True)
        idx = jnp.min(jnp.where(cur == m, lane, LANES), axis=-1, keepdims=True)
        vals.append(m)
        idxs.append(idx)
        cur = jnp.where(lane == idx, -jnp.inf, cur)
    es = [jnp.exp(v - vals[0]) for v in vals]
    tot = es[0] + es[1] + es[2] + es[3]
    ti = jnp.zeros(logits.shape, I32)
    tg = jnp.zeros(logits.shape, F32)
    for k in range(TOP_K):
        ti = jnp.where(lane == k, idxs[k], ti)
        tg = jnp.where(lane == k, es[k] / tot, tg)
    ti_ref[...] = ti
    tg_ref[...] = tg


def route(x2d, g, shift, scale, rows_per_mod, w_router, b_router):
    t, d = x2d.shape
    assert d == SUB * LANES
    nb = shift.shape[0]
    tm = _tile(rows_per_mod, 512)
    tpm = rows_per_mod // tm
    ne = w_router.shape[1]
    wr = jnp.pad(w_router, ((0, 0), (0, LANES - ne)))
    br = jnp.pad(b_router.reshape(1, ne), ((0, 0), (0, LANES - ne)), constant_values=NEG_BIG)
    return pl.pallas_call(
        _route_kernel,
        grid=(t // tm,),
        in_specs=[pl.BlockSpec((tm, d), lambda i: (i, 0)),
                  pl.BlockSpec((1, d), lambda i: (0, 0)),
                  pl.BlockSpec((None, 1, d), lambda i: (i // tpm, 0, 0)),
                  pl.BlockSpec((None, 1, d), lambda i: (i // tpm, 0, 0)),
                  pl.BlockSpec((d, LANES), lambda i: (0, 0)),
                  pl.BlockSpec((1, LANES), lambda i: (0, 0))],
        out_specs=[pl.BlockSpec((tm * SUB, LANES), lambda i: (i, 0)),
                   pl.BlockSpec((tm, LANES), lambda i: (i, 0)),
                   pl.BlockSpec((tm, LANES), lambda i: (i, 0))],
        out_shape=[jax.ShapeDtypeStruct((t * SUB, LANES), F32), jax.ShapeDtypeStruct((t, LANES), I32),
                   jax.ShapeDtypeStruct((t, LANES), F32)],
        compiler_params=_params(("parallel",)),
        name="moe_route",
    )(x2d, g.reshape(1, d), shift.reshape(nb, 1, d), scale.reshape(nb, 1, d), wr, br)


def _rank_kernel(ti_ref, rank_ref, cnt_ref, carry):
    @pl.when(pl.program_id(0) == 0)
    def _():
        carry[...] = jnp.zeros(carry.shape, carry.dtype)

    ti = ti_ref[...]
    tm = ti.shape[0]
    lane = lax.broadcasted_iota(I32, ti.shape, 1)
    earlier = (lax.broadcasted_iota(I32, (tm, tm), 1) < lax.broadcasted_iota(I32, (tm, tm), 0)).astype(BF16)
    base = carry[...]
    rank = jnp.zeros(ti.shape, I32)
    for k in range(TOP_K):
        hit = lane == ti[:, k:k + 1]
        onehot = jnp.where(hit, 1.0, 0.0)
        before = jnp.dot(earlier, onehot.astype(BF16), preferred_element_type=F32)
        rk = jnp.sum(jnp.where(hit, base + before, 0.0), axis=-1, keepdims=True)
        rank = jnp.where(lane == k, rk.astype(I32), rank)
        base = base + jnp.sum(onehot, axis=0, keepdims=True)
    rank_ref[...] = rank
    carry[...] = base
    cnt_ref[...] = base


def _dest_kernel(ti_ref, rank_ref, start_ref, dest_ref):
    ti = ti_ref[...]
    lane = lax.broadcasted_iota(I32, ti.shape, 1)
    dest = jnp.zeros(ti.shape, I32)
    for k in range(TOP_K):
        start = jnp.sum(jnp.where(lane == ti[:, k:k + 1], start_ref[...], 0), axis=-1, keepdims=True)
        dest = jnp.where(lane == k, start + rank_ref[:, k:k + 1], dest)
    dest_ref[...] = dest


def moe_plan(ti, t):
    tb = MOE_ROWS
    tm = _tile(t, 512)
    rank, cnt = pl.pallas_call(
        _rank_kernel,
        grid=(t // tm,),
        in_specs=[pl.BlockSpec((tm, LANES), lambda i: (i, 0))],
        out_specs=[pl.BlockSpec((tm, LANES), lambda i: (i, 0)), pl.BlockSpec((1, LANES), lambda i: (0, 0))],
        out_shape=[jax.ShapeDtypeStruct((t, LANES), I32), jax.ShapeDtypeStruct((1, LANES), F32)],
        scratch_shapes=[pltpu.VMEM((1, LANES), F32)],
        compiler_params=_params(("arbitrary",)),
        name="moe_rank",
    )(ti)
    counts = cnt[0, :N_EXPERTS].astype(I32)
    padded = (counts + tb - 1) // tb * tb
    pend = jnp.cumsum(padded)
    pstart = jnp.pad(pend - padded, (0, LANES - N_EXPERTS)).reshape(1, LANES)
    dest = pl.pallas_call(
        _dest_kernel,
        grid=(t // tm,),
        in_specs=[pl.BlockSpec((tm, LANES), lambda i: (i, 0)), pl.BlockSpec((tm, LANES), lambda i: (i, 0)),
                  pl.BlockSpec((1, LANES), lambda i: (0, 0))],
        out_specs=pl.BlockSpec((tm, LANES), lambda i: (i, 0)),
        out_shape=jax.ShapeDtypeStruct((t, LANES), I32),
        compiler_params=_params(("parallel",)),
        name="moe_dest",
    )(ti, rank, pstart)
    dest = dest[:, :TOP_K].reshape(-1)
    n_blocks = (t * TOP_K + N_EXPERTS * (tb - 1)) // tb + 1
    assert t <= 1 << ROW_CODE_SHIFT and (TOP_K * t + tb) << ROW_CODE_SHIFT < 1 << 31
    asg = jnp.arange(t * TOP_K, dtype=I32)
    real = (((asg % TOP_K) * t + asg // TOP_K) << ROW_CODE_SHIFT) | (asg // TOP_K)
    spill = (TOP_K * t + jnp.arange(n_blocks * tb, dtype=I32) % tb) << ROW_CODE_SHIFT
    codes = spill.at[dest].set(real)
    starts = jnp.arange(n_blocks, dtype=I32) * tb
    blk_e = jnp.minimum(jnp.sum((pend[None, :] <= starts[:, None]).astype(I32), axis=1), N_EXPERTS - 1)
    nact = (pend[-1] // tb).astype(I32).reshape(1)
    return codes, blk_e, nact


def _expert_kernel(tb, n_tok, be_ref, code_ref, nact_ref, h_hbm, wgu_ref, bgu_ref, wdn_ref, bdn_ref, y_hbm,
                   xbuf, ybuf, wgu_s, wdn_perm, wdn_s, gsem, ssem):
    i = pl.program_id(0)
    n_blocks = pl.num_programs(0)
    nact = nact_ref[0]
    de = wdn_ref.shape[0]
    slot = i % 2
    other = 1 - slot

    def start_gather(base, r, dst_slot, priority=0):
        tok = code_ref[base + r] & ((1 << ROW_CODE_SHIFT) - 1)
        pltpu.make_async_copy(_row_tile(h_hbm, tok), _row_tile(xbuf.at[dst_slot], r),
                              gsem.at[dst_slot]).start(priority)

    def start_scatter(base, r, src_slot, priority=0):
        row = code_ref[base + r] >> ROW_CODE_SHIFT
        pltpu.make_async_copy(_row_tile(ybuf.at[src_slot], r), _row_tile(y_hbm, row),
                              ssem.at[src_slot]).start(priority)

    def wait_gather(s):
        pltpu.make_async_copy(h_hbm.at[pl.ds(0, tb * SUB), :], xbuf.at[s], gsem.at[s]).wait()

    def wait_scatter(s):
        pltpu.make_async_copy(ybuf.at[s], y_hbm.at[pl.ds(0, tb * SUB), :], ssem.at[s]).wait()

    @pl.when(i == 0)
    def _():
        ybuf[...] = jnp.zeros(ybuf.shape, ybuf.dtype)

        def first(r, carry):
            start_gather(0, r, 0)
            return carry
        lax.fori_loop(0, tb, first, 0, unroll=8)

    @pl.when((i < nact) & ((i == 0) | (be_ref[i] != be_ref[jnp.maximum(i - 1, 0)])))
    def _():
        wgu_s[...] = wgu_ref[...].astype(BF16)
        half = LANES // 2
        for c in range(wdn_perm.shape[0]):
            for m in range(0, de, LANES):
                wdn_perm[c, pl.ds(m, half, stride=2), :] = wdn_ref[m:m + half, c * LANES:(c + 1) * LANES]
                wdn_perm[c, pl.ds(m + 1, half, stride=2), :] = wdn_ref[m + half:m + LANES, c * LANES:(c + 1) * LANES]
            wdn_s[:, c * LANES:(c + 1) * LANES] = wdn_perm[c].astype(BF16)

    def start_copies(slot, other):
        wait_gather(slot)

        @pl.when(i >= 1)
        def _():
            wait_scatter(slot)

        prev_base = jnp.where(i == 0, n_blocks - 1, i - 1) * tb
        next_base = (i + 1) * tb
        for r in range(tb):
            start_gather(next_base, r, other, r % 2)
            start_scatter(prev_base, r, other, (r + 1) % 2)

    def run_block(slot):
        x = _from_row_tiles(xbuf.at[slot], tb).astype(BF16)
        even = lax.broadcasted_iota(I32, (tb, LANES), 1) % 2 == 0
        cw = 2 * LANES
        n_chunks = wgu_s.shape[1] // cw
        parts = []
        for c in range(n_chunks):
            gu = jnp.dot(x, wgu_s[:, c * cw:(c + 1) * cw], preferred_element_type=F32) + bgu_ref[:, c * cw:(c + 1) * cw]
            acts = []
            for c0 in range(0, cw, LANES):
                v = gu[:, c0:c0 + LANES]
                glu = jnp.minimum(v, SWIGLU_LIMIT)
                lin = jnp.clip(v, -SWIGLU_LIMIT, SWIGLU_LIMIT) + 1.0
                acts.append(glu * jax.nn.sigmoid(SWIGLU_ALPHA * glu) * pltpu.roll(lin, LANES - 1, 1))
            parts.append(jnp.where(even, acts[0], pltpu.roll(acts[1], 1, 1)).astype(BF16))
        y = jnp.dot(jnp.concatenate(parts, axis=1), wdn_s[...], preferred_element_type=F32) + bdn_ref[...]
        _to_row_tiles(ybuf.at[slot], y)

    for parity in range(2):
        pl.when((i < nact) & (slot == parity))(functools.partial(start_copies, parity, 1 - parity))
    for parity in range(2):
        pl.when((i < nact) & (slot == parity))(functools.partial(run_block, parity))

    @pl.when(i == nact)
    def _():
        wait_gather(slot)
        wait_scatter(slot)

        def last(r, carry):
            start_scatter((i - 1) * tb, r, other)
            return carry
        lax.fori_loop(0, tb, last, 0, unroll=8)
        wait_scatter(other)


def expert_ffn(h_rt, codes, blk_e, nact, layer, w_gu, b_gu, w_dn, b_dn):
    r = codes.shape[0]
    n_tok = h_rt.shape[0] // SUB
    tb = MOE_ROWS
    _, ne, d, de2 = w_gu.shape
    de = de2 // 2
    b_gu = b_gu.reshape(-1, ne, 1, de2)
    b_dn = b_dn.reshape(-1, ne, 1, d)
    return pl.pallas_call(
        functools.partial(_expert_kernel, tb, n_tok),
        grid_spec=pltpu.PrefetchScalarGridSpec(
            num_scalar_prefetch=3, grid=(r // tb,),
            in_specs=[pl.BlockSpec(memory_space=pl.ANY),
                      pl.BlockSpec((None, None, d, de2), lambda i, be, cd, na: (layer, be[i], 0, 0)),
                      pl.BlockSpec((None, None, 1, de2), lambda i, be, cd, na: (layer, be[i], 0, 0)),
                      pl.BlockSpec((None, None, de, d), lambda i, be, cd, na: (layer, be[i], 0, 0)),
                      pl.BlockSpec((None, None, 1, d), lambda i, be, cd, na: (layer, be[i], 0, 0))],
            out_specs=pl.BlockSpec(memory_space=pl.ANY),
            scratch_shapes=[pltpu.VMEM((2, tb * SUB, LANES), F32), pltpu.VMEM((2, tb * SUB, LANES), F32),
                            pltpu.VMEM((d, de2), BF16), pltpu.VMEM((d // LANES, de, LANES), F32),
                            pltpu.VMEM((de, d), BF16),
                            pltpu.SemaphoreType.DMA((2,)), pltpu.SemaphoreType.DMA((2,))]),
        out_shape=jax.ShapeDtypeStruct(((TOP_K * n_tok + tb) * SUB, LANES), F32),
        compiler_params=_params(("arbitrary",), 58),
        name="moe_experts",
    )(blk_e, codes, nact, h_rt, w_gu, b_gu, w_dn, b_dn)


def _combine_kernel(y0_ref, y1_ref, y2_ref, y3_ref, x_ref, gate_ref, tg_ref, o_ref):
    tg = tg_ref[...]
    tm = x_ref.shape[0]
    ys = [_from_row_tiles(y_ref, tm) * tg[:, k:k + 1] for k, y_ref in enumerate((y0_ref, y1_ref, y2_ref, y3_ref))]
    o_ref[...] = x_ref[...] + gate_ref[...] * ((ys[0] + ys[1]) + (ys[2] + ys[3]))


def moe_combine(y4, tg, x2d, gate, rows_per_mod):
    t, d = x2d.shape
    nb = gate.shape[0]
    tm = _tile(rows_per_mod, MOE_ROWS)
    tpm = rows_per_mod // tm
    nt = t // tm
    planes = [pl.BlockSpec((tm * SUB, LANES), functools.partial(lambda k, i: (k * nt + i, 0), k))
              for k in range(TOP_K)]
    return pl.pallas_call(
        _combine_kernel,
        grid=(nt,),
        in_specs=planes + [
                  pl.BlockSpec((tm, d), lambda i: (i, 0)),
                  pl.BlockSpec((None, 1, d), lambda i: (i // tpm, 0, 0)),
                  pl.BlockSpec((tm, LANES), lambda i: (i, 0))],
        out_specs=pl.BlockSpec((tm, d), lambda i: (i, 0)),
        out_shape=jax.ShapeDtypeStruct((t, d), F32),
        compiler_params=_params(("parallel",)),
        name="moe_combine",
    )(y4, y4, y4, y4, x2d, gate.reshape(nb, 1, d), tg)


def moe_layer(x2d, norm_g, shift, scale, gate, rows_per_mod, w_router, b_router, layer, w_gu, b_gu, w_dn, b_dn):
    t = x2d.shape[0]
    h_rt, ti, tg = route(x2d, norm_g, shift, scale, rows_per_mod, w_router, b_router)
    codes, blk_e, nact = moe_plan(ti, t)
    y4 = expert_ffn(h_rt, codes, blk_e, nact, layer, w_gu, b_gu, w_dn, b_dn)
    return moe_combine(y4, tg, x2d, gate, rows_per_mod)


def _rope_tables(s):
    n_freq = DA_HEAD_DIM // 4
    freqs = ROPE_BASE ** (-jnp.arange(n_freq, dtype=F32) / n_freq)
    pos = jnp.arange(s, dtype=I32)
    ang_r = (pos // GRID_W).astype(F32)[:, None] * freqs
    ang_c = (pos % GRID_W).astype(F32)[:, None] * freqs
    ang = jnp.concatenate([ang_r, ang_r, ang_c, ang_c], axis=-1)
    ang = jnp.concatenate([ang, ang], axis=-1)
    return jnp.cos(ang), jnp.sin(ang)


def _even_layer(x2d, ctx2d, mods, norm_g1, w_in, w_out, q_norm_g, k_norm_g, da_lambda, da_subln_g,
                conv_xb_w, conv_xb_b, conv_c_w, conv_c_b, dt_bias, a_log, d_skip, ssm_norm_g,
                lam_init, batch):
    sx1, cx1, gx1, sc1, cc1 = mods
    t, d = x2d.shape
    s = t // batch
    ctx_len = ctx2d.shape[0] // batch
    qkw = DA_HEADS * 2 * DA_HEAD_DIM
    vw = DA_HEADS * DA_V_DIM
    col_q, col_z = 0, qkw
    col_c = col_z + SSM_D_INNER
    col_k = col_c + SSM_BC_W
    col_v = col_k + qkw
    col_xb = col_v + vw
    col_dt = col_xb + SSM_XB_W
    wb = w_in.astype(BF16)
    w_q, w_k, w_v = wb[:, col_q:col_q + qkw], wb[:, col_k:col_k + qkw], wb[:, col_v:col_v + vw]
    w_z = wb[:, col_z:col_z + SSM_D_INNER]
    w_xbc = jnp.concatenate([wb[:, col_xb:col_xb + SSM_XB_W], wb[:, col_c:col_c + SSM_BC_W]], axis=1)
    w_dt = jnp.pad(wb[:, col_dt:col_dt + 2 * SSM_HEADS], ((0, 0), (0, LANES - 2 * SSM_HEADS)))
    weights = [w_q, w_k, w_v, w_z, w_xbc, w_dt]
    dts = [BF16, BF16, BF16, BF16, BF16, F32]
    q_u, k_u, v_x, z_x, xbc_u, dt_x = nm_panel(x2d, norm_g1, sx1, cx1, s, weights, dts)
    _, kc_u, v_c, _, xbc_cu, dt_c = nm_panel(ctx2d, norm_g1, sc1, cc1, batch * ctx_len, weights, dts)

    lam = (jnp.exp(jnp.sum(da_lambda[0] * da_lambda[1])) -
           jnp.exp(jnp.sum(da_lambda[2] * da_lambda[3]))).astype(F32) + lam_init
    a_neg = -jnp.exp(a_log.astype(F32)).reshape(-1)

    cos_t, sin_t = _rope_tables(s)
    q = qk_prep(q_u, q_norm_g, cos_t, sin_t, s, True, DA_HEAD_DIM ** -0.5 * LOG2E)
    score_bound = (1.01 * DA_HEAD_DIM ** 0.5 * LOG2E) * jnp.max(jnp.abs(q_norm_g)) * jnp.max(jnp.abs(k_norm_g))
    k_x = qk_prep(k_u, k_norm_g, cos_t, sin_t, s, True, 1.0)
    k_c = qk_prep(kc_u, k_norm_g, cos_t, sin_t, ctx_len, False, 1.0)
    k_all = jnp.concatenate([k_x.reshape(batch, s, qkw), k_c.reshape(batch, ctx_len, qkw)], axis=1)
    v_all = jnp.concatenate([v_x.reshape(batch, s, vw), v_c.reshape(batch, ctx_len, vw)], axis=1)
    o_att = diff_attention(q, k_all, v_all, lam, score_bound, da_subln_g, lam_init, batch)

    conv_w = jnp.concatenate([conv_xb_w, conv_c_w], axis=1)
    conv_b = jnp.concatenate([conv_xb_b, conv_c_b], axis=0)
    xbc_x = conv_silu(xbc_u, conv_w, conv_b, s)
    xbc_c = conv_silu(xbc_cu, conv_w, conv_b, ctx_len)
    hshape = (batch, SSM_GROUPS, SSM_STATE, SSM_HPG * SSM_HEAD_DIM)
    zero_h = jnp.zeros(hshape, F32)
    dt_bias_f = dt_bias.reshape(-1).astype(F32)
    _, _, h_f, h_b = ssd_bidir(xbc_c, dt_c, dt_bias_f, a_neg, d_skip, zero_h, zero_h, batch)
    y_f, y_b, _, _ = ssd_bidir(xbc_x, dt_x, dt_bias_f, a_neg, d_skip, h_f, h_b, batch)
    wo = w_out.astype(BF16)
    return out_proj(o_att, y_f, y_b, z_x, ssm_norm_g, wo[:vw], wo[vw:], x2d, gx1, s)


def _odd_layer(x2d, mods, norm_g1, w_fourier, batch):
    sx1, cx1, gx1 = mods
    t, d = x2d.shape
    s = t // batch
    w_cs, m1, m2 = _dft_tables(s, d)
    (z,) = nm_panel(x2d, norm_g1, sx1, cx1, s, [w_cs], [BF16])
    f = fourier_positions(z, m1, m2, batch)
    return matmul_residual(f, w_fourier.astype(BF16), x2d, gx1, s)


def kernel(x, c, ctx, c_ctx, ada_w, ada_b, norm_g, w_in, w_out, q_norm_g, k_norm_g, da_lambda, da_subln_g, conv_xb_w, conv_xb_b, conv_c_w, conv_c_b, dt_bias, a_log, d_skip, ssm_norm_g, w_fourier, w_router, b_router, w_gate_up, b_gate_up, w_down, b_down):
    b, s, d = x.shape
    depth = ada_w.shape[0]
    assert b + 1 <= 8
    cvecs = jnp.concatenate([c, c_ctx[None, :], jnp.zeros((8 - b - 1, d), F32)], axis=0)
    mods = ada_all(cvecs, ada_w, ada_b)
    x2d = x.reshape(b * s, d)
    ctx2d = ctx.reshape(-1, d)
    for i in range(depth):
        m = mods[i].reshape(8, 6, d)
        sx1, cx1, gx1, sx2, cx2, gx2 = [m[:b, j] for j in range(6)]
        if i % 2 == 0:
            e = i // 2
            lam_init = 0.8 - 0.6 * math.exp(-0.3 * i)
            if any(j % 2 == 0 for j in range(i + 1, depth)):
                raise NotImplementedError("context stream output is only needed for depth > 2")
            sc1, cc1 = m[b:b + 1, 0], m[b:b + 1, 1]
            x2d = _even_layer(x2d, ctx2d, (sx1, cx1, gx1, sc1, cc1), norm_g[i, 0], w_in[e], w_out[e],
                              q_norm_g[e], k_norm_g[e], da_lambda[e], da_subln_g[e], conv_xb_w[e],
                              conv_xb_b[e], conv_c_w[e], conv_c_b[e], dt_bias[e], a_log[e], d_skip[e],
                              ssm_norm_g[e], lam_init, b)
        else:
            x2d = _odd_layer(x2d, (sx1, cx1, gx1), norm_g[i, 0], w_fourier[i // 2], b)
        x2d = moe_layer(x2d, norm_g[i, 1], sx2, cx2, gx2, s, w_router[i], b_router[i],
                        i, w_gate_up, b_gate_up, w_down, b_down)
    return x2d.reshape(b, s, d)
```

```python
import functools
import math

import jax
import jax.numpy as jnp
from jax import lax
from jax.experimental import pallas as pl
from jax.experimental.pallas import tpu as pltpu

F32 = jnp.float32
BF16 = jnp.bfloat16
I32 = jnp.int32

RMS_EPS = 1e-6
GRID_W = 64
ROPE_BASE = 10000.0
DA_HEADS = 8
DA_HEAD_DIM = 64
DA_V_DIM = 128
SSM_HEADS = 16
SSM_HEAD_DIM = 64
SSM_GROUPS = 2
SSM_HPG = 8
SSM_STATE = 128
SSM_CONV = 5
SSM_CHUNK = 128
SSM_D_INNER = 1024
SSM_BC_W = 256
SSM_XB_W = 1280
FOURIER_GROUPS = 4
FOURIER_N1 = 128
N_EXPERTS = 32
TOP_K = 4
SWIGLU_LIMIT = 7.0
SWIGLU_ALPHA = 1.702
MOE_ROWS = 256
LANES = 128
NEG_BIG = -1e30
MIB = 1024 * 1024
HIGHEST = lax.Precision.HIGHEST


def _params(sem, vmem_mib=48):
    return pltpu.CompilerParams(dimension_semantics=sem, vmem_limit_bytes=vmem_mib * MIB)


def _tile(n, pref):
    t = min(n, pref)
    while n % t:
        t //= 2
    return t


def _silu(v):
    return v * jax.nn.sigmoid(v)


def _norm_mod(xf, g, shift, scale):
    r = lax.rsqrt(jnp.mean(xf * xf, axis=-1, keepdims=True) + RMS_EPS)
    return ((xf * r) * g) * (1.0 + scale) + shift


def _ada_kernel(a_ref, w_ref, b_ref, o_ref):
    s = _silu(a_ref[...])
    o_ref[...] = jnp.dot(s.astype(BF16), w_ref[...].astype(BF16),
                         preferred_element_type=F32) + b_ref[...]


def ada_all(cvecs, ada_w, ada_b):
    depth, d, n = ada_w.shape
    tn = _tile(n, 1536)
    return pl.pallas_call(
        _ada_kernel,
        grid=(depth, n // tn),
        in_specs=[pl.BlockSpec((8, d), lambda l, j: (0, 0)),
                  pl.BlockSpec((None, d, tn), lambda l, j: (l, 0, j)),
                  pl.BlockSpec((None, 1, tn), lambda l, j: (l, 0, j))],
        out_specs=pl.BlockSpec((None, 8, tn), lambda l, j: (l, 0, j)),
        out_shape=jax.ShapeDtypeStruct((depth, 8, n), F32),
        compiler_params=_params(("parallel", "parallel")),
        name="ada_mod",
    )(cvecs, ada_w, ada_b.reshape(depth, 1, n))


def _nm_panel_kernel(n_w, x_ref, g_ref, sh_ref, sc_ref, *refs):
    w_refs, o_refs = refs[:n_w], refs[n_w:2 * n_w]
    hb = _norm_mod(x_ref[...], g_ref[...], sh_ref[...], sc_ref[...]).astype(BF16)
    for w_ref, o_ref in zip(w_refs, o_refs):
        n = w_ref.shape[1]
        cw = _tile(n, 512)
        for c0 in range(0, n, cw):
            o_ref[:, c0:c0 + cw] = jnp.dot(hb, w_ref[:, c0:c0 + cw],
                                           preferred_element_type=F32).astype(o_ref.dtype)


def nm_panel(x2d, g, shift, scale, rows_per_mod, weights, out_dtypes, tm_pref=256):
    t, d = x2d.shape
    nb = shift.shape[0]
    tm = _tile(rows_per_mod, tm_pref)
    tpm = rows_per_mod // tm
    n_w = len(weights)
    in_specs = [pl.BlockSpec((tm, d), lambda i: (i, 0)),
                pl.BlockSpec((1, d), lambda i: (0, 0)),
                pl.BlockSpec((None, 1, d), lambda i: (i // tpm, 0, 0)),
                pl.BlockSpec((None, 1, d), lambda i: (i // tpm, 0, 0))]
    in_specs += [pl.BlockSpec(w.shape, lambda i: (0, 0)) for w in weights]
    out_specs = [pl.BlockSpec((tm, w.shape[1]), lambda i: (i, 0)) for w in weights]
    out_shape = [jax.ShapeDtypeStruct((t, w.shape[1]), dt) for w, dt in zip(weights, out_dtypes)]
    return pl.pallas_call(
        functools.partial(_nm_panel_kernel, n_w),
        grid=(t // tm,),
        in_specs=in_specs, out_specs=out_specs, out_shape=out_shape,
        compiler_params=_params(("parallel",), 56),
        name="norm_mod_proj",
    )(x2d, g.reshape(1, d), shift.reshape(nb, 1, d), scale.reshape(nb, 1, d), *weights)


def _qk_prep_kernel(rope, out_scale, u_ref, g_ref, seg_ref, cos_ref, sin_ref, o_ref):
    seg = seg_ref[...]
    g = g_ref[...]
    n_heads = u_ref.shape[1] // LANES
    for c in range(n_heads):
        u = u_ref[:, c * LANES:(c + 1) * LANES].astype(F32)
        u2 = u * u
        hi = u2.astype(BF16)
        lo = (u2 - hi.astype(F32)).astype(BF16)
        ss = (jnp.dot(hi, seg, preferred_element_type=F32)
              + jnp.dot(lo, seg, preferred_element_type=F32))
        nrm = (u * lax.rsqrt(ss * (1.0 / DA_HEAD_DIM) + RMS_EPS)) * g
        if rope:
            lane = lax.broadcasted_iota(I32, nrm.shape, 1)
            first = (lane % 32) < 16
            rot = jnp.where(first, -pltpu.roll(nrm, LANES - 16, 1), pltpu.roll(nrm, 16, 1))
            nrm = nrm * cos_ref[...] + rot * sin_ref[...]
        o_ref[:, c * LANES:(c + 1) * LANES] = (nrm * out_scale).astype(o_ref.dtype)


def qk_prep(u, gain, cos_t, sin_t, seq, rope, out_scale):
    t, w = u.shape
    tm = _tile(seq, 256)
    tps = seq // tm
    seg = (jnp.arange(LANES)[:, None] // DA_HEAD_DIM == jnp.arange(LANES)[None, :] // DA_HEAD_DIM)
    return pl.pallas_call(
        functools.partial(_qk_prep_kernel, rope, out_scale),
        grid=(t // tm,),
        in_specs=[pl.BlockSpec((tm, w), lambda i: (i, 0)),
                  pl.BlockSpec((1, LANES), lambda i: (0, 0)),
                  pl.BlockSpec((LANES, LANES), lambda i: (0, 0)),
                  pl.BlockSpec((tm, LANES), lambda i: (i % tps, 0)),
                  pl.BlockSpec((tm, LANES), lambda i: (i % tps, 0))],
        out_specs=pl.BlockSpec((tm, w), lambda i: (i, 0)),
        out_shape=jax.ShapeDtypeStruct((t, w), BF16),
        compiler_params=_params(("parallel",)),
        name="qk_norm_rope",
    )(u, jnp.tile(gain, 2).reshape(1, LANES), seg.astype(BF16), cos_t, sin_t)


LOG2E = math.log2(math.e)
ATTN_SHIFT_LIMIT = 60.0


def _split_components(q):
    lane = lax.broadcasted_iota(I32, q.shape, 1)
    zero = jnp.zeros_like(q)
    return jnp.where(lane < DA_HEAD_DIM, q, zero), jnp.where(lane >= DA_HEAD_DIM, q, zero)


def _attn_finish(n0, l0, n1, l1, lam, g, out_mult, o_ref):
    o = n0 / l0 - lam * (n1 / l1)
    o = (o * lax.rsqrt(jnp.mean(o * o, axis=-1, keepdims=True) + RMS_EPS)) * g
    o_ref[...] = (o * out_mult).astype(o_ref.dtype)


def _attn_online_kernel(tk, out_mult, sc_ref, q_ref, k_ref, v_ref, g_ref, o_ref):
    qs = _split_components(q_ref[...])
    tq = q_ref.shape[0]
    n_kv = k_ref.shape[0] // tk

    def body(i, carry):
        off = pl.multiple_of(i * tk, tk)
        k = k_ref[pl.ds(off, tk), :]
        v = v_ref[pl.ds(off, tk), :]
        new = []
        for c in range(2):
            m, l, acc = carry[c]
            s = lax.dot_general(qs[c], k, (((1,), (1,)), ((), ())), preferred_element_type=F32)
            m_new = jnp.maximum(m, jnp.max(s, axis=-1, keepdims=True))
            alpha = jnp.exp2(m - m_new)
            p = jnp.exp2(s - m_new)
            l_new = alpha * l + jnp.sum(p, axis=-1, keepdims=True)
            acc_new = alpha * acc + jnp.dot(p.astype(BF16), v, preferred_element_type=F32)
            new.append((m_new, l_new, acc_new))
        return tuple(new)

    init = tuple((jnp.full((tq, 1), -jnp.inf, F32), jnp.zeros((tq, 1), F32),
                  jnp.zeros((tq, DA_V_DIM), F32)) for _ in range(2))
    (_, l0, a0), (_, l1, a1) = lax.fori_loop(0, n_kv, body, init)
    _attn_finish(a0, l0, a1, l1, sc_ref[0], g_ref[...], out_mult, o_ref)


def _attn_shift_kernel(tk, out_mult, sc_ref, q_ref, k_ref, v_ref, g_ref, o_ref):
    qs = _split_components(q_ref[...])
    tq = q_ref.shape[0]
    n_kv = k_ref.shape[0] // tk
    shift = sc_ref[1]
    ones = jnp.ones((tk, LANES), BF16)

    sub = 2 * LANES

    def body(i, acc):
        acc = list(acc)
        for j in range(tk // sub):
            off = pl.multiple_of(i * tk + j * sub, sub)
            k = k_ref[pl.ds(off, sub), :]
            va = jnp.concatenate([v_ref[pl.ds(off, sub), :], ones[:sub]], axis=1)
            for c in range(2):
                s = lax.dot_general(qs[c], k, (((1,), (1,)), ((), ())), preferred_element_type=F32)
                acc[c] = acc[c] + jnp.dot(jnp.exp2(s - shift).astype(BF16), va, preferred_element_type=F32)
        return tuple(acc)

    zero = jnp.zeros((tq, 2 * LANES), F32)
    a0, a1 = lax.fori_loop(0, n_kv, body, (zero, zero))
    _attn_finish(a0[:, :LANES], a0[:, LANES:], a1[:, :LANES], a1[:, LANES:], sc_ref[0], g_ref[...], out_mult,
                 o_ref)


def diff_attention(q, k_all, v_all, lam, score_bound, subln_g, lam_init, batch):
    t, w = q.shape
    s = t // batch
    sk = k_all.shape[1]
    tq = _tile(s, 1024)
    tk = 2816 if sk % 2816 == 0 else _tile(sk, 256)
    nq = s // tq
    scalars = jnp.stack([lam, score_bound]).astype(F32)

    def run(body, tk):
        return pl.pallas_call(
            functools.partial(body, tk, 1.0 - lam_init),
            grid=(batch, DA_HEADS, nq),
            in_specs=[pl.BlockSpec(memory_space=pltpu.SMEM),
                      pl.BlockSpec((tq, LANES), lambda b, h, i: (b * nq + i, h)),
                      pl.BlockSpec((None, sk, LANES), lambda b, h, i: (b, 0, h)),
                      pl.BlockSpec((None, sk, LANES), lambda b, h, i: (b, 0, h)),
                      pl.BlockSpec((1, LANES), lambda b, h, i: (0, 0))],
            out_specs=pl.BlockSpec((tq, LANES), lambda b, h, i: (b * nq + i, h)),
            out_shape=jax.ShapeDtypeStruct((t, w), BF16),
            compiler_params=_params(("parallel", "parallel", "parallel")),
            name="diff_attention",
        )(scalars, q, k_all, v_all, subln_g.reshape(1, LANES))

    return lax.cond(score_bound <= ATTN_SHIFT_LIMIT,
                    lambda: run(_attn_shift_kernel, sk), lambda: run(_attn_online_kernel, tk))


HALO = 16


def _conv_kernel(tps, p_ref, c_ref, n_ref, w_ref, b_ref, o_ref, ext_ref):
    t = pl.program_id(0) % tps
    tm = c_ref.shape[0]
    prev = p_ref[...].astype(F32)
    nxt = n_ref[...].astype(F32)
    ext_ref[0:HALO, :] = jnp.where(t == 0, 0.0, prev)
    ext_ref[HALO:HALO + tm, :] = c_ref[...].astype(F32)
    ext_ref[HALO + tm:2 * HALO + tm, :] = jnp.where(t == tps - 1, 0.0, nxt)
    width = c_ref.shape[1]
    cw = _tile(width, 256)
    pad = (SSM_CONV - 1) // 2
    for c0 in range(0, width, cw):
        acc = jnp.broadcast_to(b_ref[:, c0:c0 + cw], (tm, cw))
        for k in range(SSM_CONV):
            acc = acc + w_ref[k:k + 1, c0:c0 + cw] * ext_ref[HALO - pad + k:HALO - pad + k + tm, c0:c0 + cw]
        o_ref[:, c0:c0 + cw] = _silu(acc).astype(o_ref.dtype)


def conv_silu(u, w, b, seq):
    t, c = u.shape
    tm = _tile(seq, 512)
    tps = seq // tm
    hb = tm // HALO
    last = t // HALO - 1
    return pl.pallas_call(
        functools.partial(_conv_kernel, tps),
        grid=(t // tm,),
        in_specs=[pl.BlockSpec((HALO, c), lambda i: (jnp.maximum(i * hb - 1, 0), 0)),
                  pl.BlockSpec((tm, c), lambda i: (i, 0)),
                  pl.BlockSpec((HALO, c), lambda i: (jnp.minimum((i + 1) * hb, last), 0)),
                  pl.BlockSpec((SSM_CONV, c), lambda i: (0, 0)),
                  pl.BlockSpec((1, c), lambda i: (0, 0))],
        out_specs=pl.BlockSpec((tm, c), lambda i: (i, 0)),
        out_shape=jax.ShapeDtypeStruct((t, c), BF16),
        scratch_shapes=[pltpu.VMEM((tm + 2 * HALO, c), F32)],
        compiler_params=_params(("parallel",)),
        name="dwconv_silu",
    )(u, u, u, w, b.reshape(1, c))


def _softplus(v):
    return jnp.maximum(v, 0.0) + jnp.log1p(jnp.exp(-jnp.abs(v)))


def _pair(lane_lo, a, b):
    return jnp.where(lane_lo, a, b)


def _ssd_kernel(xf_ref, dtf_ref, dtTf_ref, xb_ref, dtb_ref, dtTb_ref, bias_r, a_r, bias_c, a_c,
                dsk_ref, h0f_ref, h0b_ref, yf_ref, yb_ref, hfo_ref, hbo_ref, hf_s, hb_s):
    c = pl.program_id(1)
    q = SSM_CHUNK
    nh = SSM_HEADS

    @pl.when(c == 0)
    def _():
        hf_s[...] = h0f_ref[...]
        hb_s[...] = h0b_ref[...]

    li = lax.broadcasted_iota(I32, (q, q), 0)
    si = lax.broadcasted_iota(I32, (q, q), 1)
    lower = li >= si
    upper = li <= si
    tri_l = lower.astype(F32)
    tri_u = upper.astype(F32)
    lane_lo = lax.broadcasted_iota(I32, (q, LANES), 1) < SSM_HEAD_DIM
    lane_lo1 = lax.broadcasted_iota(I32, (1, LANES), 1) < SSM_HEAD_DIM

    def bc(col):
        return jnp.broadcast_to(col, (q, LANES))

    def state_update(h_s, g, xs32, bg, wq, edge, base):
        parts, decs = [], []
        for pr in range(SSM_HPG // 2):
            h0 = g * SSM_HPG + 2 * pr
            wp = _pair(lane_lo, bc(wq[:, base + h0:base + h0 + 1]), bc(wq[:, base + h0 + 1:base + h0 + 2]))
            parts.append((xs32[:, h0 * 64:h0 * 64 + LANES] * wp).astype(BF16))
            decs.append(_pair(lane_lo1, jnp.broadcast_to(edge[:, base + h0:base + h0 + 1], (1, LANES)),
                              jnp.broadcast_to(edge[:, base + h0 + 1:base + h0 + 2], (1, LANES))))
        xw = jnp.concatenate(parts, axis=1)
        dec = jnp.exp(jnp.concatenate(decs, axis=1))
        upd = lax.dot_general(bg, xw, (((0,), (0,)), ((), ())), preferred_element_type=F32)
        h_s[g] = h_s[g] * dec + upd

    xbc = xf_ref[...]
    xs_b = xbc[:, :SSM_D_INNER]
    xs32 = xs_b.astype(F32)
    dt = _softplus(dtf_ref[...] + bias_r[...])
    da = dt * a_r[...]
    acs = jnp.dot(tri_l, da, precision=HIGHEST, preferred_element_type=F32)
    racs = jnp.dot(tri_u, da, precision=HIGHEST, preferred_element_type=F32)
    dt_t = _softplus(dtTf_ref[...] + bias_c[...])
    da_t = dt_t * a_c[...]
    acs_t = jnp.dot(da_t, tri_u, precision=HIGHEST, preferred_element_type=F32)
    racs_t = jnp.dot(da_t, tri_l, precision=HIGHEST, preferred_element_type=F32)
    last = acs[q - 1:q, :]
    wq = jnp.exp(last - acs) * dt
    for g in range(SSM_GROUPS):
        bg = xbc[:, SSM_D_INNER + g * SSM_STATE:SSM_D_INNER + (g + 1) * SSM_STATE]
        cg = xbc[:, SSM_D_INNER + SSM_BC_W + g * SSM_STATE:SSM_D_INNER + SSM_BC_W + (g + 1) * SSM_STATE]
        cb = lax.dot_general(cg, bg, (((1,), (1,)), ((), ())), preferred_element_type=F32)
        yoff = jnp.dot(cg, hf_s[g].astype(BF16), preferred_element_type=F32)
        for pr in range(SSM_HPG // 2):
            h0 = g * SSM_HPG + 2 * pr
            col0 = h0 * SSM_HEAD_DIM
            xp = xs_b[:, col0:col0 + LANES]
            ys = []
            for hh in (h0, h0 + 1):
                mf = jnp.exp(jnp.where(lower, acs[:, hh:hh + 1] - acs_t[hh:hh + 1, :], NEG_BIG)) \
                    * dt_t[hh:hh + 1, :]
                mb = jnp.exp(jnp.where(upper, racs[:, nh + hh:nh + hh + 1] - racs_t[nh + hh:nh + hh + 1, :],
                                       NEG_BIG)) * dt_t[nh + hh:nh + hh + 1, :]
                ys.append(jnp.dot((cb * (mf + mb)).astype(BF16), xp, preferred_element_type=F32))
            ef = jnp.exp(_pair(lane_lo, bc(acs[:, h0:h0 + 1]), bc(acs[:, h0 + 1:h0 + 2])))
            yf_ref[:, col0:col0 + LANES] = (_pair(lane_lo, ys[0], ys[1])
                                            + yoff[:, 2 * pr * 64:2 * pr * 64 + LANES] * ef
                                            + xs32[:, col0:col0 + LANES] * dsk_ref[:, col0:col0 + LANES])
        state_update(hf_s, g, xs32, bg, wq, last, 0)

    xbc2 = xb_ref[...]
    xs2 = xbc2[:, :SSM_D_INNER].astype(F32)
    dt2 = _softplus(dtb_ref[...] + bias_r[...])
    racs2 = jnp.dot(tri_u, dt2 * a_r[...], precision=HIGHEST, preferred_element_type=F32)
    first = racs2[0:1, :]
    wq2 = jnp.exp(first - racs2) * dt2
    for g in range(SSM_GROUPS):
        bg = xbc2[:, SSM_D_INNER + g * SSM_STATE:SSM_D_INNER + (g + 1) * SSM_STATE]
        cg = xbc2[:, SSM_D_INNER + SSM_BC_W + g * SSM_STATE:SSM_D_INNER + SSM_BC_W + (g + 1) * SSM_STATE]
        yoff = jnp.dot(cg, hb_s[g].astype(BF16), preferred_element_type=F32)
        for pr in range(SSM_HPG // 2):
            h0 = g * SSM_HPG + 2 * pr
            col0 = h0 * SSM_HEAD_DIM
            eb = jnp.exp(_pair(lane_lo, bc(racs2[:, nh + h0:nh + h0 + 1]), bc(racs2[:, nh + h0 + 1:nh + h0 + 2])))
            yb_ref[:, col0:col0 + LANES] = yoff[:, 2 * pr * 64:2 * pr * 64 + LANES] * eb
        state_update(hb_s, g, xs2, bg, wq2, first, nh)

    @pl.when(c == pl.num_programs(1) - 1)
    def _():
        hfo_ref[...] = hf_s[...]
        hbo_ref[...] = hb_s[...]


def ssd_bidir(xbc, dt_raw, dt_bias, a_neg, d_skip, h0f, h0b, batch):
    t = xbc.shape[0]
    l = t // batch
    q = SSM_CHUNK
    nc = l // q
    w = xbc.shape[1]
    dt_t = jnp.transpose(dt_raw[:, :2 * SSM_HEADS].reshape(batch, l, 2 * SSM_HEADS), (0, 2, 1))
    pad = LANES - 2 * SSM_HEADS
    bias_r = jnp.pad(dt_bias.reshape(1, -1), ((0, 0), (0, pad)))
    a_r = jnp.pad(a_neg.reshape(1, -1), ((0, 0), (0, pad)))
    dsk = jnp.repeat(d_skip, SSM_HEAD_DIM).reshape(1, SSM_D_INNER)
    hshape = (batch, SSM_GROUPS, SSM_STATE, SSM_HPG * SSM_HEAD_DIM)
    fwd = lambda b, c: (b * nc + c, 0)
    bwd = lambda b, c: (b * nc + nc - 1 - c, 0)
    const2 = lambda b, c: (0, 0)
    hmap = lambda b, c: (b, 0, 0, 0)
    hspec = pl.BlockSpec((None,) + hshape[1:], hmap)
    return pl.pallas_call(
        _ssd_kernel,
        grid=(batch, nc),
        in_specs=[pl.BlockSpec((q, w), fwd), pl.BlockSpec((q, LANES), fwd),
                  pl.BlockSpec((None, 2 * SSM_HEADS, q), lambda b, c: (b, 0, c)),
                  pl.BlockSpec((q, w), bwd), pl.BlockSpec((q, LANES), bwd),
                  pl.BlockSpec((None, 2 * SSM_HEADS, q), lambda b, c: (b, 0, nc - 1 - c)),
                  pl.BlockSpec((1, LANES), const2), pl.BlockSpec((1, LANES), const2),
                  pl.BlockSpec((2 * SSM_HEADS, 1), const2), pl.BlockSpec((2 * SSM_HEADS, 1), const2),
                  pl.BlockSpec((1, SSM_D_INNER), const2), hspec, hspec],
        out_specs=[pl.BlockSpec((q, SSM_D_INNER), fwd), pl.BlockSpec((q, SSM_D_INNER), bwd), hspec, hspec],
        out_shape=[jax.ShapeDtypeStruct((t, SSM_D_INNER), F32), jax.ShapeDtypeStruct((t, SSM_D_INNER), F32),
                   jax.ShapeDtypeStruct(hshape, F32), jax.ShapeDtypeStruct(hshape, F32)],
        scratch_shapes=[pltpu.VMEM(hshape[1:], F32), pltpu.VMEM(hshape[1:], F32)],
        compiler_params=_params(("parallel", "arbitrary")),
        name="ssd_bidir",
    )(xbc, dt_raw, dt_t, xbc, dt_raw, dt_t, bias_r, a_r, dt_bias.reshape(-1, 1), a_neg.reshape(-1, 1),
      dsk, h0f, h0b)


def _outproj_kernel(oa_ref, yf_ref, yb_ref, z_ref, ng_ref, w1_ref, w2_ref, x_ref, gate_ref, o_ref):
    y = yf_ref[...] + yb_ref[...]
    u = y * _silu(z_ref[...].astype(F32))
    un = (u * lax.rsqrt(jnp.mean(u * u, axis=-1, keepdims=True) + RMS_EPS)) * ng_ref[...]
    acc = (jnp.dot(oa_ref[...], w1_ref[...], preferred_element_type=F32)
           + jnp.dot(un.astype(BF16), w2_ref[...], preferred_element_type=F32))
    o_ref[...] = x_ref[...] + gate_ref[...] * acc


def out_proj(o_att, yf, yb, z, norm_g, w1, w2, x2d, gate, rows_per_mod):
    t, d = x2d.shape
    tm = _tile(rows_per_mod, 512)
    tpm = rows_per_mod // tm
    nb = gate.shape[0]
    row = lambda i: (i, 0)
    const = lambda i: (0, 0)
    return pl.pallas_call(
        _outproj_kernel,
        grid=(t // tm,),
        in_specs=[pl.BlockSpec((tm, d), row), pl.BlockSpec((tm, d), row), pl.BlockSpec((tm, d), row),
                  pl.BlockSpec((tm, d), row), pl.BlockSpec((1, d), const),
                  pl.BlockSpec(w1.shape, const), pl.BlockSpec(w2.shape, const),
                  pl.BlockSpec((tm, d), row), pl.BlockSpec((None, 1, d), lambda i: (i // tpm, 0, 0))],
        out_specs=pl.BlockSpec((tm, d), row),
        out_shape=jax.ShapeDtypeStruct((t, d), F32),
        compiler_params=_params(("parallel",)),
        name="mixer_out_proj",
    )(o_att, yf, yb, z, norm_g.reshape(1, d), w1, w2, x2d, gate.reshape(nb, 1, d))


def _mm_res_kernel(a_ref, w_ref, x_ref, gate_ref, o_ref):
    acc = jnp.dot(a_ref[...], w_ref[...], preferred_element_type=F32)
    o_ref[...] = x_ref[...] + gate_ref[...] * acc


def matmul_residual(a, w, x2d, gate, rows_per_mod):
    t, d = x2d.shape
    tm = _tile(rows_per_mod, 512)
    tpm = rows_per_mod // tm
    nb = gate.shape[0]
    return pl.pallas_call(
        _mm_res_kernel,
        grid=(t // tm,),
        in_specs=[pl.BlockSpec((tm, a.shape[1]), lambda i: (i, 0)),
                  pl.BlockSpec(w.shape, lambda i: (0, 0)),
                  pl.BlockSpec((tm, d), lambda i: (i, 0)),
                  pl.BlockSpec((None, 1, d), lambda i: (i // tpm, 0, 0))],
        out_specs=pl.BlockSpec((tm, d), lambda i: (i, 0)),
        out_shape=jax.ShapeDtypeStruct((t, d), F32),
        compiler_params=_params(("parallel",)),
        name="matmul_gated_residual",
    )(a, w, x2d, gate.reshape(nb, 1, d))


def _dft_stage1_kernel(m_ref, z_ref, o_ref):
    z = z_ref[...]
    half = z.shape[1] // 2
    zs = jnp.concatenate([z[:, :half], z[:, half:]], axis=0)
    a = jnp.dot(m_ref[...], zs, preferred_element_type=F32)
    n1 = a.shape[0] // 2
    o_ref[0] = a[:n1].astype(o_ref.dtype)
    o_ref[1] = a[n1:].astype(o_ref.dtype)


def _dft_stage2_kernel(m_ref, a_ref, o_ref):
    o_ref[...] = jnp.dot(m_ref[...], a_ref[...], preferred_element_type=F32).astype(o_ref.dtype)


def _dft_tables(l, d):
    n1 = FOURIER_N1
    n2 = l // n1
    gw = d // FOURIER_GROUPS
    two_pi = 2.0 * math.pi
    ch = jnp.arange(gw, dtype=I32)
    ph = (ch[:, None] * ch[None, :]) % gw
    ang = ph.astype(F32) * (two_pi / gw)
    cc, sc = jnp.cos(ang) * gw ** -0.5, jnp.sin(ang) * gw ** -0.5
    eye = jnp.eye(FOURIER_GROUPS, dtype=F32)
    w_cs = jnp.concatenate([jnp.kron(eye, cc), -jnp.kron(eye, sc)], axis=1)
    k1 = jnp.arange(n1, dtype=I32)
    pos = jnp.arange(n2, dtype=I32)[:, None, None] + n2 * jnp.arange(n1, dtype=I32)[None, None, :]
    ang1 = ((k1[None, :, None] * pos) % l).astype(F32) * (two_pi / l)
    gc, gs = jnp.cos(ang1) * l ** -0.5, jnp.sin(ang1) * l ** -0.5
    m1 = jnp.concatenate([jnp.concatenate([gc, gs], axis=2),
                          jnp.concatenate([-gs, gc], axis=2)], axis=1)
    k2 = jnp.arange(n2, dtype=I32)
    ang2 = ((k2[:, None] * k2[None, :]) % n2).astype(F32) * (two_pi / n2)
    m2 = jnp.concatenate([jnp.cos(ang2), jnp.sin(ang2)], axis=1)
    return w_cs.astype(BF16), m1.astype(BF16), m2.astype(BF16)


def fourier_positions(z, m1, m2, batch):
    t, d2 = z.shape
    d = d2 // 2
    l = t // batch
    n1 = FOURIER_N1
    n2 = l // n1
    a = pl.pallas_call(
        _dft_stage1_kernel,
        grid=(batch, n2),
        in_specs=[pl.BlockSpec((None, 2 * n1, 2 * n1), lambda b, j: (j, 0, 0)),
                  pl.BlockSpec((None, n1, d2), lambda b, j: (b, 0, j))],
        out_specs=pl.BlockSpec((None, 2, None, n1, d), lambda b, j: (b, 0, j, 0, 0)),
        out_shape=jax.ShapeDtypeStruct((batch, 2, n2, n1, d), BF16),
        compiler_params=_params(("parallel", "parallel")),
        name="dft_stage1",
    )(m1, z.reshape(batch, n1, n2 * d2))
    cols = n1 * d
    tn = _tile(cols, 8192)
    f = pl.pallas_call(
        _dft_stage2_kernel,
        grid=(batch, cols // tn),
        in_specs=[pl.BlockSpec((n2, 2 * n2), lambda b, j: (0, 0)),
                  pl.BlockSpec((None, 2 * n2, tn), lambda b, j: (b, 0, j))],
        out_specs=pl.BlockSpec((None, n2, tn), lambda b, j: (b, 0, j)),
        out_shape=jax.ShapeDtypeStruct((batch, n2, cols), BF16),
        compiler_params=_params(("parallel", "parallel")),
        name="dft_stage2",
    )(m2, a.reshape(batch, 2 * n2, cols))
    return f.reshape(t, d)


SUB = 8
ROW_CODE_SHIFT = 14


def _to_row_tiles(ref, val):
    rows = val.shape[0]
    for j in range(SUB):
        ref[pl.ds(j, rows, stride=SUB), :] = val[:, j * LANES:(j + 1) * LANES]


def _from_row_tiles(ref, rows):
    return jnp.concatenate([ref[pl.ds(j, rows, stride=SUB), :] for j in range(SUB)], axis=1)


def _row_tile(ref, row):
    if isinstance(row, int):
        return ref.at[pl.ds(row * SUB, SUB), :]
    return ref.at[pl.ds(pl.multiple_of(row * SUB, SUB), SUB), :]


def _route_kernel(x_ref, g_ref, sh_ref, sc_ref, wr_ref, br_ref, h_ref, ti_ref, tg_ref):
    h = _norm_mod(x_ref[...], g_ref[...], sh_ref[...], sc_ref[...])
    _to_row_tiles(h_ref, h)
    logits = jnp.dot(h, wr_ref[...], precision=HIGHEST, preferred_element_type=F32) + br_ref[...]
    lane = lax.broadcasted_iota(I32, logits.shape, 1)
    vals, idxs = [], []
    cur = logits
    for _ in range(TOP_K):
        m = jnp.max(cur, axis=-1, keepdims=True)
        idx = jnp.min(jnp.where(cur == m, lane, LANES), axis=-1, keepdims=True)
        vals.append(m)
        idxs.append(idx)
        cur = jnp.where(lane == idx, -jnp.inf, cur)
    es = [jnp.exp(v - vals[0]) for v in vals]
    tot = es[0] + es[1] + es[2] + es[3]
    ti = jnp.zeros(logits.shape, I32)
    tg = jnp.zeros(logits.shape, F32)
    for k in range(TOP_K):
        ti = jnp.where(lane == k, idxs[k], ti)
        tg = jnp.where(lane == k, es[k] / tot, tg)
    ti_ref[...] = ti
    tg_ref[...] = tg


def route(x2d, g, shift, scale, rows_per_mod, w_router, b_router):
    t, d = x2d.shape
    assert d == SUB * LANES
    nb = shift.shape[0]
    tm = _tile(rows_per_mod, 512)
    tpm = rows_per_mod // tm
    ne = w_router.shape[1]
    wr = jnp.pad(w_router, ((0, 0), (0, LANES - ne)))
    br = jnp.pad(b_router.reshape(1, ne), ((0, 0), (0, LANES - ne)), constant_values=NEG_BIG)
    return pl.pallas_call(
        _route_kernel,
        grid=(t // tm,),
        in_specs=[pl.BlockSpec((tm, d), lambda i: (i, 0)),
                  pl.BlockSpec((1, d), lambda i: (0, 0)),
                  pl.BlockSpec((None, 1, d), lambda i: (i // tpm, 0, 0)),
                  pl.BlockSpec((None, 1, d), lambda i: (i // tpm, 0, 0)),
                  pl.BlockSpec((d, LANES), lambda i: (0, 0)),
                  pl.BlockSpec((1, LANES), lambda i: (0, 0))],
        out_specs=[pl.BlockSpec((tm * SUB, LANES), lambda i: (i, 0)),
                   pl.BlockSpec((tm, LANES), lambda i: (i, 0)),
                   pl.BlockSpec((tm, LANES), lambda i: (i, 0))],
        out_shape=[jax.ShapeDtypeStruct((t * SUB, LANES), F32), jax.ShapeDtypeStruct((t, LANES), I32),
                   jax.ShapeDtypeStruct((t, LANES), F32)],
        compiler_params=_params(("parallel",)),
        name="moe_route",
    )(x2d, g.reshape(1, d), shift.reshape(nb, 1, d), scale.reshape(nb, 1, d), wr, br)


def _rank_kernel(ti_ref, rank_ref, cnt_ref, carry):
    @pl.when(pl.program_id(0) == 0)
    def _():
        carry[...] = jnp.zeros(carry.shape, carry.dtype)

    ti = ti_ref[...]
    tm = ti.shape[0]
    lane = lax.broadcasted_iota(I32, ti.shape, 1)
    earlier = (lax.broadcasted_iota(I32, (tm, tm), 1) < lax.broadcasted_iota(I32, (tm, tm), 0)).astype(BF16)
    base = carry[...]
    rank = jnp.zeros(ti.shape, I32)
    for k in range(TOP_K):
        hit = lane == ti[:, k:k + 1]
        onehot = jnp.where(hit, 1.0, 0.0)
        before = jnp.dot(earlier, onehot.astype(BF16), preferred_element_type=F32)
        rk = jnp.sum(jnp.where(hit, base + before, 0.0), axis=-1, keepdims=True)
        rank = jnp.where(lane == k, rk.astype(I32), rank)
        base = base + jnp.sum(onehot, axis=0, keepdims=True)
    rank_ref[...] = rank
    carry[...] = base
    cnt_ref[...] = base


def _dest_kernel(ti_ref, rank_ref, start_ref, dest_ref):
    ti = ti_ref[...]
    lane = lax.broadcasted_iota(I32, ti.shape, 1)
    dest = jnp.zeros(ti.shape, I32)
    for k in range(TOP_K):
        start = jnp.sum(jnp.where(lane == ti[:, k:k + 1], start_ref[...], 0), axis=-1, keepdims=True)
        dest = jnp.where(lane == k, start + rank_ref[:, k:k + 1], dest)
    dest_ref[...] = dest


def moe_plan(ti, t):
    tb = MOE_ROWS
    tm = _tile(t, 512)
    rank, cnt = pl.pallas_call(
        _rank_kernel,
        grid=(t // tm,),
        in_specs=[pl.BlockSpec((tm, LANES), lambda i: (i, 0))],
        out_specs=[pl.BlockSpec((tm, LANES), lambda i: (i, 0)), pl.BlockSpec((1, LANES), lambda i: (0, 0))],
        out_shape=[jax.ShapeDtypeStruct((t, LANES), I32), jax.ShapeDtypeStruct((1, LANES), F32)],
        scratch_shapes=[pltpu.VMEM((1, LANES), F32)],
        compiler_params=_params(("arbitrary",)),
        name="moe_rank",
    )(ti)
    counts = cnt[0, :N_EXPERTS].astype(I32)
    padded = (counts + tb - 1) // tb * tb
    pend = jnp.cumsum(padded)
    pstart = jnp.pad(pend - padded, (0, LANES - N_EXPERTS)).reshape(1, LANES)
    dest = pl.pallas_call(
        _dest_kernel,
        grid=(t // tm,),
        in_specs=[pl.BlockSpec((tm, LANES), lambda i: (i, 0)), pl.BlockSpec((tm, LANES), lambda i: (i, 0)),
                  pl.BlockSpec((1, LANES), lambda i: (0, 0))],
        out_specs=pl.BlockSpec((tm, LANES), lambda i: (i, 0)),
        out_shape=jax.ShapeDtypeStruct((t, LANES), I32),
        compiler_params=_params(("parallel",)),
        name="moe_dest",
    )(ti, rank, pstart)
    dest = dest[:, :TOP_K].reshape(-1)
    n_blocks = (t * TOP_K + N_EXPERTS * (tb - 1)) // tb + 1
    assert t <= 1 << ROW_CODE_SHIFT and (TOP_K * t + tb) << ROW_CODE_SHIFT < 1 << 31
    asg = jnp.arange(t * TOP_K, dtype=I32)
    real = (((asg % TOP_K) * t + asg // TOP_K) << ROW_CODE_SHIFT) | (asg // TOP_K)
    spill = (TOP_K * t + jnp.arange(n_blocks * tb, dtype=I32) % tb) << ROW_CODE_SHIFT
    codes = spill.at[dest].set(real)
    starts = jnp.arange(n_blocks, dtype=I32) * tb
    blk_e = jnp.minimum(jnp.sum((pend[None, :] <= starts[:, None]).astype(I32), axis=1), N_EXPERTS - 1)
    nact = (pend[-1] // tb).astype(I32).reshape(1)
    return codes, blk_e, nact


def _expert_kernel(tb, n_tok, be_ref, code_ref, nact_ref, h_hbm, wgu_ref, bgu_ref, wdn_ref, bdn_ref, y_hbm,
                   xbuf, ybuf, wgu_s, wdn_perm, wdn_s, gsem, ssem):
    i = pl.program_id(0)
    n_blocks = pl.num_programs(0)
    nact = nact_ref[0]
    de = wdn_ref.shape[0]
    slot = i % 2
    other = 1 - slot

    def start_gather(base, r, dst_slot, priority=0):
        tok = code_ref[base + r] & ((1 << ROW_CODE_SHIFT) - 1)
        pltpu.make_async_copy(_row_tile(h_hbm, tok), _row_tile(xbuf.at[dst_slot], r),
                              gsem.at[dst_slot]).start(priority)

    def start_scatter(base, r, src_slot, priority=0):
        row = code_ref[base + r] >> ROW_CODE_SHIFT
        pltpu.make_async_copy(_row_tile(ybuf.at[src_slot], r), _row_tile(y_hbm, row),
                              ssem.at[src_slot]).start(priority)

    def wait_gather(s):
        pltpu.make_async_copy(h_hbm.at[pl.ds(0, tb * SUB), :], xbuf.at[s], gsem.at[s]).wait()

    def wait_scatter(s):
        pltpu.make_async_copy(ybuf.at[s], y_hbm.at[pl.ds(0, tb * SUB), :], ssem.at[s]).wait()

    @pl.when(i == 0)
    def _():
        ybuf[...] = jnp.zeros(ybuf.shape, ybuf.dtype)

        def first(r, carry):
            start_gather(0, r, 0)
            return carry
        lax.fori_loop(0, tb, first, 0, unroll=8)

    @pl.when((i < nact) & ((i == 0) | (be_ref[i] != be_ref[jnp.maximum(i - 1, 0)])))
    def _():
        wgu_s[...] = wgu_ref[...].astype(BF16)
        half = LANES // 2
        for c in range(wdn_perm.shape[0]):
            for m in range(0, de, LANES):
                wdn_perm[c, pl.ds(m, half, stride=2), :] = wdn_ref[m:m + half, c * LANES:(c + 1) * LANES]
                wdn_perm[c, pl.ds(m + 1, half, stride=2), :] = wdn_ref[m + half:m + LANES, c * LANES:(c + 1) * LANES]
            wdn_s[:, c * LANES:(c + 1) * LANES] = wdn_perm[c].astype(BF16)

    def start_copies(slot, other):
        wait_gather(slot)

        @pl.when(i >= 1)
        def _():
            wait_scatter(slot)

        prev_base = jnp.where(i == 0, n_blocks - 1, i - 1) * tb
        next_base = (i + 1) * tb
        for r in range(tb):
            start_gather(next_base, r, other, r % 2)
            start_scatter(prev_base, r, other, (r + 1) % 2)

    def run_block(slot):
        x = _from_row_tiles(xbuf.at[slot], tb).astype(BF16)
        even = lax.broadcasted_iota(I32, (tb, LANES), 1) % 2 == 0
        cw = 2 * LANES
        n_chunks = wgu_s.shape[1] // cw
        parts = []
        for c in range(n_chunks):
            gu = jnp.dot(x, wgu_s[:, c * cw:(c + 1) * cw], preferred_element_type=F32) + bgu_ref[:, c * cw:(c + 1) * cw]
            acts = []
            for c0 in range(0, cw, LANES):
                v = gu[:, c0:c0 + LANES]
                glu = jnp.minimum(v, SWIGLU_LIMIT)
                lin = jnp.clip(v, -SWIGLU_LIMIT, SWIGLU_LIMIT) + 1.0
                acts.append(glu * jax.nn.sigmoid(SWIGLU_ALPHA * glu) * pltpu.roll(lin, LANES - 1, 1))
            parts.append(jnp.where(even, acts[0], pltpu.roll(acts[1], 1, 1)).astype(BF16))
        y = jnp.dot(jnp.concatenate(parts, axis=1), wdn_s[...], preferred_element_type=F32) + bdn_ref[...]
        _to_row_tiles(ybuf.at[slot], y)

    for parity in range(2):
        pl.when((i < nact) & (slot == parity))(functools.partial(start_copies, parity, 1 - parity))
    for parity in range(2):
        pl.when((i < nact) & (slot == parity))(functools.partial(run_block, parity))

    @pl.when(i == nact)
    def _():
        wait_gather(slot)
        wait_scatter(slot)

        def last(r, carry):
            start_scatter((i - 1) * tb, r, other)
            return carry
        lax.fori_loop(0, tb, last, 0, unroll=8)
        wait_scatter(other)


def expert_ffn(h_rt, codes, blk_e, nact, layer, w_gu, b_gu, w_dn, b_dn):
    r = codes.shape[0]
    n_tok = h_rt.shape[0] // SUB
    tb = MOE_ROWS
    _, ne, d, de2 = w_gu.shape
    de = de2 // 2
    b_gu = b_gu.reshape(-1, ne, 1, de2)
    b_dn = b_dn.reshape(-1, ne, 1, d)
    return pl.pallas_call(
        functools.partial(_expert_kernel, tb, n_tok),
        grid_spec=pltpu.PrefetchScalarGridSpec(
            num_scalar_prefetch=3, grid=(r // tb,),
            in_specs=[pl.BlockSpec(memory_space=pl.ANY),
                      pl.BlockSpec((None, None, d, de2), lambda i, be, cd, na: (layer, be[i], 0, 0)),
                      pl.BlockSpec((None, None, 1, de2), lambda i, be, cd, na: (layer, be[i], 0, 0)),
                      pl.BlockSpec((None, None, de, d), lambda i, be, cd, na: (layer, be[i], 0, 0)),
                      pl.BlockSpec((None, None, 1, d), lambda i, be, cd, na: (layer, be[i], 0, 0))],
            out_specs=pl.BlockSpec(memory_space=pl.ANY),
            scratch_shapes=[pltpu.VMEM((2, tb * SUB, LANES), F32), pltpu.VMEM((2, tb * SUB, LANES), F32),
                            pltpu.VMEM((d, de2), BF16), pltpu.VMEM((d // LANES, de, LANES), F32),
                            pltpu.VMEM((de, d), BF16),
                            pltpu.SemaphoreType.DMA((2,)), pltpu.SemaphoreType.DMA((2,))]),
        out_shape=jax.ShapeDtypeStruct(((TOP_K * n_tok + tb) * SUB, LANES), F32),
        compiler_params=_params(("arbitrary",), 58),
        name="moe_experts",
    )(blk_e, codes, nact, h_rt, w_gu, b_gu, w_dn, b_dn)


def _combine_kernel(y0_ref, y1_ref, y2_ref, y3_ref, x_ref, gate_ref, tg_ref, o_ref):
    tg = tg_ref[...]
    tm = x_ref.shape[0]
    ys = [_from_row_tiles(y_ref, tm) * tg[:, k:k + 1] for k, y_ref in enumerate((y0_ref, y1_ref, y2_ref, y3_ref))]
    o_ref[...] = x_ref[...] + gate_ref[...] * ((ys[0] + ys[1]) + (ys[2] + ys[3]))


def moe_combine(y4, tg, x2d, gate, rows_per_mod):
    t, d = x2d.shape
    nb = gate.shape[0]
    tm = _tile(rows_per_mod, MOE_ROWS)
    tpm = rows_per_mod // tm
    nt = t // tm
    planes = [pl.BlockSpec((tm * SUB, LANES), functools.partial(lambda k, i: (k * nt + i, 0), k))
              for k in range(TOP_K)]
    return pl.pallas_call(
        _combine_kernel,
        grid=(nt,),
        in_specs=planes + [
                  pl.BlockSpec((tm, d), lambda i: (i, 0)),
                  pl.BlockSpec((None, 1, d), lambda i: (i // tpm, 0, 0)),
                  pl.BlockSpec((tm, LANES), lambda i: (i, 0))],
        out_specs=pl.BlockSpec((tm, d), lambda i: (i, 0)),
        out_shape=jax.ShapeDtypeStruct((t, d), F32),
        compiler_params=_params(("parallel",)),
        name="moe_combine",
    )(y4, y4, y4, y4, x2d, gate.reshape(nb, 1, d), tg)


def moe_layer(x2d, norm_g, shift, scale, gate, rows_per_mod, w_router, b_router, layer, w_gu, b_gu, w_dn, b_dn):
    t = x2d.shape[0]
    h_rt, ti, tg = route(x2d, norm_g, shift, scale, rows_per_mod, w_router, b_router)
    codes, blk_e, nact = moe_plan(ti, t)
    y4 = expert_ffn(h_rt, codes, blk_e, nact, layer, w_gu, b_gu, w_dn, b_dn)
    return moe_combine(y4, tg, x2d, gate, rows_per_mod)


def _rope_tables(s):
    n_freq = DA_HEAD_DIM // 4
    freqs = ROPE_BASE ** (-jnp.arange(n_freq, dtype=F32) / n_freq)
    pos = jnp.arange(s, dtype=I32)
    ang_r = (pos // GRID_W).astype(F32)[:, None] * freqs
    ang_c = (pos % GRID_W).astype(F32)[:, None] * freqs
    ang = jnp.concatenate([ang_r, ang_r, ang_c, ang_c], axis=-1)
    ang = jnp.concatenate([ang, ang], axis=-1)
    return jnp.cos(ang), jnp.sin(ang)


def _even_layer(x2d, ctx2d, mods, norm_g1, w_in, w_out, q_norm_g, k_norm_g, da_lambda, da_subln_g,
                conv_xb_w, conv_xb_b, conv_c_w, conv_c_b, dt_bias, a_log, d_skip, ssm_norm_g,
                lam_init, batch):
    sx1, cx1, gx1, sc1, cc1 = mods
    t, d = x2d.shape
    s = t // batch
    ctx_len = ctx2d.shape[0] // batch
    qkw = DA_HEADS * 2 * DA_HEAD_DIM
    vw = DA_HEADS * DA_V_DIM
    col_q, col_z = 0, qkw
    col_c = col_z + SSM_D_INNER
    col_k = col_c + SSM_BC_W
    col_v = col_k + qkw
    col_xb = col_v + vw
    col_dt = col_xb + SSM_XB_W
    wb = w_in.astype(BF16)
    w_q, w_k, w_v = wb[:, col_q:col_q + qkw], wb[:, col_k:col_k + qkw], wb[:, col_v:col_v + vw]
    w_z = wb[:, col_z:col_z + SSM_D_INNER]
    w_xbc = jnp.concatenate([wb[:, col_xb:col_xb + SSM_XB_W], wb[:, col_c:col_c + SSM_BC_W]], axis=1)
    w_dt = jnp.pad(wb[:, col_dt:col_dt + 2 * SSM_HEADS], ((0, 0), (0, LANES - 2 * SSM_HEADS)))
    weights = [w_q, w_k, w_v, w_z, w_xbc, w_dt]
    dts = [BF16, BF16, BF16, BF16, BF16, F32]
    q_u, k_u, v_x, z_x, xbc_u, dt_x = nm_panel(x2d, norm_g1, sx1, cx1, s, weights, dts)
    _, kc_u, v_c, _, xbc_cu, dt_c = nm_panel(ctx2d, norm_g1, sc1, cc1, batch * ctx_len, weights, dts)

    lam = (jnp.exp(jnp.sum(da_lambda[0] * da_lambda[1])) -
           jnp.exp(jnp.sum(da_lambda[2] * da_lambda[3]))).astype(F32) + lam_init
    a_neg = -jnp.exp(a_log.astype(F32)).reshape(-1)

    cos_t, sin_t = _rope_tables(s)
    q = qk_prep(q_u, q_norm_g, cos_t, sin_t, s, True, DA_HEAD_DIM ** -0.5 * LOG2E)
    score_bound = (1.01 * DA_HEAD_DIM ** 0.5 * LOG2E) * jnp.max(jnp.abs(q_norm_g)) * jnp.max(jnp.abs(k_norm_g))
    k_x = qk_prep(k_u, k_norm_g, cos_t, sin_t, s, True, 1.0)
    k_c = qk_prep(kc_u, k_norm_g, cos_t, sin_t, ctx_len, False, 1.0)
    k_all = jnp.concatenate([k_x.reshape(batch, s, qkw), k_c.reshape(batch, ctx_len, qkw)], axis=1)
    v_all = jnp.concatenate([v_x.reshape(batch, s, vw), v_c.reshape(batch, ctx_len, vw)], axis=1)
    o_att = diff_attention(q, k_all, v_all, lam, score_bound, da_subln_g, lam_init, batch)

    conv_w = jnp.concatenate([conv_xb_w, conv_c_w], axis=1)
    conv_b = jnp.concatenate([conv_xb_b, conv_c_b], axis=0)
    xbc_x = conv_silu(xbc_u, conv_w, conv_b, s)
    xbc_c = conv_silu(xbc_cu, conv_w, conv_b, ctx_len)
    hshape = (batch, SSM_GROUPS, SSM_STATE, SSM_HPG * SSM_HEAD_DIM)
    zero_h = jnp.zeros(hshape, F32)
    dt_bias_f = dt_bias.reshape(-1).astype(F32)
    _, _, h_f, h_b = ssd_bidir(xbc_c, dt_c, dt_bias_f, a_neg, d_skip, zero_h, zero_h, batch)
    y_f, y_b, _, _ = ssd_bidir(xbc_x, dt_x, dt_bias_f, a_neg, d_skip, h_f, h_b, batch)
    wo = w_out.astype(BF16)
    return out_proj(o_att, y_f, y_b, z_x, ssm_norm_g, wo[:vw], wo[vw:], x2d, gx1, s)


def _odd_layer(x2d, mods, norm_g1, w_fourier, batch):
    sx1, cx1, gx1 = mods
    t, d = x2d.shape
    s = t // batch
    w_cs, m1, m2 = _dft_tables(s, d)
    (z,) = nm_panel(x2d, norm_g1, sx1, cx1, s, [w_cs], [BF16])
    f = fourier_positions(z, m1, m2, batch)
    return matmul_residual(f, w_fourier.astype(BF16), x2d, gx1, s)


def kernel(x, c, ctx, c_ctx, ada_w, ada_b, norm_g, w_in, w_out, q_norm_g, k_norm_g, da_lambda, da_subln_g, conv_xb_w, conv_xb_b, conv_c_w, conv_c_b, dt_bias, a_log, d_skip, ssm_norm_g, w_fourier, w_router, b_router, w_gate_up, b_gate_up, w_down, b_down):
    b, s, d = x.shape
    depth = ada_w.shape[0]
    assert b + 1 <= 8
    cvecs = jnp.concatenate([c, c_ctx[None, :], jnp.zeros((8 - b - 1, d), F32)], axis=0)
    mods = ada_all(cvecs, ada_w, ada_b)
    x2d = x.reshape(b * s, d)
    ctx2d = ctx.reshape(-1, d)
    for i in range(depth):
        m = mods[i].reshape(8, 6, d)
        sx1, cx1, gx1, sx2, cx2, gx2 = [m[:b, j] for j in range(6)]
        if i % 2 == 0:
            e = i // 2
            lam_init = 0.8 - 0.6 * math.exp(-0.3 * i)
            if any(j % 2 == 0 for j in range(i + 1, depth)):
                raise NotImplementedError("context stream output is only needed for depth > 2")
            sc1, cc1 = m[b:b + 1, 0], m[b:b + 1, 1]
            x2d = _even_layer(x2d, ctx2d, (sx1, cx1, gx1, sc1, cc1), norm_g[i, 0], w_in[e], w_out[e],
                              q_norm_g[e], k_norm_g[e], da_lambda[e], da_subln_g[e], conv_xb_w[e],
                              conv_xb_b[e], conv_c_w[e], conv_c_b[e], dt_bias[e], a_log[e], d_skip[e],
                              ssm_norm_g[e], lam_init, b)
        else:
            x2d = _odd_layer(x2d, (sx1, cx1, gx1), norm_g[i, 0], w_fourier[i // 2], b)
        x2d = moe_layer(x2d, norm_g[i, 1], sx2, cx2, gx2, s, w_router[i], b_router[i],
                        i, w_gate_up, b_gate_up, w_down, b_down)
    return x2d.reshape(b, s, d)
```

```python
import functools
import math

import jax
import jax.numpy as jnp
from jax import lax
from jax.experimental import pallas as pl
from jax.experimental.pallas import tpu as pltpu

F32 = jnp.float32
BF16 = jnp.bfloat16
I32 = jnp.int32

RMS_EPS = 1e-6
GRID_W = 64
ROPE_BASE = 10000.0
DA_HEADS = 8
DA_HEAD_DIM = 64
DA_V_DIM = 128
SSM_HEADS = 16
SSM_HEAD_DIM = 64
SSM_GROUPS = 2
SSM_HPG = 8
SSM_STATE = 128
SSM_CONV = 5
SSM_CHUNK = 128
SSM_D_INNER = 1024
SSM_BC_W = 256
SSM_XB_W = 1280
FOURIER_GROUPS = 4
FOURIER_N1 = 128
N_EXPERTS = 32
TOP_K = 4
SWIGLU_LIMIT = 7.0
SWIGLU_ALPHA = 1.702
MOE_ROWS = 256
LANES = 128
NEG_BIG = -1e30
MIB = 1024 * 1024
HIGHEST = lax.Precision.HIGHEST


def _params(sem, vmem_mib=48):
    return pltpu.CompilerParams(dimension_semantics=sem, vmem_limit_bytes=vmem_mib * MIB)


def _tile(n, pref):
    t = min(n, pref)
    while n % t:
        t //= 2
    return t


def _silu(v):
    return v * jax.nn.sigmoid(v)


def _norm_mod(xf, g, shift, scale):
    r = lax.rsqrt(jnp.mean(xf * xf, axis=-1, keepdims=True) + RMS_EPS)
    return ((xf * r) * g) * (1.0 + scale) + shift


def _ada_kernel(a_ref, w_ref, b_ref, o_ref):
    s = _silu(a_ref[...])
    o_ref[...] = jnp.dot(s.astype(BF16), w_ref[...].astype(BF16),
                         preferred_element_type=F32) + b_ref[...]


def ada_all(cvecs, ada_w, ada_b):
    depth, d, n = ada_w.shape
    tn = _tile(n, 1536)
    return pl.pallas_call(
        _ada_kernel,
        grid=(depth, n // tn),
        in_specs=[pl.BlockSpec((8, d), lambda l, j: (0, 0)),
                  pl.BlockSpec((None, d, tn), lambda l, j: (l, 0, j)),
                  pl.BlockSpec((None, 1, tn), lambda l, j: (l, 0, j))],
        out_specs=pl.BlockSpec((None, 8, tn), lambda l, j: (l, 0, j)),
        out_shape=jax.ShapeDtypeStruct((depth, 8, n), F32),
        compiler_params=_params(("parallel", "parallel")),
        name="ada_mod",
    )(cvecs, ada_w, ada_b.reshape(depth, 1, n))


def _nm_panel_kernel(n_w, x_ref, g_ref, sh_ref, sc_ref, *refs):
    w_refs, o_refs = refs[:n_w], refs[n_w:2 * n_w]
    hb = _norm_mod(x_ref[...], g_ref[...], sh_ref[...], sc_ref[...]).astype(BF16)
    for w_ref, o_ref in zip(w_refs, o_refs):
        n = w_ref.shape[1]
        cw = _tile(n, 512)
        for c0 in range(0, n, cw):
            o_ref[:, c0:c0 + cw] = jnp.dot(hb, w_ref[:, c0:c0 + cw],
                                           preferred_element_type=F32).astype(o_ref.dtype)


def nm_panel(x2d, g, shift, scale, rows_per_mod, weights, out_dtypes, tm_pref=256):
    t, d = x2d.shape
    nb = shift.shape[0]
    tm = _tile(rows_per_mod, tm_pref)
    tpm = rows_per_mod // tm
    n_w = len(weights)
    in_specs = [pl.BlockSpec((tm, d), lambda i: (i, 0)),
                pl.BlockSpec((1, d), lambda i: (0, 0)),
                pl.BlockSpec((None, 1, d), lambda i: (i // tpm, 0, 0)),
                pl.BlockSpec((None, 1, d), lambda i: (i // tpm, 0, 0))]
    in_specs += [pl.BlockSpec(w.shape, lambda i: (0, 0)) for w in weights]
    out_specs = [pl.BlockSpec((tm, w.shape[1]), lambda i: (i, 0)) for w in weights]
    out_shape = [jax.ShapeDtypeStruct((t, w.shape[1]), dt) for w, dt in zip(weights, out_dtypes)]
    return pl.pallas_call(
        functools.partial(_nm_panel_kernel, n_w),
        grid=(t // tm,),
        in_specs=in_specs, out_specs=out_specs, out_shape=out_shape,
        compiler_params=_params(("parallel",), 56),
        name="norm_mod_proj",
    )(x2d, g.reshape(1, d), shift.reshape(nb, 1, d), scale.reshape(nb, 1, d), *weights)


def _qk_prep_kernel(rope, out_scale, u_ref, g_ref, seg_ref, cos_ref, sin_ref, o_ref):
    seg = seg_ref[...]
    g = g_ref[...]
    n_heads = u_ref.shape[1] // LANES
    for c in range(n_heads):
        u = u_ref[:, c * LANES:(c + 1) * LANES].astype(F32)
        u2 = u * u
        hi = u2.astype(BF16)
        lo = (u2 - hi.astype(F32)).astype(BF16)
        ss = (jnp.dot(hi, seg, preferred_element_type=F32)
              + jnp.dot(lo, seg, preferred_element_type=F32))
        nrm = (u * lax.rsqrt(ss * (1.0 / DA_HEAD_DIM) + RMS_EPS)) * g
        if rope:
            lane = lax.broadcasted_iota(I32, nrm.shape, 1)
            first = (lane % 32) < 16
            rot = jnp.where(first, -pltpu.roll(nrm, LANES - 16, 1), pltpu.roll(nrm, 16, 1))
            nrm = nrm * cos_ref[...] + rot * sin_ref[...]
        o_ref[:, c * LANES:(c + 1) * LANES] = (nrm * out_scale).astype(o_ref.dtype)


def qk_prep(u, gain, cos_t, sin_t, seq, rope, out_scale):
    t, w = u.shape
    tm = _tile(seq, 256)
    tps = seq // tm
    seg = (jnp.arange(LANES)[:, None] // DA_HEAD_DIM == jnp.arange(LANES)[None, :] // DA_HEAD_DIM)
    return pl.pallas_call(
        functools.partial(_qk_prep_kernel, rope, out_scale),
        grid=(t // tm,),
        in_specs=[pl.BlockSpec((tm, w), lambda i: (i, 0)),
                  pl.BlockSpec((1, LANES), lambda i: (0, 0)),
                  pl.BlockSpec((LANES, LANES), lambda i: (0, 0)),
                  pl.BlockSpec((tm, LANES), lambda i: (i % tps, 0)),
                  pl.BlockSpec((tm, LANES), lambda i: (i % tps, 0))],
        out_specs=pl.BlockSpec((tm, w), lambda i: (i, 0)),
        out_shape=jax.ShapeDtypeStruct((t, w), BF16),
        compiler_params=_params(("parallel",)),
        name="qk_norm_rope",
    )(u, jnp.tile(gain, 2).reshape(1, LANES), seg.astype(BF16), cos_t, sin_t)


LOG2E = math.log2(math.e)
ATTN_SHIFT_LIMIT = 60.0


def _split_components(q):
    lane = lax.broadcasted_iota(I32, q.shape, 1)
    zero = jnp.zeros_like(q)
    return jnp.where(lane < DA_HEAD_DIM, q, zero), jnp.where(lane >= DA_HEAD_DIM, q, zero)


def _attn_finish(n0, l0, n1, l1, lam, g, out_mult, o_ref):
    o = n0 / l0 - lam * (n1 / l1)
    o = (o * lax.rsqrt(jnp.mean(o * o, axis=-1, keepdims=True) + RMS_EPS)) * g
    o_ref[...] = (o * out_mult).astype(o_ref.dtype)


def _attn_online_kernel(tk, out_mult, sc_ref, q_ref, k_ref, v_ref, g_ref, o_ref):
    qs = _split_components(q_ref[...])
    tq = q_ref.shape[0]
    n_kv = k_ref.shape[0] // tk

    def body(i, carry):
        off = pl.multiple_of(i * tk, tk)
        k = k_ref[pl.ds(off, tk), :]
        v = v_ref[pl.ds(off, tk), :]
        new = []
        for c in range(2):
            m, l, acc = carry[c]
            s = lax.dot_general(qs[c], k, (((1,), (1,)), ((), ())), preferred_element_type=F32)
            m_new = jnp.maximum(m, jnp.max(s, axis=-1, keepdims=True))
            alpha = jnp.exp2(m - m_new)
            p = jnp.exp2(s - m_new)
            l_new = alpha * l + jnp.sum(p, axis=-1, keepdims=True)
            acc_new = alpha * acc + jnp.dot(p.astype(BF16), v, preferred_element_type=F32)
            new.append((m_new, l_new, acc_new))
        return tuple(new)

    init = tuple((jnp.full((tq, 1), -jnp.inf, F32), jnp.zeros((tq, 1), F32),
                  jnp.zeros((tq, DA_V_DIM), F32)) for _ in range(2))
    (_, l0, a0), (_, l1, a1) = lax.fori_loop(0, n_kv, body, init)
    _attn_finish(a0, l0, a1, l1, sc_ref[0], g_ref[...], out_mult, o_ref)


def _attn_shift_kernel(tk, out_mult, sc_ref, q_ref, k_ref, v_ref, g_ref, o_ref):
    qs = _split_components(q_ref[...])
    tq = q_ref.shape[0]
    n_kv = k_ref.shape[0] // tk
    shift = sc_ref[1]
    ones = jnp.ones((tk, LANES), BF16)

    sub = 2 * LANES

    def body(i, acc):
        acc = list(acc)
        for j in range(tk // sub):
            off = pl.multiple_of(i * tk + j * sub, sub)
            k = k_ref[pl.ds(off, sub), :]
            va = jnp.concatenate([v_ref[pl.ds(off, sub), :], ones[:sub]], axis=1)
            for c in range(2):
                s = lax.dot_general(qs[c], k, (((1,), (1,)), ((), ())), preferred_element_type=F32)
                acc[c] = acc[c] + jnp.dot(jnp.exp2(s - shift).astype(BF16), va, preferred_element_type=F32)
        return tuple(acc)

    zero = jnp.zeros((tq, 2 * LANES), F32)
    a0, a1 = lax.fori_loop(0, n_kv, body, (zero, zero))
    _attn_finish(a0[:, :LANES], a0[:, LANES:], a1[:, :LANES], a1[:, LANES:], sc_ref[0], g_ref[...], out_mult,
                 o_ref)


def diff_attention(q, k_all, v_all, lam, score_bound, subln_g, lam_init, batch):
    t, w = q.shape
    s = t // batch
    sk = k_all.shape[1]
    tq = _tile(s, 1024)
    tk = 2816 if sk % 2816 == 0 else _tile(sk, 256)
    nq = s // tq
    scalars = jnp.stack([lam, score_bound]).astype(F32)

    def run(body, tk):
        return pl.pallas_call(
            functools.partial(body, tk, 1.0 - lam_init),
            grid=(batch, DA_HEADS, nq),
            in_specs=[pl.BlockSpec(memory_space=pltpu.SMEM),
                      pl.BlockSpec((tq, LANES), lambda b, h, i: (b * nq + i, h)),
                      pl.BlockSpec((None, sk, LANES), lambda b, h, i: (b, 0, h)),
                      pl.BlockSpec((None, sk, LANES), lambda b, h, i: (b, 0, h)),
                      pl.BlockSpec((1, LANES), lambda b, h, i: (0, 0))],
            out_specs=pl.BlockSpec((tq, LANES), lambda b, h, i: (b * nq + i, h)),
            out_shape=jax.ShapeDtypeStruct((t, w), BF16),
            compiler_params=_params(("parallel", "parallel", "parallel")),
            name="diff_attention",
        )(scalars, q, k_all, v_all, subln_g.reshape(1, LANES))

    return lax.cond(score_bound <= ATTN_SHIFT_LIMIT,
                    lambda: run(_attn_shift_kernel, sk), lambda: run(_attn_online_kernel, tk))


HALO = 16


def _conv_kernel(tps, p_ref, c_ref, n_ref, w_ref, b_ref, o_ref, ext_ref):
    t = pl.program_id(0) % tps
    tm = c_ref.shape[0]
    prev = p_ref[...].astype(F32)
    nxt = n_ref[...].astype(F32)
    ext_ref[0:HALO, :] = jnp.where(t == 0, 0.0, prev)
    ext_ref[HALO:HALO + tm, :] = c_ref[...].astype(F32)
    ext_ref[HALO + tm:2 * HALO + tm, :] = jnp.where(t == tps - 1, 0.0, nxt)
    width = c_ref.shape[1]
    cw = _tile(width, 256)
    pad = (SSM_CONV - 1) // 2
    for c0 in range(0, width, cw):
        acc = jnp.broadcast_to(b_ref[:, c0:c0 + cw], (tm, cw))
        for k in range(SSM_CONV):
            acc = acc + w_ref[k:k + 1, c0:c0 + cw] * ext_ref[HALO - pad + k:HALO - pad + k + tm, c0:c0 + cw]
        o_ref[:, c0:c0 + cw] = _silu(acc).astype(o_ref.dtype)


def conv_silu(u, w, b, seq):
    t, c = u.shape
    tm = _tile(seq, 512)
    tps = seq // tm
    hb = tm // HALO
    last = t // HALO - 1
    return pl.pallas_call(
        functools.partial(_conv_kernel, tps),
        grid=(t // tm,),
        in_specs=[pl.BlockSpec((HALO, c), lambda i: (jnp.maximum(i * hb - 1, 0), 0)),
                  pl.BlockSpec((tm, c), lambda i: (i, 0)),
                  pl.BlockSpec((HALO, c), lambda i: (jnp.minimum((i + 1) * hb, last), 0)),
                  pl.BlockSpec((SSM_CONV, c), lambda i: (0, 0)),
                  pl.BlockSpec((1, c), lambda i: (0, 0))],
        out_specs=pl.BlockSpec((tm, c), lambda i: (i, 0)),
        out_shape=jax.ShapeDtypeStruct((t, c), BF16),
        scratch_shapes=[pltpu.VMEM((tm + 2 * HALO, c), F32)],
        compiler_params=_params(("parallel",)),
        name="dwconv_silu",
    )(u, u, u, w, b.reshape(1, c))


def _softplus(v):
    return jnp.maximum(v, 0.0) + jnp.log1p(jnp.exp(-jnp.abs(v)))


def _pair(lane_lo, a, b):
    return jnp.where(lane_lo, a, b)


def _ssd_kernel(xf_ref, dtf_ref, dtTf_ref, xb_ref, dtb_ref, dtTb_ref, bias_r, a_r, bias_c, a_c,
                dsk_ref, h0f_ref, h0b_ref, yf_ref, yb_ref, hfo_ref, hbo_ref, hf_s, hb_s):
    c = pl.program_id(1)
    q = SSM_CHUNK
    nh = SSM_HEADS

    @pl.when(c == 0)
    def _():
        hf_s[...] = h0f_ref[...]
        hb_s[...] = h0b_ref[...]

    li = lax.broadcasted_iota(I32, (q, q), 0)
    si = lax.broadcasted_iota(I32, (q, q), 1)
    lower = li >= si
    upper = li <= si
    tri_l = lower.astype(F32)
    tri_u = upper.astype(F32)
    lane_lo = lax.broadcasted_iota(I32, (q, LANES), 1) < SSM_HEAD_DIM
    lane_lo1 = lax.broadcasted_iota(I32, (1, LANES), 1) < SSM_HEAD_DIM

    def bc(col):
        return jnp.broadcast_to(col, (q, LANES))

    def state_update(h_s, g, xs32, bg, wq, edge, base):
        parts, decs = [], []
        for pr in range(SSM_HPG // 2):
            h0 = g * SSM_HPG + 2 * pr
            wp = _pair(lane_lo, bc(wq[:, base + h0:base + h0 + 1]), bc(wq[:, base + h0 + 1:base + h0 + 2]))
            parts.append((xs32[:, h0 * 64:h0 * 64 + LANES] * wp).astype(BF16))
            decs.append(_pair(lane_lo1, jnp.broadcast_to(edge[:, base + h0:base + h0 + 1], (1, LANES)),
                              jnp.broadcast_to(edge[:, base + h0 + 1:base + h0 + 2], (1, LANES))))
        xw = jnp.concatenate(parts, axis=1)
        dec = jnp.exp(jnp.concatenate(decs, axis=1))
        upd = lax.dot_general(bg, xw, (((0,), (0,)), ((), ())), preferred_element_type=F32)
        h_s[g] = h_s[g] * dec + upd

    xbc = xf_ref[...]
    xs_b = xbc[:, :SSM_D_INNER]
    xs32 = xs_b.astype(F32)
    dt = _softplus(dtf_ref[...] + bias_r[...])
    da = dt * a_r[...]
    acs = jnp.dot(tri_l, da, precision=HIGHEST, preferred_element_type=F32)
    racs = jnp.dot(tri_u, da, precision=HIGHEST, preferred_element_type=F32)
    dt_t = _softplus(dtTf_ref[...] + bias_c[...])
    da_t = dt_t * a_c[...]
    acs_t = jnp.dot(da_t, tri_u, precision=HIGHEST, preferred_element_type=F32)
    racs_t = jnp.dot(da_t, tri_l, precision=HIGHEST, preferred_element_type=F32)
    last = acs[q - 1:q, :]
    wq = jnp.exp(last - acs) * dt
    for g in range(SSM_GROUPS):
        bg = xbc[:, SSM_D_INNER + g * SSM_STATE:SSM_D_INNER + (g + 1) * SSM_STATE]
        cg = xbc[:, SSM_D_INNER + SSM_BC_W + g * SSM_STATE:SSM_D_INNER + SSM_BC_W + (g + 1) * SSM_STATE]
        cb = lax.dot_general(cg, bg, (((1,), (1,)), ((), ())), preferred_element_type=F32)
        yoff = jnp.dot(cg, hf_s[g].astype(BF16), preferred_element_type=F32)
        for pr in range(SSM_HPG // 2):
            h0 = g * SSM_HPG + 2 * pr
            col0 = h0 * SSM_HEAD_DIM
            xp = xs_b[:, col0:col0 + LANES]
            ys = []
            for hh in (h0, h0 + 1):
                mf = jnp.exp(jnp.where(lower, acs[:, hh:hh + 1] - acs_t[hh:hh + 1, :], NEG_BIG)) \
                    * dt_t[hh:hh + 1, :]
                mb = jnp.exp(jnp.where(upper, racs[:, nh + hh:nh + hh + 1] - racs_t[nh + hh:nh + hh + 1, :],
                                       NEG_BIG)) * dt_t[nh + hh:nh + hh + 1, :]
                ys.append(jnp.dot((cb * (mf + mb)).astype(BF16), xp, preferred_element_type=F32))
            ef = jnp.exp(_pair(lane_lo, bc(acs[:, h0:h0 + 1]), bc(acs[:, h0 + 1:h0 + 2])))
            yf_ref[:, col0:col0 + LANES] = (_pair(lane_lo, ys[0], ys[1])
                                            + yoff[:, 2 * pr * 64:2 * pr * 64 + LANES] * ef
                                            + xs32[:, col0:col0 + LANES] * dsk_ref[:, col0:col0 + LANES])
        state_update(hf_s, g, xs32, bg, wq, last, 0)

    xbc2 = xb_ref[...]
    xs2 = xbc2[:, :SSM_D_INNER].astype(F32)
    dt2 = _softplus(dtb_ref[...] + bias_r[...])
    racs2 = jnp.dot(tri_u, dt2 * a_r[...], precision=HIGHEST, preferred_element_type=F32)
    first = racs2[0:1, :]
    wq2 = jnp.exp(first - racs2) * dt2
    for g in range(SSM_GROUPS):
        bg = xbc2[:, SSM_D_INNER + g * SSM_STATE:SSM_D_INNER + (g + 1) * SSM_STATE]
        cg = xbc2[:, SSM_D_INNER + SSM_BC_W + g * SSM_STATE:SSM_D_INNER + SSM_BC_W + (g + 1) * SSM_STATE]
        yoff = jnp.dot(cg, hb_s[g].astype(BF16), preferred_element_type=F32)
        for pr in range(SSM_HPG // 2):
            h0 = g * SSM_HPG + 2 * pr
            col0 = h0 * SSM_HEAD_DIM
            eb = jnp.exp(_pair(lane_lo, bc(racs2[:, nh + h0:nh + h0 + 1]), bc(racs2[:, nh + h0 + 1:nh + h0 + 2])))
            yb_ref[:, col0:col0 + LANES] = yoff[:, 2 * pr * 64:2 * pr * 64 + LANES] * eb
        state_update(hb_s, g, xs2, bg, wq2, first, nh)

    @pl.when(c == pl.num_programs(1) - 1)
    def _():
        hfo_ref[...] = hf_s[...]
        hbo_ref[...] = hb_s[...]


def ssd_bidir(xbc, dt_raw, dt_bias, a_neg, d_skip, h0f, h0b, batch):
    t = xbc.shape[0]
    l = t // batch
    q = SSM_CHUNK
    nc = l // q
    w = xbc.shape[1]
    dt_t = jnp.transpose(dt_raw[:, :2 * SSM_HEADS].reshape(batch, l, 2 * SSM_HEADS), (0, 2, 1))
    pad = LANES - 2 * SSM_HEADS
    bias_r = jnp.pad(dt_bias.reshape(1, -1), ((0, 0), (0, pad)))
    a_r = jnp.pad(a_neg.reshape(1, -1), ((0, 0), (0, pad)))
    dsk = jnp.repeat(d_skip, SSM_HEAD_DIM).reshape(1, SSM_D_INNER)
    hshape = (batch, SSM_GROUPS, SSM_STATE, SSM_HPG * SSM_HEAD_DIM)
    fwd = lambda b, c: (b * nc + c, 0)
    bwd = lambda b, c: (b * nc + nc - 1 - c, 0)
    const2 = lambda b, c: (0, 0)
    hmap = lambda b, c: (b, 0, 0, 0)
    hspec = pl.BlockSpec((None,) + hshape[1:], hmap)
    return pl.pallas_call(
        _ssd_kernel,
        grid=(batch, nc),
        in_specs=[pl.BlockSpec((q, w), fwd), pl.BlockSpec((q, LANES), fwd),
                  pl.BlockSpec((None, 2 * SSM_HEADS, q), lambda b, c: (b, 0, c)),
                  pl.BlockSpec((q, w), bwd), pl.BlockSpec((q, LANES), bwd),
                  pl.BlockSpec((None, 2 * SSM_HEADS, q), lambda b, c: (b, 0, nc - 1 - c)),
                  pl.BlockSpec((1, LANES), const2), pl.BlockSpec((1, LANES), const2),
                  pl.BlockSpec((2 * SSM_HEADS, 1), const2), pl.BlockSpec((2 * SSM_HEADS, 1), const2),
                  pl.BlockSpec((1, SSM_D_INNER), const2), hspec, hspec],
        out_specs=[pl.BlockSpec((q, SSM_D_INNER), fwd), pl.BlockSpec((q, SSM_D_INNER), bwd), hspec, hspec],
        out_shape=[jax.ShapeDtypeStruct((t, SSM_D_INNER), F32), jax.ShapeDtypeStruct((t, SSM_D_INNER), F32),
                   jax.ShapeDtypeStruct(hshape, F32), jax.ShapeDtypeStruct(hshape, F32)],
        scratch_shapes=[pltpu.VMEM(hshape[1:], F32), pltpu.VMEM(hshape[1:], F32)],
        compiler_params=_params(("parallel", "arbitrary")),
        name="ssd_bidir",
    )(xbc, dt_raw, dt_t, xbc, dt_raw, dt_t, bias_r, a_r, dt_bias.reshape(-1, 1), a_neg.reshape(-1, 1),
      dsk, h0f, h0b)


def _outproj_kernel(oa_ref, yf_ref, yb_ref, z_ref, ng_ref, w1_ref, w2_ref, x_ref, gate_ref, o_ref):
    y = yf_ref[...] + yb_ref[...]
    u = y * _silu(z_ref[...].astype(F32))
    un = (u * lax.rsqrt(jnp.mean(u * u, axis=-1, keepdims=True) + RMS_EPS)) * ng_ref[...]
    acc = (jnp.dot(oa_ref[...], w1_ref[...], preferred_element_type=F32)
           + jnp.dot(un.astype(BF16), w2_ref[...], preferred_element_type=F32))
    o_ref[...] = x_ref[...] + gate_ref[...] * acc


def out_proj(o_att, yf, yb, z, norm_g, w1, w2, x2d, gate, rows_per_mod):
    t, d = x2d.shape
    tm = _tile(rows_per_mod, 512)
    tpm = rows_per_mod // tm
    nb = gate.shape[0]
    row = lambda i: (i, 0)
    const = lambda i: (0, 0)
    return pl.pallas_call(
        _outproj_kernel,
        grid=(t // tm,),
        in_specs=[pl.BlockSpec((tm, d), row), pl.BlockSpec((tm, d), row), pl.BlockSpec((tm, d), row),
                  pl.BlockSpec((tm, d), row), pl.BlockSpec((1, d), const),
                  pl.BlockSpec(w1.shape, const), pl.BlockSpec(w2.shape, const),
                  pl.BlockSpec((tm, d), row), pl.BlockSpec((None, 1, d), lambda i: (i // tpm, 0, 0))],
        out_specs=pl.BlockSpec((tm, d), row),
        out_shape=jax.ShapeDtypeStruct((t, d), F32),
        compiler_params=_params(("parallel",)),
        name="mixer_out_proj",
    )(o_att, yf, yb, z, norm_g.reshape(1, d), w1, w2, x2d, gate.reshape(nb, 1, d))


def _mm_res_kernel(a_ref, w_ref, x_ref, gate_ref, o_ref):
    acc = jnp.dot(a_ref[...], w_ref[...], preferred_element_type=F32)
    o_ref[...] = x_ref[...] + gate_ref[...] * acc


def matmul_residual(a, w, x2d, gate, rows_per_mod):
    t, d = x2d.shape
    tm = _tile(rows_per_mod, 512)
    tpm = rows_per_mod // tm
    nb = gate.shape[0]
    return pl.pallas_call(
        _mm_res_kernel,
        grid=(t // tm,),
        in_specs=[pl.BlockSpec((tm, a.shape[1]), lambda i: (i, 0)),
                  pl.BlockSpec(w.shape, lambda i: (0, 0)),
                  pl.BlockSpec((tm, d), lambda i: (i, 0)),
                  pl.BlockSpec((None, 1, d), lambda i: (i // tpm, 0, 0))],
        out_specs=pl.BlockSpec((tm, d), lambda i: (i, 0)),
        out_shape=jax.ShapeDtypeStruct((t, d), F32),
        compiler_params=_params(("parallel",)),
        name="matmul_gated_residual",
    )(a, w, x2d, gate.reshape(nb, 1, d))


def _dft_stage1_kernel(m_ref, z_ref, o_ref):
    z = z_ref[...]
    half = z.shape[1] // 2
    zs = jnp.concatenate([z[:, :half], z[:, half:]], axis=0)
    a = jnp.dot(m_ref[...], zs, preferred_element_type=F32)
    n1 = a.shape[0] // 2
    o_ref[0] = a[:n1].astype(o_ref.dtype)
    o_ref[1] = a[n1:].astype(o_ref.dtype)


def _dft_stage2_kernel(m_ref, a_ref, o_ref):
    o_ref[...] = jnp.dot(m_ref[...], a_ref[...], preferred_element_type=F32).astype(o_ref.dtype)


def _dft_tables(l, d):
    n1 = FOURIER_N1
    n2 = l // n1
    gw = d // FOURIER_GROUPS
    two_pi = 2.0 * math.pi
    ch = jnp.arange(gw, dtype=I32)
    ph = (ch[:, None] * ch[None, :]) % gw
    ang = ph.astype(F32) * (two_pi / gw)
    cc, sc = jnp.cos(ang) * gw ** -0.5, jnp.sin(ang) * gw ** -0.5
    eye = jnp.eye(FOURIER_GROUPS, dtype=F32)
    w_cs = jnp.concatenate([jnp.kron(eye, cc), -jnp.kron(eye, sc)], axis=1)
    k1 = jnp.arange(n1, dtype=I32)
    pos = jnp.arange(n2, dtype=I32)[:, None, None] + n2 * jnp.arange(n1, dtype=I32)[None, None, :]
    ang1 = ((k1[None, :, None] * pos) % l).astype(F32) * (two_pi / l)
    gc, gs = jnp.cos(ang1) * l ** -0.5, jnp.sin(ang1) * l ** -0.5
    m1 = jnp.concatenate([jnp.concatenate([gc, gs], axis=2),
                          jnp.concatenate([-gs, gc], axis=2)], axis=1)
    k2 = jnp.arange(n2, dtype=I32)
    ang2 = ((k2[:, None] * k2[None, :]) % n2).astype(F32) * (two_pi / n2)
    m2 = jnp.concatenate([jnp.cos(ang2), jnp.sin(ang2)], axis=1)
    return w_cs.astype(BF16), m1.astype(BF16), m2.astype(BF16)


def fourier_positions(z, m1, m2, batch):
    t, d2 = z.shape
    d = d2 // 2
    l = t // batch
    n1 = FOURIER_N1
    n2 = l // n1
    a = pl.pallas_call(
        _dft_stage1_kernel,
        grid=(batch, n2),
        in_specs=[pl.BlockSpec((None, 2 * n1, 2 * n1), lambda b, j: (j, 0, 0)),
                  pl.BlockSpec((None, n1, d2), lambda b, j: (b, 0, j))],
        out_specs=pl.BlockSpec((None, 2, None, n1, d), lambda b, j: (b, 0, j, 0, 0)),
        out_shape=jax.ShapeDtypeStruct((batch, 2, n2, n1, d), BF16),
        compiler_params=_params(("parallel", "parallel")),
        name="dft_stage1",
    )(m1, z.reshape(batch, n1, n2 * d2))
    cols = n1 * d
    tn = _tile(cols, 8192)
    f = pl.pallas_call(
        _dft_stage2_kernel,
        grid=(batch, cols // tn),
        in_specs=[pl.BlockSpec((n2, 2 * n2), lambda b, j: (0, 0)),
                  pl.BlockSpec((None, 2 * n2, tn), lambda b, j: (b, 0, j))],
        out_specs=pl.BlockSpec((None, n2, tn), lambda b, j: (b, 0, j)),
        out_shape=jax.ShapeDtypeStruct((batch, n2, cols), BF16),
        compiler_params=_params(("parallel", "parallel")),
        name="dft_stage2",
    )(m2, a.reshape(batch, 2 * n2, cols))
    return f.reshape(t, d)


SUB = 8
ROW_CODE_SHIFT = 14


def _to_row_tiles(ref, val):
    rows = val.shape[0]
    for j in range(SUB):
        ref[pl.ds(j, rows, stride=SUB), :] = val[:, j * LANES:(j + 1) * LANES]


def _from_row_tiles(ref, rows):
    return jnp.concatenate([ref[pl.ds(j, rows, stride=SUB), :] for j in range(SUB)], axis=1)


def _row_tile(ref, row):
    if isinstance(row, int):
        return ref.at[pl.ds(row * SUB, SUB), :]
    return ref.at[pl.ds(pl.multiple_of(row * SUB, SUB), SUB), :]


def _route_kernel(x_ref, g_ref, sh_ref, sc_ref, wr_ref, br_ref, h_ref, ti_ref, tg_ref, rank_ref, cnt_ref, carry):
    @pl.when(pl.program_id(0) == 0)
    def _():
        carry[...] = jnp.zeros(carry.shape, carry.dtype)

    h = _norm_mod(x_ref[...], g_ref[...], sh_ref[...], sc_ref[...])
    _to_row_tiles(h_ref, h)
    logits = jnp.dot(h, wr_ref[...], precision=HIGHEST, preferred_element_type=F32) + br_ref[...]
    lane = lax.broadcasted_iota(I32, logits.shape, 1)
    vals, idxs = [], []
    cur = logits
    for _ in range(TOP_K):
        m = jnp.max(cur, axis=-1, keepdims=True)
        idx = jnp.min(jnp.where(cur == m, lane, LANES), axis=-1, keepdims=True)
        vals.append(m)
        idxs.append(idx)
        cur = jnp.where(lane == idx, -jnp.inf, cur)
    es = [jnp.exp(v - vals[0]) for v in vals]
    tot = es[0] + es[1] + es[2] + es[3]
    ti = jnp.zeros(logits.shape, I32)
    tg = jnp.zeros(logits.shape, F32)
    for k in range(TOP_K):
        ti = jnp.where(lane == k, idxs[k], ti)
        tg = jnp.where(lane == k, es[k] / tot, tg)
    ti_ref[...] = ti
    tg_ref[...] = tg
    tm = logits.shape[0]
    earlier = (lax.broadcasted_iota(I32, (tm, tm), 1) < lax.broadcasted_iota(I32, (tm, tm), 0)).astype(BF16)
    base = carry[...]
    rank = jnp.zeros(logits.shape, I32)
    for k in range(TOP_K):
        hit = lane == idxs[k]
        onehot = jnp.where(hit, 1.0, 0.0)
        before = jnp.dot(earlier, onehot.astype(BF16), preferred_element_type=F32)
        rk = jnp.sum(jnp.where(hit, base + before, 0.0), axis=-1, keepdims=True)
        rank = jnp.where(lane == k, rk.astype(I32), rank)
        base = base + jnp.sum(onehot, axis=0, keepdims=True)
    rank_ref[...] = rank
    carry[...] = base
    cnt_ref[...] = base


def route(x2d, g, shift, scale, rows_per_mod, w_router, b_router):
    t, d = x2d.shape
    assert d == SUB * LANES
    nb = shift.shape[0]
    tm = _tile(rows_per_mod, 512)
    tpm = rows_per_mod // tm
    ne = w_router.shape[1]
    wr = jnp.pad(w_router, ((0, 0), (0, LANES - ne)))
    br = jnp.pad(b_router.reshape(1, ne), ((0, 0), (0, LANES - ne)), constant_values=NEG_BIG)
    return pl.pallas_call(
        _route_kernel,
        grid=(t // tm,),
        in_specs=[pl.BlockSpec((tm, d), lambda i: (i, 0)),
                  pl.BlockSpec((1, d), lambda i: (0, 0)),
                  pl.BlockSpec((None, 1, d), lambda i: (i // tpm, 0, 0)),
                  pl.BlockSpec((None, 1, d), lambda i: (i // tpm, 0, 0)),
                  pl.BlockSpec((d, LANES), lambda i: (0, 0)),
                  pl.BlockSpec((1, LANES), lambda i: (0, 0))],
        out_specs=[pl.BlockSpec((tm * SUB, LANES), lambda i: (i, 0)),
                   pl.BlockSpec((tm, LANES), lambda i: (i, 0)),
                   pl.BlockSpec((tm, LANES), lambda i: (i, 0)),
                   pl.BlockSpec((tm, LANES), lambda i: (i, 0)),
                   pl.BlockSpec((1, LANES), lambda i: (0, 0))],
        out_shape=[jax.ShapeDtypeStruct((t * SUB, LANES), F32), jax.ShapeDtypeStruct((t, LANES), I32),
                   jax.ShapeDtypeStruct((t, LANES), F32), jax.ShapeDtypeStruct((t, LANES), I32),
                   jax.ShapeDtypeStruct((1, LANES), F32)],
        scratch_shapes=[pltpu.VMEM((1, LANES), F32)],
        compiler_params=_params(("arbitrary",)),
        name="moe_route",
    )(x2d, g.reshape(1, d), shift.reshape(nb, 1, d), scale.reshape(nb, 1, d), wr, br)


def _dest_kernel(ti_ref, rank_ref, start_ref, dest_ref):
    ti = ti_ref[...]
    lane = lax.broadcasted_iota(I32, ti.shape, 1)
    dest = jnp.zeros(ti.shape, I32)
    for k in range(TOP_K):
        start = jnp.sum(jnp.where(lane == ti[:, k:k + 1], start_ref[...], 0), axis=-1, keepdims=True)
        dest = jnp.where(lane == k, start + rank_ref[:, k:k + 1], dest)
    dest_ref[...] = dest


def moe_plan(ti, rank, cnt, t):
    tb = MOE_ROWS
    tm = _tile(t, 512)
    counts = cnt[0, :N_EXPERTS].astype(I32)
    padded = (counts + tb - 1) // tb * tb
    pend = jnp.cumsum(padded)
    pstart = jnp.pad(pend - padded, (0, LANES - N_EXPERTS)).reshape(1, LANES)
    dest = pl.pallas_call(
        _dest_kernel,
        grid=(t // tm,),
        in_specs=[pl.BlockSpec((tm, LANES), lambda i: (i, 0)), pl.BlockSpec((tm, LANES), lambda i: (i, 0)),
                  pl.BlockSpec((1, LANES), lambda i: (0, 0))],
        out_specs=pl.BlockSpec((tm, LANES), lambda i: (i, 0)),
        out_shape=jax.ShapeDtypeStruct((t, LANES), I32),
        compiler_params=_params(("parallel",)),
        name="moe_dest",
    )(ti, rank, pstart)
    dest = dest[:, :TOP_K].reshape(-1)
    n_blocks = (t * TOP_K + N_EXPERTS * (tb - 1)) // tb + 1
    assert t <= 1 << ROW_CODE_SHIFT and (TOP_K * t + tb) << ROW_CODE_SHIFT < 1 << 31
    asg = jnp.arange(t * TOP_K, dtype=I32)
    real = (((asg % TOP_K) * t + asg // TOP_K) << ROW_CODE_SHIFT) | (asg // TOP_K)
    spill = (TOP_K * t + jnp.arange(n_blocks * tb, dtype=I32) % tb) << ROW_CODE_SHIFT
    codes = spill.at[dest].set(real)
    starts = jnp.arange(n_blocks, dtype=I32) * tb
    blk_e = jnp.minimum(jnp.sum((pend[None, :] <= starts[:, None]).astype(I32), axis=1), N_EXPERTS - 1)
    nact = (pend[-1] // tb).astype(I32).reshape(1)
    return codes, blk_e, nact


def _expert_kernel(tb, n_tok, be_ref, code_ref, nact_ref, h_hbm, wgu_ref, bgu_ref, wdn_ref, bdn_ref, y_hbm,
                   xbuf, ybuf, wgu_s, wdn_perm, wdn_s, gsem, ssem):
    i = pl.program_id(0)
    n_blocks = pl.num_programs(0)
    nact = nact_ref[0]
    de = wdn_ref.shape[0]
    slot = i % 2
    other = 1 - slot

    def start_gather(base, r, dst_slot, priority=0):
        tok = code_ref[base + r] & ((1 << ROW_CODE_SHIFT) - 1)
        pltpu.make_async_copy(_row_tile(h_hbm, tok), _row_tile(xbuf.at[dst_slot], r),
                              gsem.at[dst_slot]).start(priority)

    def start_scatter(base, r, src_slot, priority=0):
        row = code_ref[base + r] >> ROW_CODE_SHIFT
        pltpu.make_async_copy(_row_tile(ybuf.at[src_slot], r), _row_tile(y_hbm, row),
                              ssem.at[src_slot]).start(priority)

    def wait_gather(s):
        pltpu.make_async_copy(h_hbm.at[pl.ds(0, tb * SUB), :], xbuf.at[s], gsem.at[s]).wait()

    def wait_scatter(s):
        pltpu.make_async_copy(ybuf.at[s], y_hbm.at[pl.ds(0, tb * SUB), :], ssem.at[s]).wait()

    @pl.when(i == 0)
    def _():
        ybuf[...] = jnp.zeros(ybuf.shape, ybuf.dtype)

        def first(r, carry):
            start_gather(0, r, 0)
            return carry
        lax.fori_loop(0, tb, first, 0, unroll=8)

    @pl.when((i < nact) & ((i == 0) | (be_ref[i] != be_ref[jnp.maximum(i - 1, 0)])))
    def _():
        wgu_s[...] = wgu_ref[...].astype(BF16)
        half = LANES // 2
        for c in range(wdn_perm.shape[0]):
            for m in range(0, de, LANES):
                wdn_perm[c, pl.ds(m, half, stride=2), :] = wdn_ref[m:m + half, c * LANES:(c + 1) * LANES]
                wdn_perm[c, pl.ds(m + 1, half, stride=2), :] = wdn_ref[m + half:m + LANES, c * LANES:(c + 1) * LANES]
            wdn_s[:, c * LANES:(c + 1) * LANES] = wdn_perm[c].astype(BF16)

    def start_copies(slot, other):
        wait_gather(slot)

        @pl.when(i >= 1)
        def _():
            wait_scatter(slot)

        prev_base = jnp.where(i == 0, n_blocks - 1, i - 1) * tb
        next_base = (i + 1) * tb
        for r in range(tb):
            start_gather(next_base, r, other, r % 2)
            start_scatter(prev_base, r, other, (r + 1) % 2)

    def run_block(slot):
        x = _from_row_tiles(xbuf.at[slot], tb).astype(BF16)
        even = lax.broadcasted_iota(I32, (tb, LANES), 1) % 2 == 0
        cw = 2 * LANES
        n_chunks = wgu_s.shape[1] // cw
        parts = []
        for c in range(n_chunks):
            gu = jnp.dot(x, wgu_s[:, c * cw:(c + 1) * cw], preferred_element_type=F32) + bgu_ref[:, c * cw:(c + 1) * cw]
            acts = []
            for c0 in range(0, cw, LANES):
                v = gu[:, c0:c0 + LANES]
                glu = jnp.minimum(v, SWIGLU_LIMIT)
                lin = jnp.clip(v, -SWIGLU_LIMIT, SWIGLU_LIMIT) + 1.0
                acts.append(glu * jax.nn.sigmoid(SWIGLU_ALPHA * glu) * pltpu.roll(lin, LANES - 1, 1))
            parts.append(jnp.where(even, acts[0], pltpu.roll(acts[1], 1, 1)).astype(BF16))
        y = jnp.dot(jnp.concatenate(parts, axis=1), wdn_s[...], preferred_element_type=F32) + bdn_ref[...]
        _to_row_tiles(ybuf.at[slot], y)

    for parity in range(2):
        pl.when((i < nact) & (slot == parity))(functools.partial(start_copies, parity, 1 - parity))
    for parity in range(2):
        pl.when((i < nact) & (slot == parity))(functools.partial(run_block, parity))

    @pl.when(i == nact)
    def _():
        wait_gather(slot)
        wait_scatter(slot)

        def last(r, carry):
            start_scatter((i - 1) * tb, r, other)
            return carry
        lax.fori_loop(0, tb, last, 0, unroll=8)
        wait_scatter(other)


def expert_ffn(h_rt, codes, blk_e, nact, layer, w_gu, b_gu, w_dn, b_dn):
    r = codes.shape[0]
    n_tok = h_rt.shape[0] // SUB
    tb = MOE_ROWS
    _, ne, d, de2 = w_gu.shape
    de = de2 // 2
    b_gu = b_gu.reshape(-1, ne, 1, de2)
    b_dn = b_dn.reshape(-1, ne, 1, d)
    return pl.pallas_call(
        functools.partial(_expert_kernel, tb, n_tok),
        grid_spec=pltpu.PrefetchScalarGridSpec(
            num_scalar_prefetch=3, grid=(r // tb,),
            in_specs=[pl.BlockSpec(memory_space=pl.ANY),
                      pl.BlockSpec((None, None, d, de2), lambda i, be, cd, na: (layer, be[i], 0, 0)),
                      pl.BlockSpec((None, None, 1, de2), lambda i, be, cd, na: (layer, be[i], 0, 0)),
                      pl.BlockSpec((None, None, de, d), lambda i, be, cd, na: (layer, be[i], 0, 0)),
                      pl.BlockSpec((None, None, 1, d), lambda i, be, cd, na: (layer, be[i], 0, 0))],
            out_specs=pl.BlockSpec(memory_space=pl.ANY),
            scratch_shapes=[pltpu.VMEM((2, tb * SUB, LANES), F32), pltpu.VMEM((2, tb * SUB, LANES), F32),
                            pltpu.VMEM((d, de2), BF16), pltpu.VMEM((d // LANES, de, LANES), F32),
                            pltpu.VMEM((de, d), BF16),
                            pltpu.SemaphoreType.DMA((2,)), pltpu.SemaphoreType.DMA((2,))]),
        out_shape=jax.ShapeDtypeStruct(((TOP_K * n_tok + tb) * SUB, LANES), F32),
        compiler_params=_params(("arbitrary",), 58),
        name="moe_experts",
    )(blk_e, codes, nact, h_rt, w_gu, b_gu, w_dn, b_dn)


def _combine_kernel(y0_ref, y1_ref, y2_ref, y3_ref, x_ref, gate_ref, tg_ref, o_ref):
    tg = tg_ref[...]
    tm = x_ref.shape[0]
    ys = [_from_row_tiles(y_ref, tm) * tg[:, k:k + 1] for k, y_ref in enumerate((y0_ref, y1_ref, y2_ref, y3_ref))]
    o_ref[...] = x_ref[...] + gate_ref[...] * ((ys[0] + ys[1]) + (ys[2] + ys[3]))


def moe_combine(y4, tg, x2d, gate, rows_per_mod):
    t, d = x2d.shape
    nb = gate.shape[0]
    tm = _tile(rows_per_mod, MOE_ROWS)
    tpm = rows_per_mod // tm
    nt = t // tm
    planes = [pl.BlockSpec((tm * SUB, LANES), functools.partial(lambda k, i: (k * nt + i, 0), k))
              for k in range(TOP_K)]
    return pl.pallas_call(
        _combine_kernel,
        grid=(nt,),
        in_specs=planes + [
                  pl.BlockSpec((tm, d), lambda i: (i, 0)),
                  pl.BlockSpec((None, 1, d), lambda i: (i // tpm, 0, 0)),
                  pl.BlockSpec((tm, LANES), lambda i: (i, 0))],
        out_specs=pl.BlockSpec((tm, d), lambda i: (i, 0)),
        out_shape=jax.ShapeDtypeStruct((t, d), F32),
        compiler_params=_params(("parallel",)),
        name="moe_combine",
    )(y4, y4, y4, y4, x2d, gate.reshape(nb, 1, d), tg)


def moe_layer(x2d, norm_g, shift, scale, gate, rows_per_mod, w_router, b_router, layer, w_gu, b_gu, w_dn, b_dn):
    t = x2d.shape[0]
    h_rt, ti, tg, rank, cnt = route(x2d, norm_g, shift, scale, rows_per_mod, w_router, b_router)
    codes, blk_e, nact = moe_plan(ti, rank, cnt, t)
    y4 = expert_ffn(h_rt, codes, blk_e, nact, layer, w_gu, b_gu, w_dn, b_dn)
    return moe_combine(y4, tg, x2d, gate, rows_per_mod)


def _rope_tables(s):
    n_freq = DA_HEAD_DIM // 4
    freqs = ROPE_BASE ** (-jnp.arange(n_freq, dtype=F32) / n_freq)
    pos = jnp.arange(s, dtype=I32)
    ang_r = (pos // GRID_W).astype(F32)[:, None] * freqs
    ang_c = (pos % GRID_W).astype(F32)[:, None] * freqs
    ang = jnp.concatenate([ang_r, ang_r, ang_c, ang_c], axis=-1)
    ang = jnp.concatenate([ang, ang], axis=-1)
    return jnp.cos(ang), jnp.sin(ang)


def _even_layer(x2d, ctx2d, mods, norm_g1, w_in, w_out, q_norm_g, k_norm_g, da_lambda, da_subln_g,
                conv_xb_w, conv_xb_b, conv_c_w, conv_c_b, dt_bias, a_log, d_skip, ssm_norm_g,
                lam_init, batch):
    sx1, cx1, gx1, sc1, cc1 = mods
    t, d = x2d.shape
    s = t // batch
    ctx_len = ctx2d.shape[0] // batch
    qkw = DA_HEADS * 2 * DA_HEAD_DIM
    vw = DA_HEADS * DA_V_DIM
    col_q, col_z = 0, qkw
    col_c = col_z + SSM_D_INNER
    col_k = col_c + SSM_BC_W
    col_v = col_k + qkw
    col_xb = col_v + vw
    col_dt = col_xb + SSM_XB_W
    wb = w_in.astype(BF16)
    w_q, w_k, w_v = wb[:, col_q:col_q + qkw], wb[:, col_k:col_k + qkw], wb[:, col_v:col_v + vw]
    w_z = wb[:, col_z:col_z + SSM_D_INNER]
    w_xbc = jnp.concatenate([wb[:, col_xb:col_xb + SSM_XB_W], wb[:, col_c:col_c + SSM_BC_W]], axis=1)
    w_dt = jnp.pad(wb[:, col_dt:col_dt + 2 * SSM_HEADS], ((0, 0), (0, LANES - 2 * SSM_HEADS)))
    weights = [w_q, w_k, w_v, w_z, w_xbc, w_dt]
    dts = [BF16, BF16, BF16, BF16, BF16, F32]
    q_u, k_u, v_x, z_x, xbc_u, dt_x = nm_panel(x2d, norm_g1, sx1, cx1, s, weights, dts)
    _, kc_u, v_c, _, xbc_cu, dt_c = nm_panel(ctx2d, norm_g1, sc1, cc1, batch * ctx_len, weights, dts)

    lam = (jnp.exp(jnp.sum(da_lambda[0] * da_lambda[1])) -
           jnp.exp(jnp.sum(da_lambda[2] * da_lambda[3]))).astype(F32) + lam_init
    a_neg = -jnp.exp(a_log.astype(F32)).reshape(-1)

    cos_t, sin_t = _rope_tables(s)
    q = qk_prep(q_u, q_norm_g, cos_t, sin_t, s, True, DA_HEAD_DIM ** -0.5 * LOG2E)
    score_bound = (1.01 * DA_HEAD_DIM ** 0.5 * LOG2E) * jnp.max(jnp.abs(q_norm_g)) * jnp.max(jnp.abs(k_norm_g))
    k_x = qk_prep(k_u, k_norm_g, cos_t, sin_t, s, True, 1.0)
    k_c = qk_prep(kc_u, k_norm_g, cos_t, sin_t, ctx_len, False, 1.0)
    k_all = jnp.concatenate([k_x.reshape(batch, s, qkw), k_c.reshape(batch, ctx_len, qkw)], axis=1)
    v_all = jnp.concatenate([v_x.reshape(batch, s, vw), v_c.reshape(batch, ctx_len, vw)], axis=1)
    o_att = diff_attention(q, k_all, v_all, lam, score_bound, da_subln_g, lam_init, batch)

    conv_w = jnp.concatenate([conv_xb_w, conv_c_w], axis=1)
    conv_b = jnp.concatenate([conv_xb_b, conv_c_b], axis=0)
    xbc_x = conv_silu(xbc_u, conv_w, conv_b, s)
    xbc_c = conv_silu(xbc_cu, conv_w, conv_b, ctx_len)
    hshape = (batch, SSM_GROUPS, SSM_STATE, SSM_HPG * SSM_HEAD_DIM)
    zero_h = jnp.zeros(hshape, F32)
    dt_bias_f = dt_bias.reshape(-1).astype(F32)
    _, _, h_f, h_b = ssd_bidir(xbc_c, dt_c, dt_bias_f, a_neg, d_skip, zero_h, zero_h, batch)
    y_f, y_b, _, _ = ssd_bidir(xbc_x, dt_x, dt_bias_f, a_neg, d_skip, h_f, h_b, batch)
    wo = w_out.astype(BF16)
    return out_proj(o_att, y_f, y_b, z_x, ssm_norm_g, wo[:vw], wo[vw:], x2d, gx1, s)


def _odd_layer(x2d, mods, norm_g1, w_fourier, batch):
    sx1, cx1, gx1 = mods
    t, d = x2d.shape
    s = t // batch
    w_cs, m1, m2 = _dft_tables(s, d)
    (z,) = nm_panel(x2d, norm_g1, sx1, cx1, s, [w_cs], [BF16])
    f = fourier_positions(z, m1, m2, batch)
    return matmul_residual(f, w_fourier.astype(BF16), x2d, gx1, s)


def kernel(x, c, ctx, c_ctx, ada_w, ada_b, norm_g, w_in, w_out, q_norm_g, k_norm_g, da_lambda, da_subln_g, conv_xb_w, conv_xb_b, conv_c_w, conv_c_b, dt_bias, a_log, d_skip, ssm_norm_g, w_fourier, w_router, b_router, w_gate_up, b_gate_up, w_down, b_down):
    b, s, d = x.shape
    depth = ada_w.shape[0]
    assert b + 1 <= 8
    cvecs = jnp.concatenate([c, c_ctx[None, :], jnp.zeros((8 - b - 1, d), F32)], axis=0)
    mods = ada_all(cvecs, ada_w, ada_b)
    x2d = x.reshape(b * s, d)
    ctx2d = ctx.reshape(-1, d)
    for i in range(depth):
        m = mods[i].reshape(8, 6, d)
        sx1, cx1, gx1, sx2, cx2, gx2 = [m[:b, j] for j in range(6)]
        if i % 2 == 0:
            e = i // 2
            lam_init = 0.8 - 0.6 * math.exp(-0.3 * i)
            if any(j % 2 == 0 for j in range(i + 1, depth)):
                raise NotImplementedError("context stream output is only needed for depth > 2")
            sc1, cc1 = m[b:b + 1, 0], m[b:b + 1, 1]
            x2d = _even_layer(x2d, ctx2d, (sx1, cx1, gx1, sc1, cc1), norm_g[i, 0], w_in[e], w_out[e],
                              q_norm_g[e], k_norm_g[e], da_lambda[e], da_subln_g[e], conv_xb_w[e],
                              conv_xb_b[e], conv_c_w[e], conv_c_b[e], dt_bias[e], a_log[e], d_skip[e],
                              ssm_norm_g[e], lam_init, b)
        else:
            x2d = _odd_layer(x2d, (sx1, cx1, gx1), norm_g[i, 0], w_fourier[i // 2], b)
        x2d = moe_layer(x2d, norm_g[i, 1], sx2, cx2, gx2, s, w_router[i], b_router[i],
                        i, w_gate_up, b_gate_up, w_down, b_down)
    return x2d.reshape(b, s, d)
```

```python
import functools
import math

import jax
import jax.numpy as jnp
from jax import lax
from jax.experimental import pallas as pl
from jax.experimental.pallas import tpu as pltpu

F32 = jnp.float32
BF16 = jnp.bfloat16
I32 = jnp.int32

RMS_EPS = 1e-6
GRID_W = 64
ROPE_BASE = 10000.0
DA_HEADS = 8
DA_HEAD_DIM = 64
DA_V_DIM = 128
SSM_HEADS = 16
SSM_HEAD_DIM = 64
SSM_GROUPS = 2
SSM_HPG = 8
SSM_STATE = 128
SSM_CONV = 5
SSM_CHUNK = 128
SSM_D_INNER = 1024
SSM_BC_W = 256
SSM_XB_W = 1280
FOURIER_GROUPS = 4
FOURIER_N1 = 128
N_EXPERTS = 32
TOP_K = 4
SWIGLU_LIMIT = 7.0
SWIGLU_ALPHA = 1.702
MOE_ROWS = 256
LANES = 128
NEG_BIG = -1e30
MIB = 1024 * 1024
HIGHEST = lax.Precision.HIGHEST


def _params(sem, vmem_mib=48):
    return pltpu.CompilerParams(dimension_semantics=sem, vmem_limit_bytes=vmem_mib * MIB)


def _tile(n, pref):
    t = min(n, pref)
    while n % t:
        t //= 2
    return t


def _silu(v):
    return v * jax.nn.sigmoid(v)


def _norm_mod(xf, g, shift, scale):
    r = lax.rsqrt(jnp.mean(xf * xf, axis=-1, keepdims=True) + RMS_EPS)
    return ((xf * r) * g) * (1.0 + scale) + shift


def _ada_kernel(a_ref, w_ref, b_ref, o_ref):
    s = _silu(a_ref[...])
    o_ref[...] = jnp.dot(s.astype(BF16), w_ref[...].astype(BF16),
                         preferred_element_type=F32) + b_ref[...]


def ada_all(cvecs, ada_w, ada_b):
    depth, d, n = ada_w.shape
    tn = _tile(n, 1536)
    return pl.pallas_call(
        _ada_kernel,
        grid=(depth, n // tn),
        in_specs=[pl.BlockSpec((8, d), lambda l, j: (0, 0)),
                  pl.BlockSpec((None, d, tn), lambda l, j: (l, 0, j)),
                  pl.BlockSpec((None, 1, tn), lambda l, j: (l, 0, j))],
        out_specs=pl.BlockSpec((None, 8, tn), lambda l, j: (l, 0, j)),
        out_shape=jax.ShapeDtypeStruct((depth, 8, n), F32),
        compiler_params=_params(("parallel", "parallel")),
        name="ada_mod",
    )(cvecs, ada_w, ada_b.reshape(depth, 1, n))


def _nm_panel_kernel(n_w, x_ref, g_ref, sh_ref, sc_ref, *refs):
    w_refs, o_refs = refs[:n_w], refs[n_w:2 * n_w]
    hb = _norm_mod(x_ref[...], g_ref[...], sh_ref[...], sc_ref[...]).astype(BF16)
    for w_ref, o_ref in zip(w_refs, o_refs):
        n = w_ref.shape[1]
        cw = _tile(n, 512)
        for c0 in range(0, n, cw):
            o_ref[:, c0:c0 + cw] = jnp.dot(hb, w_ref[:, c0:c0 + cw],
                                           preferred_element_type=F32).astype(o_ref.dtype)


def nm_panel(x2d, g, shift, scale, rows_per_mod, weights, out_dtypes, tm_pref=256):
    t, d = x2d.shape
    nb = shift.shape[0]
    tm = _tile(rows_per_mod, tm_pref)
    tpm = rows_per_mod // tm
    n_w = len(weights)
    in_specs = [pl.BlockSpec((tm, d), lambda i: (i, 0)),
                pl.BlockSpec((1, d), lambda i: (0, 0)),
                pl.BlockSpec((None, 1, d), lambda i: (i // tpm, 0, 0)),
                pl.BlockSpec((None, 1, d), lambda i: (i // tpm, 0, 0))]
    in_specs += [pl.BlockSpec(w.shape, lambda i: (0, 0)) for w in weights]
    out_specs = [pl.BlockSpec((tm, w.shape[1]), lambda i: (i, 0)) for w in weights]
    out_shape = [jax.ShapeDtypeStruct((t, w.shape[1]), dt) for w, dt in zip(weights, out_dtypes)]
    return pl.pallas_call(
        functools.partial(_nm_panel_kernel, n_w),
        grid=(t // tm,),
        in_specs=in_specs, out_specs=out_specs, out_shape=out_shape,
        compiler_params=_params(("parallel",), 56),
        name="norm_mod_proj",
    )(x2d, g.reshape(1, d), shift.reshape(nb, 1, d), scale.reshape(nb, 1, d), *weights)


def _qk_prep_kernel(rope, out_scale, u_ref, g_ref, seg_ref, cos_ref, sin_ref, o_ref):
    seg = seg_ref[...]
    g = g_ref[...]
    n_heads = u_ref.shape[1] // LANES
    for c in range(n_heads):
        u = u_ref[:, c * LANES:(c + 1) * LANES].astype(F32)
        u2 = u * u
        hi = u2.astype(BF16)
        lo = (u2 - hi.astype(F32)).astype(BF16)
        ss = (jnp.dot(hi, seg, preferred_element_type=F32)
              + jnp.dot(lo, seg, preferred_element_type=F32))
        nrm = (u * lax.rsqrt(ss * (1.0 / DA_HEAD_DIM) + RMS_EPS)) * g
        if rope:
            lane = lax.broadcasted_iota(I32, nrm.shape, 1)
            first = (lane % 32) < 16
            rot = jnp.where(first, -pltpu.roll(nrm, LANES - 16, 1), pltpu.roll(nrm, 16, 1))
            nrm = nrm * cos_ref[...] + rot * sin_ref[...]
        o_ref[:, c * LANES:(c + 1) * LANES] = (nrm * out_scale).astype(o_ref.dtype)


def qk_prep(u, gain, cos_t, sin_t, seq, rope, out_scale):
    t, w = u.shape
    tm = _tile(seq, 256)
    tps = seq // tm
    seg = (jnp.arange(LANES)[:, None] // DA_HEAD_DIM == jnp.arange(LANES)[None, :] // DA_HEAD_DIM)
    return pl.pallas_call(
        functools.partial(_qk_prep_kernel, rope, out_scale),
        grid=(t // tm,),
        in_specs=[pl.BlockSpec((tm, w), lambda i: (i, 0)),
                  pl.BlockSpec((1, LANES), lambda i: (0, 0)),
                  pl.BlockSpec((LANES, LANES), lambda i: (0, 0)),
                  pl.BlockSpec((tm, LANES), lambda i: (i % tps, 0)),
                  pl.BlockSpec((tm, LANES), lambda i: (i % tps, 0))],
        out_specs=pl.BlockSpec((tm, w), lambda i: (i, 0)),
        out_shape=jax.ShapeDtypeStruct((t, w), BF16),
        compiler_params=_params(("parallel",)),
        name="qk_norm_rope",
    )(u, jnp.tile(gain, 2).reshape(1, LANES), seg.astype(BF16), cos_t, sin_t)


LOG2E = math.log2(math.e)
ATTN_SHIFT_LIMIT = 60.0


def _split_components(q):
    lane = lax.broadcasted_iota(I32, q.shape, 1)
    zero = jnp.zeros_like(q)
    return jnp.where(lane < DA_HEAD_DIM, q, zero), jnp.where(lane >= DA_HEAD_DIM, q, zero)


def _attn_finish(n0, l0, n1, l1, lam, g, out_mult, o_ref):
    o = n0 / l0 - lam * (n1 / l1)
    o = (o * lax.rsqrt(jnp.mean(o * o, axis=-1, keepdims=True) + RMS_EPS)) * g
    o_ref[...] = (o * out_mult).astype(o_ref.dtype)


def _attn_online_kernel(tk, out_mult, sc_ref, q_ref, k_ref, v_ref, g_ref, o_ref):
    qs = _split_components(q_ref[...])
    tq = q_ref.shape[0]
    n_kv = k_ref.shape[0] // tk

    def body(i, carry):
        off = pl.multiple_of(i * tk, tk)
        k = k_ref[pl.ds(off, tk), :]
        v = v_ref[pl.ds(off, tk), :]
        new = []
        for c in range(2):
            m, l, acc = carry[c]
            s = lax.dot_general(qs[c], k, (((1,), (1,)), ((), ())), preferred_element_type=F32)
            m_new = jnp.maximum(m, jnp.max(s, axis=-1, keepdims=True))
            alpha = jnp.exp2(m - m_new)
            p = jnp.exp2(s - m_new)
            l_new = alpha * l + jnp.sum(p, axis=-1, keepdims=True)
            acc_new = alpha * acc + jnp.dot(p.astype(BF16), v, preferred_element_type=F32)
            new.append((m_new, l_new, acc_new))
        return tuple(new)

    init = tuple((jnp.full((tq, 1), -jnp.inf, F32), jnp.zeros((tq, 1), F32),
                  jnp.zeros((tq, DA_V_DIM), F32)) for _ in range(2))
    (_, l0, a0), (_, l1, a1) = lax.fori_loop(0, n_kv, body, init)
    _attn_finish(a0, l0, a1, l1, sc_ref[0], g_ref[...], out_mult, o_ref)


def _attn_shift_kernel(tk, out_mult, sc_ref, q_ref, k_ref, v_ref, g_ref, o_ref):
    qs = _split_components(q_ref[...])
    tq = q_ref.shape[0]
    n_kv = k_ref.shape[0] // tk
    shift = sc_ref[1]
    ones = jnp.ones((tk, LANES), BF16)

    sub = 2 * LANES

    def body(i, acc):
        acc = list(acc)
        for j in range(tk // sub):
            off = pl.multiple_of(i * tk + j * sub, sub)
            k = k_ref[pl.ds(off, sub), :]
            va = jnp.concatenate([v_ref[pl.ds(off, sub), :], ones[:sub]], axis=1)
            for c in range(2):
                s = lax.dot_general(qs[c], k, (((1,), (1,)), ((), ())), preferred_element_type=F32)
                acc[c] = acc[c] + jnp.dot(jnp.exp2(s - shift).astype(BF16), va, preferred_element_type=F32)
        return tuple(acc)

    zero = jnp.zeros((tq, 2 * LANES), F32)
    a0, a1 = lax.fori_loop(0, n_kv, body, (zero, zero))
    _attn_finish(a0[:, :LANES], a0[:, LANES:], a1[:, :LANES], a1[:, LANES:], sc_ref[0], g_ref[...], out_mult,
                 o_ref)


def diff_attention(q, k_all, v_all, lam, score_bound, subln_g, lam_init, batch):
    t, w = q.shape
    s = t // batch
    sk = k_all.shape[1]
    tq = _tile(s, 1024)
    tk = 2816 if sk % 2816 == 0 else _tile(sk, 256)
    nq = s // tq
    scalars = jnp.stack([lam, score_bound]).astype(F32)

    def run(body, tk):
        return pl.pallas_call(
            functools.partial(body, tk, 1.0 - lam_init),
            grid=(batch, DA_HEADS, nq),
            in_specs=[pl.BlockSpec(memory_space=pltpu.SMEM),
                      pl.BlockSpec((tq, LANES), lambda b, h, i: (b * nq + i, h)),
                      pl.BlockSpec((None, sk, LANES), lambda b, h, i: (b, 0, h)),
                      pl.BlockSpec((None, sk, LANES), lambda b, h, i: (b, 0, h)),
                      pl.BlockSpec((1, LANES), lambda b, h, i: (0, 0))],
            out_specs=pl.BlockSpec((tq, LANES), lambda b, h, i: (b * nq + i, h)),
            out_shape=jax.ShapeDtypeStruct((t, w), BF16),
            compiler_params=_params(("parallel", "parallel", "parallel")),
            name="diff_attention",
        )(scalars, q, k_all, v_all, subln_g.reshape(1, LANES))

    return lax.cond(score_bound <= ATTN_SHIFT_LIMIT,
                    lambda: run(_attn_shift_kernel, sk), lambda: run(_attn_online_kernel, tk))


HALO = 16


def _conv_kernel(tps, p_ref, c_ref, n_ref, w_ref, b_ref, o_ref, ext_ref):
    t = pl.program_id(0) % tps
    tm = c_ref.shape[0]
    prev = p_ref[...].astype(F32)
    nxt = n_ref[...].astype(F32)
    ext_ref[0:HALO, :] = jnp.where(t == 0, 0.0, prev)
    ext_ref[HALO:HALO + tm, :] = c_ref[...].astype(F32)
    ext_ref[HALO + tm:2 * HALO + tm, :] = jnp.where(t == tps - 1, 0.0, nxt)
    width = c_ref.shape[1]
    cw = _tile(width, 256)
    pad = (SSM_CONV - 1) // 2
    for c0 in range(0, width, cw):
        acc = jnp.broadcast_to(b_ref[:, c0:c0 + cw], (tm, cw))
        for k in range(SSM_CONV):
            acc = acc + w_ref[k:k + 1, c0:c0 + cw] * ext_ref[HALO - pad + k:HALO - pad + k + tm, c0:c0 + cw]
        o_ref[:, c0:c0 + cw] = _silu(acc).astype(o_ref.dtype)


def conv_silu(u, w, b, seq):
    t, c = u.shape
    tm = _tile(seq, 512)
    tps = seq // tm
    hb = tm // HALO
    last = t // HALO - 1
    return pl.pallas_call(
        functools.partial(_conv_kernel, tps),
        grid=(t // tm,),
        in_specs=[pl.BlockSpec((HALO, c), lambda i: (jnp.maximum(i * hb - 1, 0), 0)),
                  pl.BlockSpec((tm, c), lambda i: (i, 0)),
                  pl.BlockSpec((HALO, c), lambda i: (jnp.minimum((i + 1) * hb, last), 0)),
                  pl.BlockSpec((SSM_CONV, c), lambda i: (0, 0)),
                  pl.BlockSpec((1, c), lambda i: (0, 0))],
        out_specs=pl.BlockSpec((tm, c), lambda i: (i, 0)),
        out_shape=jax.ShapeDtypeStruct((t, c), BF16),
        scratch_shapes=[pltpu.VMEM((tm + 2 * HALO, c), F32)],
        compiler_params=_params(("parallel",)),
        name="dwconv_silu",
    )(u, u, u, w, b.reshape(1, c))


def _softplus(v):
    return jnp.maximum(v, 0.0) + jnp.log1p(jnp.exp(-jnp.abs(v)))


def _pair(lane_lo, a, b):
    return jnp.where(lane_lo, a, b)


def _ssd_kernel(xf_ref, dtf_ref, dtTf_ref, xb_ref, dtb_ref, dtTb_ref, bias_r, a_r, bias_c, a_c,
                dsk_ref, h0f_ref, h0b_ref, yf_ref, yb_ref, hfo_ref, hbo_ref, hf_s, hb_s):
    c = pl.program_id(1)
    q = SSM_CHUNK
    nh = SSM_HEADS

    @pl.when(c == 0)
    def _():
        hf_s[...] = h0f_ref[...]
        hb_s[...] = h0b_ref[...]

    li = lax.broadcasted_iota(I32, (q, q), 0)
    si = lax.broadcasted_iota(I32, (q, q), 1)
    lower = li >= si
    upper = li <= si
    tri_l = lower.astype(F32)
    tri_u = upper.astype(F32)
    lane_lo = lax.broadcasted_iota(I32, (q, LANES), 1) < SSM_HEAD_DIM
    lane_lo1 = lax.broadcasted_iota(I32, (1, LANES), 1) < SSM_HEAD_DIM

    def bc(col):
        return jnp.broadcast_to(col, (q, LANES))

    def state_update(h_s, g, xs32, bg, wq, edge, base):
        parts, decs = [], []
        for pr in range(SSM_HPG // 2):
            h0 = g * SSM_HPG + 2 * pr
            wp = _pair(lane_lo, bc(wq[:, base + h0:base + h0 + 1]), bc(wq[:, base + h0 + 1:base + h0 + 2]))
            parts.append((xs32[:, h0 * 64:h0 * 64 + LANES] * wp).astype(BF16))
            decs.append(_pair(lane_lo1, jnp.broadcast_to(edge[:, base + h0:base + h0 + 1], (1, LANES)),
                              jnp.broadcast_to(edge[:, base + h0 + 1:base + h0 + 2], (1, LANES))))
        xw = jnp.concatenate(parts, axis=1)
        dec = jnp.exp(jnp.concatenate(decs, axis=1))
        upd = lax.dot_general(bg, xw, (((0,), (0,)), ((), ())), preferred_element_type=F32)
        h_s[g] = h_s[g] * dec + upd

    xbc = xf_ref[...]
    xs_b = xbc[:, :SSM_D_INNER]
    xs32 = xs_b.astype(F32)
    dt = _softplus(dtf_ref[...] + bias_r[...])
    da = dt * a_r[...]
    acs = jnp.dot(tri_l, da, precision=HIGHEST, preferred_element_type=F32)
    racs = jnp.dot(tri_u, da, precision=HIGHEST, preferred_element_type=F32)
    dt_t = _softplus(dtTf_ref[...] + bias_c[...])
    da_t = dt_t * a_c[...]
    acs_t = jnp.dot(da_t, tri_u, precision=HIGHEST, preferred_element_type=F32)
    racs_t = jnp.dot(da_t, tri_l, precision=HIGHEST, preferred_element_type=F32)
    last = acs[q - 1:q, :]
    wq = jnp.exp(last - acs) * dt
    for g in range(SSM_GROUPS):
        bg = xbc[:, SSM_D_INNER + g * SSM_STATE:SSM_D_INNER + (g + 1) * SSM_STATE]
        cg = xbc[:, SSM_D_INNER + SSM_BC_W + g * SSM_STATE:SSM_D_INNER + SSM_BC_W + (g + 1) * SSM_STATE]
        cb = lax.dot_general(cg, bg, (((1,), (1,)), ((), ())), preferred_element_type=F32)
        yoff = jnp.dot(cg, hf_s[g].astype(BF16), preferred_element_type=F32)
        for pr in range(SSM_HPG // 2):
            h0 = g * SSM_HPG + 2 * pr
            col0 = h0 * SSM_HEAD_DIM
            xp = xs_b[:, col0:col0 + LANES]
            ys = []
            for hh in (h0, h0 + 1):
                mf = jnp.exp(jnp.where(lower, acs[:, hh:hh + 1] - acs_t[hh:hh + 1, :], NEG_BIG)) \
                    * dt_t[hh:hh + 1, :]
                mb = jnp.exp(jnp.where(upper, racs[:, nh + hh:nh + hh + 1] - racs_t[nh + hh:nh + hh + 1, :],
                                       NEG_BIG)) * dt_t[nh + hh:nh + hh + 1, :]
                ys.append(jnp.dot((cb * (mf + mb)).astype(BF16), xp, preferred_element_type=F32))
            ef = jnp.exp(_pair(lane_lo, bc(acs[:, h0:h0 + 1]), bc(acs[:, h0 + 1:h0 + 2])))
            yf_ref[:, col0:col0 + LANES] = (_pair(lane_lo, ys[0], ys[1])
                                            + yoff[:, 2 * pr * 64:2 * pr * 64 + LANES] * ef
                                            + xs32[:, col0:col0 + LANES] * dsk_ref[:, col0:col0 + LANES])
        state_update(hf_s, g, xs32, bg, wq, last, 0)

    xbc2 = xb_ref[...]
    xs2 = xbc2[:, :SSM_D_INNER].astype(F32)
    dt2 = _softplus(dtb_ref[...] + bias_r[...])
    racs2 = jnp.dot(tri_u, dt2 * a_r[...], precision=HIGHEST, preferred_element_type=F32)
    first = racs2[0:1, :]
    wq2 = jnp.exp(first - racs2) * dt2
    for g in range(SSM_GROUPS):
        bg = xbc2[:, SSM_D_INNER + g * SSM_STATE:SSM_D_INNER + (g + 1) * SSM_STATE]
        cg = xbc2[:, SSM_D_INNER + SSM_BC_W + g * SSM_STATE:SSM_D_INNER + SSM_BC_W + (g + 1) * SSM_STATE]
        yoff = jnp.dot(cg, hb_s[g].astype(BF16), preferred_element_type=F32)
        for pr in range(SSM_HPG // 2):
            h0 = g * SSM_HPG + 2 * pr
            col0 = h0 * SSM_HEAD_DIM
            eb = jnp.exp(_pair(lane_lo, bc(racs2[:, nh + h0:nh + h0 + 1]), bc(racs2[:, nh + h0 + 1:nh + h0 + 2])))
            yb_ref[:, col0:col0 + LANES] = yoff[:, 2 * pr * 64:2 * pr * 64 + LANES] * eb
        state_update(hb_s, g, xs2, bg, wq2, first, nh)

    @pl.when(c == pl.num_programs(1) - 1)
    def _():
        hfo_ref[...] = hf_s[...]
        hbo_ref[...] = hb_s[...]


def ssd_bidir(xbc, dt_raw, dt_bias, a_neg, d_skip, h0f, h0b, batch):
    t = xbc.shape[0]
    l = t // batch
    q = SSM_CHUNK
    nc = l // q
    w = xbc.shape[1]
    dt_t = jnp.transpose(dt_raw[:, :2 * SSM_HEADS].reshape(batch, l, 2 * SSM_HEADS), (0, 2, 1))
    pad = LANES - 2 * SSM_HEADS
    bias_r = jnp.pad(dt_bias.reshape(1, -1), ((0, 0), (0, pad)))
    a_r = jnp.pad(a_neg.reshape(1, -1), ((0, 0), (0, pad)))
    dsk = jnp.repeat(d_skip, SSM_HEAD_DIM).reshape(1, SSM_D_INNER)
    hshape = (batch, SSM_GROUPS, SSM_STATE, SSM_HPG * SSM_HEAD_DIM)
    fwd = lambda b, c: (b * nc + c, 0)
    bwd = lambda b, c: (b * nc + nc - 1 - c, 0)
    const2 = lambda b, c: (0, 0)
    hmap = lambda b, c: (b, 0, 0, 0)
    hspec = pl.BlockSpec((None,) + hshape[1:], hmap)
    return pl.pallas_call(
        _ssd_kernel,
        grid=(batch, nc),
        in_specs=[pl.BlockSpec((q, w), fwd), pl.BlockSpec((q, LANES), fwd),
                  pl.BlockSpec((None, 2 * SSM_HEADS, q), lambda b, c: (b, 0, c)),
                  pl.BlockSpec((q, w), bwd), pl.BlockSpec((q, LANES), bwd),
                  pl.BlockSpec((None, 2 * SSM_HEADS, q), lambda b, c: (b, 0, nc - 1 - c)),
                  pl.BlockSpec((1, LANES), const2), pl.BlockSpec((1, LANES), const2),
                  pl.BlockSpec((2 * SSM_HEADS, 1), const2), pl.BlockSpec((2 * SSM_HEADS, 1), const2),
                  pl.BlockSpec((1, SSM_D_INNER), const2), hspec, hspec],
        out_specs=[pl.BlockSpec((q, SSM_D_INNER), fwd), pl.BlockSpec((q, SSM_D_INNER), bwd), hspec, hspec],
        out_shape=[jax.ShapeDtypeStruct((t, SSM_D_INNER), F32), jax.ShapeDtypeStruct((t, SSM_D_INNER), F32),
                   jax.ShapeDtypeStruct(hshape, F32), jax.ShapeDtypeStruct(hshape, F32)],
        scratch_shapes=[pltpu.VMEM(hshape[1:], F32), pltpu.VMEM(hshape[1:], F32)],
        compiler_params=_params(("parallel", "arbitrary")),
        name="ssd_bidir",
    )(xbc, dt_raw, dt_t, xbc, dt_raw, dt_t, bias_r, a_r, dt_bias.reshape(-1, 1), a_neg.reshape(-1, 1),
      dsk, h0f, h0b)


def _outproj_kernel(oa_ref, yf_ref, yb_ref, z_ref, ng_ref, w1_ref, w2_ref, x_ref, gate_ref, o_ref):
    y = yf_ref[...] + yb_ref[...]
    u = y * _silu(z_ref[...].astype(F32))
    un = (u * lax.rsqrt(jnp.mean(u * u, axis=-1, keepdims=True) + RMS_EPS)) * ng_ref[...]
    acc = (jnp.dot(oa_ref[...], w1_ref[...], preferred_element_type=F32)
           + jnp.dot(un.astype(BF16), w2_ref[...], preferred_element_type=F32))
    o_ref[...] = x_ref[...] + gate_ref[...] * acc


def out_proj(o_att, yf, yb, z, norm_g, w1, w2, x2d, gate, rows_per_mod):
    t, d = x2d.shape
    tm = _tile(rows_per_mod, 512)
    tpm = rows_per_mod // tm
    nb = gate.shape[0]
    row = lambda i: (i, 0)
    const = lambda i: (0, 0)
    return pl.pallas_call(
        _outproj_kernel,
        grid=(t // tm,),
        in_specs=[pl.BlockSpec((tm, d), row), pl.BlockSpec((tm, d), row), pl.BlockSpec((tm, d), row),
                  pl.BlockSpec((tm, d), row), pl.BlockSpec((1, d), const),
                  pl.BlockSpec(w1.shape, const), pl.BlockSpec(w2.shape, const),
                  pl.BlockSpec((tm, d), row), pl.BlockSpec((None, 1, d), lambda i: (i // tpm, 0, 0))],
        out_specs=pl.BlockSpec((tm, d), row),
        out_shape=jax.ShapeDtypeStruct((t, d), F32),
        compiler_params=_params(("parallel",)),
        name="mixer_out_proj",
    )(o_att, yf, yb, z, norm_g.reshape(1, d), w1, w2, x2d, gate.reshape(nb, 1, d))


def _mm_res_kernel(a_ref, w_ref, x_ref, gate_ref, o_ref):
    acc = jnp.dot(a_ref[...], w_ref[...], preferred_element_type=F32)
    o_ref[...] = x_ref[...] + gate_ref[...] * acc


def matmul_residual(a, w, x2d, gate, rows_per_mod):
    t, d = x2d.shape
    tm = _tile(rows_per_mod, 512)
    tpm = rows_per_mod // tm
    nb = gate.shape[0]
    return pl.pallas_call(
        _mm_res_kernel,
        grid=(t // tm,),
        in_specs=[pl.BlockSpec((tm, a.shape[1]), lambda i: (i, 0)),
                  pl.BlockSpec(w.shape, lambda i: (0, 0)),
                  pl.BlockSpec((tm, d), lambda i: (i, 0)),
                  pl.BlockSpec((None, 1, d), lambda i: (i // tpm, 0, 0))],
        out_specs=pl.BlockSpec((tm, d), lambda i: (i, 0)),
        out_shape=jax.ShapeDtypeStruct((t, d), F32),
        compiler_params=_params(("parallel",)),
        name="matmul_gated_residual",
    )(a, w, x2d, gate.reshape(nb, 1, d))


def _dft_stage1_kernel(m_ref, z_ref, o_ref):
    z = z_ref[...]
    half = z.shape[1] // 2
    zs = jnp.concatenate([z[:, :half], z[:, half:]], axis=0)
    a = jnp.dot(m_ref[...], zs, preferred_element_type=F32)
    n1 = a.shape[0] // 2
    o_ref[0] = a[:n1].astype(o_ref.dtype)
    o_ref[1] = a[n1:].astype(o_ref.dtype)


def _dft_stage2_kernel(m_ref, a_ref, o_ref):
    o_ref[...] = jnp.dot(m_ref[...], a_ref[...], preferred_element_type=F32).astype(o_ref.dtype)


def _dft_tables(l, d):
    n1 = FOURIER_N1
    n2 = l // n1
    gw = d // FOURIER_GROUPS
    two_pi = 2.0 * math.pi
    ch = jnp.arange(gw, dtype=I32)
    ph = (ch[:, None] * ch[None, :]) % gw
    ang = ph.astype(F32) * (two_pi / gw)
    cc, sc = jnp.cos(ang) * gw ** -0.5, jnp.sin(ang) * gw ** -0.5
    eye = jnp.eye(FOURIER_GROUPS, dtype=F32)
    w_cs = jnp.concatenate([jnp.kron(eye, cc), -jnp.kron(eye, sc)], axis=1)
    k1 = jnp.arange(n1, dtype=I32)
    pos = jnp.arange(n2, dtype=I32)[:, None, None] + n2 * jnp.arange(n1, dtype=I32)[None, None, :]
    ang1 = ((k1[None, :, None] * pos) % l).astype(F32) * (two_pi / l)
    gc, gs = jnp.cos(ang1) * l ** -0.5, jnp.sin(ang1) * l ** -0.5
    m1 = jnp.concatenate([jnp.concatenate([gc, gs], axis=2),
                          jnp.concatenate([-gs, gc], axis=2)], axis=1)
    k2 = jnp.arange(n2, dtype=I32)
    ang2 = ((k2[:, None] * k2[None, :]) % n2).astype(F32) * (two_pi / n2)
    m2 = jnp.concatenate([jnp.cos(ang2), jnp.sin(ang2)], axis=1)
    return w_cs.astype(BF16), m1.astype(BF16), m2.astype(BF16)


def fourier_positions(z, m1, m2, batch):
    t, d2 = z.shape
    d = d2 // 2
    l = t // batch
    n1 = FOURIER_N1
    n2 = l // n1
    a = pl.pallas_call(
        _dft_stage1_kernel,
        grid=(batch, n2),
        in_specs=[pl.BlockSpec((None, 2 * n1, 2 * n1), lambda b, j: (j, 0, 0)),
                  pl.BlockSpec((None, n1, d2), lambda b, j: (b, 0, j))],
        out_specs=pl.BlockSpec((None, 2, None, n1, d), lambda b, j: (b, 0, j, 0, 0)),
        out_shape=jax.ShapeDtypeStruct((batch, 2, n2, n1, d), BF16),
        compiler_params=_params(("parallel", "parallel")),
        name="dft_stage1",
    )(m1, z.reshape(batch, n1, n2 * d2))
    cols = n1 * d
    tn = _tile(cols, 8192)
    f = pl.pallas_call(
        _dft_stage2_kernel,
        grid=(batch, cols // tn),
        in_specs=[pl.BlockSpec((n2, 2 * n2), lambda b, j: (0, 0)),
                  pl.BlockSpec((None, 2 * n2, tn), lambda b, j: (b, 0, j))],
        out_specs=pl.BlockSpec((None, n2, tn), lambda b, j: (b, 0, j)),
        out_shape=jax.ShapeDtypeStruct((batch, n2, cols), BF16),
        compiler_params=_params(("parallel", "parallel")),
        name="dft_stage2",
    )(m2, a.reshape(batch, 2 * n2, cols))
    return f.reshape(t, d)


SUB = 8
ROW_CODE_SHIFT = 14


def _to_row_tiles(ref, val):
    rows = val.shape[0]
    for j in range(SUB):
        ref[pl.ds(j, rows, stride=SUB), :] = val[:, j * LANES:(j + 1) * LANES]


def _from_row_tiles(ref, rows):
    return jnp.concatenate([ref[pl.ds(j, rows, stride=SUB), :] for j in range(SUB)], axis=1)


def _row_tile(ref, row):
    if isinstance(row, int):
        return ref.at[pl.ds(row * SUB, SUB), :]
    return ref.at[pl.ds(pl.multiple_of(row * SUB, SUB), SUB), :]


def _route_kernel(x_ref, g_ref, sh_ref, sc_ref, wr_ref, br_ref, h_ref, ti_ref, tg_ref, rank_ref, cnt_ref, carry):
    @pl.when(pl.program_id(0) == 0)
    def _():
        carry[...] = jnp.zeros(carry.shape, carry.dtype)

    h = _norm_mod(x_ref[...], g_ref[...], sh_ref[...], sc_ref[...])
    _to_row_tiles(h_ref, h)
    logits = jnp.dot(h, wr_ref[...], precision=HIGHEST, preferred_element_type=F32) + br_ref[...]
    lane = lax.broadcasted_iota(I32, logits.shape, 1)
    vals, idxs = [], []
    cur = logits
    for _ in range(TOP_K):
        m = jnp.max(cur, axis=-1, keepdims=True)
        idx = jnp.min(jnp.where(cur == m, lane, LANES), axis=-1, keepdims=True)
        vals.append(m)
        idxs.append(idx)
        cur = jnp.where(lane == idx, -jnp.inf, cur)
    es = [jnp.exp(v - vals[0]) for v in vals]
    tot = es[0] + es[1] + es[2] + es[3]
    ti = jnp.zeros(logits.shape, I32)
    tg = jnp.zeros(logits.shape, F32)
    for k in range(TOP_K):
        ti = jnp.where(lane == k, idxs[k], ti)
        tg = jnp.where(lane == k, es[k] / tot, tg)
    ti_ref[...] = ti
    tg_ref[...] = tg
    tm = logits.shape[0]
    earlier = (lax.broadcasted_iota(I32, (tm, tm), 1) < lax.broadcasted_iota(I32, (tm, tm), 0)).astype(BF16)
    base = carry[...]
    rank = jnp.zeros(logits.shape, I32)
    for k in range(TOP_K):
        hit = lane == idxs[k]
        onehot = jnp.where(hit, 1.0, 0.0)
        before = jnp.dot(earlier, onehot.astype(BF16), preferred_element_type=F32)
        rk = jnp.sum(jnp.where(hit, base + before, 0.0), axis=-1, keepdims=True)
        rank = jnp.where(lane == k, rk.astype(I32), rank)
        base = base + jnp.sum(onehot, axis=0, keepdims=True)
    rank_ref[...] = rank
    carry[...] = base
    cnt_ref[...] = base


def route(x2d, g, shift, scale, rows_per_mod, w_router, b_router):
    t, d = x2d.shape
    assert d == SUB * LANES
    nb = shift.shape[0]
    tm = _tile(rows_per_mod, 512)
    tpm = rows_per_mod // tm
    ne = w_router.shape[1]
    wr = jnp.pad(w_router, ((0, 0), (0, LANES - ne)))
    br = jnp.pad(b_router.reshape(1, ne), ((0, 0), (0, LANES - ne)), constant_values=NEG_BIG)
    return pl.pallas_call(
        _route_kernel,
        grid=(t // tm,),
        in_specs=[pl.BlockSpec((tm, d), lambda i: (i, 0)),
                  pl.BlockSpec((1, d), lambda i: (0, 0)),
                  pl.BlockSpec((None, 1, d), lambda i: (i // tpm, 0, 0)),
                  pl.BlockSpec((None, 1, d), lambda i: (i // tpm, 0, 0)),
                  pl.BlockSpec((d, LANES), lambda i: (0, 0)),
                  pl.BlockSpec((1, LANES), lambda i: (0, 0))],
        out_specs=[pl.BlockSpec((tm * SUB, LANES), lambda i: (i, 0)),
                   pl.BlockSpec((tm, LANES), lambda i: (i, 0)),
                   pl.BlockSpec((tm, LANES), lambda i: (i, 0)),
                   pl.BlockSpec((tm, LANES), lambda i: (i, 0)),
                   pl.BlockSpec((1, LANES), lambda i: (0, 0))],
        out_shape=[jax.ShapeDtypeStruct((t * SUB, LANES), F32), jax.ShapeDtypeStruct((t, LANES), I32),
                   jax.ShapeDtypeStruct((t, LANES), F32), jax.ShapeDtypeStruct((t, LANES), I32),
                   jax.ShapeDtypeStruct((1, LANES), F32)],
        scratch_shapes=[pltpu.VMEM((1, LANES), F32)],
        compiler_params=_params(("arbitrary",)),
        name="moe_route",
    )(x2d, g.reshape(1, d), shift.reshape(nb, 1, d), scale.reshape(nb, 1, d), wr, br)


def _dest_kernel(ti_ref, rank_ref, start_ref, dest_ref):
    ti = ti_ref[...]
    lane = lax.broadcasted_iota(I32, ti.shape, 1)
    dest = jnp.zeros(ti.shape, I32)
    for k in range(TOP_K):
        start = jnp.sum(jnp.where(lane == ti[:, k:k + 1], start_ref[...], 0), axis=-1, keepdims=True)
        dest = jnp.where(lane == k, start + rank_ref[:, k:k + 1], dest)
    dest_ref[...] = dest


def moe_plan(ti, rank, cnt, t):
    tb = MOE_ROWS
    tm = _tile(t, 512)
    counts = cnt[0, :N_EXPERTS].astype(I32)
    padded = (counts + tb - 1) // tb * tb
    pend = jnp.cumsum(padded)
    pstart = jnp.pad(pend - padded, (0, LANES - N_EXPERTS)).reshape(1, LANES)
    dest = pl.pallas_call(
        _dest_kernel,
        grid=(t // tm,),
        in_specs=[pl.BlockSpec((tm, LANES), lambda i: (i, 0)), pl.BlockSpec((tm, LANES), lambda i: (i, 0)),
                  pl.BlockSpec((1, LANES), lambda i: (0, 0))],
        out_specs=pl.BlockSpec((tm, LANES), lambda i: (i, 0)),
        out_shape=jax.ShapeDtypeStruct((t, LANES), I32),
        compiler_params=_params(("parallel",)),
        name="moe_dest",
    )(ti, rank, pstart)
    dest = dest[:, :TOP_K].reshape(-1)
    n_blocks = (t * TOP_K + N_EXPERTS * (tb - 1)) // tb + 1
    assert t <= 1 << ROW_CODE_SHIFT and (TOP_K * t + tb) << ROW_CODE_SHIFT < 1 << 31
    asg = jnp.arange(t * TOP_K, dtype=I32)
    real = (((asg % TOP_K) * t + asg // TOP_K) << ROW_CODE_SHIFT) | (asg // TOP_K)
    spill = (TOP_K * t + jnp.arange(n_blocks * tb, dtype=I32) % tb) << ROW_CODE_SHIFT
    codes = spill.at[dest].set(real, unique_indices=True, mode='promise_in_bounds')
    starts = jnp.arange(n_blocks, dtype=I32) * tb
    blk_e = jnp.minimum(jnp.sum((pend[None, :] <= starts[:, None]).astype(I32), axis=1), N_EXPERTS - 1)
    nact = (pend[-1] // tb).astype(I32).reshape(1)
    return codes, blk_e, nact


def _expert_kernel(tb, n_tok, be_ref, code_ref, nact_ref, h_hbm, wgu_ref, bgu_ref, wdn_ref, bdn_ref, y_hbm,
                   xbuf, ybuf, wgu_s, wdn_perm, wdn_s, gsem, ssem):
    i = pl.program_id(0)
    n_blocks = pl.num_programs(0)
    nact = nact_ref[0]
    de = wdn_ref.shape[0]
    slot = i % 2
    other = 1 - slot

    def start_gather(base, r, dst_slot, priority=0):
        tok = code_ref[base + r] & ((1 << ROW_CODE_SHIFT) - 1)
        pltpu.make_async_copy(_row_tile(h_hbm, tok), _row_tile(xbuf.at[dst_slot], r),
                              gsem.at[dst_slot]).start(priority)

    def start_scatter(base, r, src_slot, priority=0):
        row = code_ref[base + r] >> ROW_CODE_SHIFT
        pltpu.make_async_copy(_row_tile(ybuf.at[src_slot], r), _row_tile(y_hbm, row),
                              ssem.at[src_slot]).start(priority)

    def wait_gather(s):
        pltpu.make_async_copy(h_hbm.at[pl.ds(0, tb * SUB), :], xbuf.at[s], gsem.at[s]).wait()

    def wait_scatter(s):
        pltpu.make_async_copy(ybuf.at[s], y_hbm.at[pl.ds(0, tb * SUB), :], ssem.at[s]).wait()

    @pl.when(i == 0)
    def _():
        ybuf[...] = jnp.zeros(ybuf.shape, ybuf.dtype)

        def first(r, carry):
            start_gather(0, r, 0)
            return carry
        lax.fori_loop(0, tb, first, 0, unroll=8)

    @pl.when((i < nact) & ((i == 0) | (be_ref[i] != be_ref[jnp.maximum(i - 1, 0)])))
    def _():
        wgu_s[...] = wgu_ref[...].astype(BF16)
        half = LANES // 2
        for c in range(wdn_perm.shape[0]):
            for m in range(0, de, LANES):
                wdn_perm[c, pl.ds(m, half, stride=2), :] = wdn_ref[m:m + half, c * LANES:(c + 1) * LANES]
                wdn_perm[c, pl.ds(m + 1, half, stride=2), :] = wdn_ref[m + half:m + LANES, c * LANES:(c + 1) * LANES]
            wdn_s[:, c * LANES:(c + 1) * LANES] = wdn_perm[c].astype(BF16)

    def start_copies(slot, other):
        wait_gather(slot)

        @pl.when(i >= 1)
        def _():
            wait_scatter(slot)

        prev_base = jnp.where(i == 0, n_blocks - 1, i - 1) * tb
        next_base = (i + 1) * tb
        for r in range(tb):
            start_gather(next_base, r, other, r % 2)
            start_scatter(prev_base, r, other, (r + 1) % 2)

    def run_block(slot):
        x = _from_row_tiles(xbuf.at[slot], tb).astype(BF16)
        even = lax.broadcasted_iota(I32, (tb, LANES), 1) % 2 == 0
        cw = 2 * LANES
        n_chunks = wgu_s.shape[1] // cw
        parts = []
        for c in range(n_chunks):
            gu = jnp.dot(x, wgu_s[:, c * cw:(c + 1) * cw], preferred_element_type=F32) + bgu_ref[:, c * cw:(c + 1) * cw]
            acts = []
            for c0 in range(0, cw, LANES):
                v = gu[:, c0:c0 + LANES]
                glu = jnp.minimum(v, SWIGLU_LIMIT)
                lin = jnp.clip(v, -SWIGLU_LIMIT, SWIGLU_LIMIT) + 1.0
                acts.append(glu * jax.nn.sigmoid(SWIGLU_ALPHA * glu) * pltpu.roll(lin, LANES - 1, 1))
            parts.append(jnp.where(even, acts[0], pltpu.roll(acts[1], 1, 1)).astype(BF16))
        y = jnp.dot(jnp.concatenate(parts, axis=1), wdn_s[...], preferred_element_type=F32) + bdn_ref[...]
        _to_row_tiles(ybuf.at[slot], y)

    for parity in range(2):
        pl.when((i < nact) & (slot == parity))(functools.partial(start_copies, parity, 1 - parity))
    for parity in range(2):
        pl.when((i < nact) & (slot == parity))(functools.partial(run_block, parity))

    @pl.when(i == nact)
    def _():
        wait_gather(slot)
        wait_scatter(slot)

        def last(r, carry):
            start_scatter((i - 1) * tb, r, other)
            return carry
        lax.fori_loop(0, tb, last, 0, unroll=8)
        wait_scatter(other)


def expert_ffn(h_rt, codes, blk_e, nact, layer, w_gu, b_gu, w_dn, b_dn):
    r = codes.shape[0]
    n_tok = h_rt.shape[0] // SUB
    tb = MOE_ROWS
    _, ne, d, de2 = w_gu.shape
    de = de2 // 2
    b_gu = b_gu.reshape(-1, ne, 1, de2)
    b_dn = b_dn.reshape(-1, ne, 1, d)
    return pl.pallas_call(
        functools.partial(_expert_kernel, tb, n_tok),
        grid_spec=pltpu.PrefetchScalarGridSpec(
            num_scalar_prefetch=3, grid=(r // tb,),
            in_specs=[pl.BlockSpec(memory_space=pl.ANY),
                      pl.BlockSpec((None, None, d, de2), lambda i, be, cd, na: (layer, be[i], 0, 0)),
                      pl.BlockSpec((None, None, 1, de2), lambda i, be, cd, na: (layer, be[i], 0, 0)),
                      pl.BlockSpec((None, None, de, d), lambda i, be, cd, na: (layer, be[i], 0, 0)),
                      pl.BlockSpec((None, None, 1, d), lambda i, be, cd, na: (layer, be[i], 0, 0))],
            out_specs=pl.BlockSpec(memory_space=pl.ANY),
            scratch_shapes=[pltpu.VMEM((2, tb * SUB, LANES), F32), pltpu.VMEM((2, tb * SUB, LANES), F32),
                            pltpu.VMEM((d, de2), BF16), pltpu.VMEM((d // LANES, de, LANES), F32),
                            pltpu.VMEM((de, d), BF16),
                            pltpu.SemaphoreType.DMA((2,)), pltpu.SemaphoreType.DMA((2,))]),
        out_shape=jax.ShapeDtypeStruct(((TOP_K * n_tok + tb) * SUB, LANES), F32),
        compiler_params=_params(("arbitrary",), 58),
        name="moe_experts",
    )(blk_e, codes, nact, h_rt, w_gu, b_gu, w_dn, b_dn)


def _combine_kernel(y0_ref, y1_ref, y2_ref, y3_ref, x_ref, gate_ref, tg_ref, o_ref):
    tg = tg_ref[...]
    tm = x_ref.shape[0]
    ys = [_from_row_tiles(y_ref, tm) * tg[:, k:k + 1] for k, y_ref in enumerate((y0_ref, y1_ref, y2_ref, y3_ref))]
    o_ref[...] = x_ref[...] + gate_ref[...] * ((ys[0] + ys[1]) + (ys[2] + ys[3]))


def moe_combine(y4, tg, x2d, gate, rows_per_mod):
    t, d = x2d.shape
    nb = gate.shape[0]
    tm = _tile(rows_per_mod, MOE_ROWS)
    tpm = rows_per_mod // tm
    nt = t // tm
    planes = [pl.BlockSpec((tm * SUB, LANES), functools.partial(lambda k, i: (k * nt + i, 0), k))
              for k in range(TOP_K)]
    return pl.pallas_call(
        _combine_kernel,
        grid=(nt,),
        in_specs=planes + [
                  pl.BlockSpec((tm, d), lambda i: (i, 0)),
                  pl.BlockSpec((None, 1, d), lambda i: (i // tpm, 0, 0)),
                  pl.BlockSpec((tm, LANES), lambda i: (i, 0))],
        out_specs=pl.BlockSpec((tm, d), lambda i: (i, 0)),
        out_shape=jax.ShapeDtypeStruct((t, d), F32),
        compiler_params=_params(("parallel",)),
        name="moe_combine",
    )(y4, y4, y4, y4, x2d, gate.reshape(nb, 1, d), tg)


def moe_layer(x2d, norm_g, shift, scale, gate, rows_per_mod, w_router, b_router, layer, w_gu, b_gu, w_dn, b_dn):
    t = x2d.shape[0]
    h_rt, ti, tg, rank, cnt = route(x2d, norm_g, shift, scale, rows_per_mod, w_router, b_router)
    codes, blk_e, nact = moe_plan(ti, rank, cnt, t)
    y4 = expert_ffn(h_rt, codes, blk_e, nact, layer, w_gu, b_gu, w_dn, b_dn)
    return moe_combine(y4, tg, x2d, gate, rows_per_mod)


def _rope_tables(s):
    n_freq = DA_HEAD_DIM // 4
    freqs = ROPE_BASE ** (-jnp.arange(n_freq, dtype=F32) / n_freq)
    pos = jnp.arange(s, dtype=I32)
    ang_r = (pos // GRID_W).astype(F32)[:, None] * freqs
    ang_c = (pos % GRID_W).astype(F32)[:, None] * freqs
    ang = jnp.concatenate([ang_r, ang_r, ang_c, ang_c], axis=-1)
    ang = jnp.concatenate([ang, ang], axis=-1)
    return jnp.cos(ang), jnp.sin(ang)


def _even_layer(x2d, ctx2d, mods, norm_g1, w_in, w_out, q_norm_g, k_norm_g, da_lambda, da_subln_g,
                conv_xb_w, conv_xb_b, conv_c_w, conv_c_b, dt_bias, a_log, d_skip, ssm_norm_g,
                lam_init, batch):
    sx1, cx1, gx1, sc1, cc1 = mods
    t, d = x2d.shape
    s = t // batch
    ctx_len = ctx2d.shape[0] // batch
    qkw = DA_HEADS * 2 * DA_HEAD_DIM
    vw = DA_HEADS * DA_V_DIM
    col_q, col_z = 0, qkw
    col_c = col_z + SSM_D_INNER
    col_k = col_c + SSM_BC_W
    col_v = col_k + qkw
    col_xb = col_v + vw
    col_dt = col_xb + SSM_XB_W
    wb = w_in.astype(BF16)
    w_q, w_k, w_v = wb[:, col_q:col_q + qkw], wb[:, col_k:col_k + qkw], wb[:, col_v:col_v + vw]
    w_z = wb[:, col_z:col_z + SSM_D_INNER]
    w_xbc = jnp.concatenate([wb[:, col_xb:col_xb + SSM_XB_W], wb[:, col_c:col_c + SSM_BC_W]], axis=1)
    w_dt = jnp.pad(wb[:, col_dt:col_dt + 2 * SSM_HEADS], ((0, 0), (0, LANES - 2 * SSM_HEADS)))
    weights = [w_q, w_k, w_v, w_z, w_xbc, w_dt]
    dts = [BF16, BF16, BF16, BF16, BF16, F32]
    q_u, k_u, v_x, z_x, xbc_u, dt_x = nm_panel(x2d, norm_g1, sx1, cx1, s, weights, dts)
    _, kc_u, v_c, _, xbc_cu, dt_c = nm_panel(ctx2d, norm_g1, sc1, cc1, batch * ctx_len, weights, dts)

    lam = (jnp.exp(jnp.sum(da_lambda[0] * da_lambda[1])) -
           jnp.exp(jnp.sum(da_lambda[2] * da_lambda[3]))).astype(F32) + lam_init
    a_neg = -jnp.exp(a_log.astype(F32)).reshape(-1)

    cos_t, sin_t = _rope_tables(s)
    q = qk_prep(q_u, q_norm_g, cos_t, sin_t, s, True, DA_HEAD_DIM ** -0.5 * LOG2E)
    score_bound = (1.01 * DA_HEAD_DIM ** 0.5 * LOG2E) * jnp.max(jnp.abs(q_norm_g)) * jnp.max(jnp.abs(k_norm_g))
    k_x = qk_prep(k_u, k_norm_g, cos_t, sin_t, s, True, 1.0)
    k_c = qk_prep(kc_u, k_norm_g, cos_t, sin_t, ctx_len, False, 1.0)
    k_all = jnp.concatenate([k_x.reshape(batch, s, qkw), k_c.reshape(batch, ctx_len, qkw)], axis=1)
    v_all = jnp.concatenate([v_x.reshape(batch, s, vw), v_c.reshape(batch, ctx_len, vw)], axis=1)
    o_att = diff_attention(q, k_all, v_all, lam, score_bound, da_subln_g, lam_init, batch)

    conv_w = jnp.concatenate([conv_xb_w, conv_c_w], axis=1)
    conv_b = jnp.concatenate([conv_xb_b, conv_c_b], axis=0)
    xbc_x = conv_silu(xbc_u, conv_w, conv_b, s)
    xbc_c = conv_silu(xbc_cu, conv_w, conv_b, ctx_len)
    hshape = (batch, SSM_GROUPS, SSM_STATE, SSM_HPG * SSM_HEAD_DIM)
    zero_h = jnp.zeros(hshape, F32)
    dt_bias_f = dt_bias.reshape(-1).astype(F32)
    _, _, h_f, h_b = ssd_bidir(xbc_c, dt_c, dt_bias_f, a_neg, d_skip, zero_h, zero_h, batch)
    y_f, y_b, _, _ = ssd_bidir(xbc_x, dt_x, dt_bias_f, a_neg, d_skip, h_f, h_b, batch)
    wo = w_out.astype(BF16)
    return out_proj(o_att, y_f, y_b, z_x, ssm_norm_g, wo[:vw], wo[vw:], x2d, gx1, s)


def _odd_layer(x2d, mods, norm_g1, w_fourier, batch):
    sx1, cx1, gx1 = mods
    t, d = x2d.shape
    s = t // batch
    w_cs, m1, m2 = _dft_tables(s, d)
    (z,) = nm_panel(x2d, norm_g1, sx1, cx1, s, [w_cs], [BF16])
    f = fourier_positions(z, m1, m2, batch)
    return matmul_residual(f, w_fourier.astype(BF16), x2d, gx1, s)


def kernel(x, c, ctx, c_ctx, ada_w, ada_b, norm_g, w_in, w_out, q_norm_g, k_norm_g, da_lambda, da_subln_g, conv_xb_w, conv_xb_b, conv_c_w, conv_c_b, dt_bias, a_log, d_skip, ssm_norm_g, w_fourier, w_router, b_router, w_gate_up, b_gate_up, w_down, b_down):
    b, s, d = x.shape
    depth = ada_w.shape[0]
    assert b + 1 <= 8
    cvecs = jnp.concatenate([c, c_ctx[None, :], jnp.zeros((8 - b - 1, d), F32)], axis=0)
    mods = ada_all(cvecs, ada_w, ada_b)
    x2d = x.reshape(b * s, d)
    ctx2d = ctx.reshape(-1, d)
    for i in range(depth):
        m = mods[i].reshape(8, 6, d)
        sx1, cx1, gx1, sx2, cx2, gx2 = [m[:b, j] for j in range(6)]
        if i % 2 == 0:
            e = i // 2
            lam_init = 0.8 - 0.6 * math.exp(-0.3 * i)
            if any(j % 2 == 0 for j in range(i + 1, depth)):
                raise NotImplementedError("context stream output is only needed for depth > 2")
            sc1, cc1 = m[b:b + 1, 0], m[b:b + 1, 1]
            x2d = _even_layer(x2d, ctx2d, (sx1, cx1, gx1, sc1, cc1), norm_g[i, 0], w_in[e], w_out[e],
                              q_norm_g[e], k_norm_g[e], da_lambda[e], da_subln_g[e], conv_xb_w[e],
                              conv_xb_b[e], conv_c_w[e], conv_c_b[e], dt_bias[e], a_log[e], d_skip[e],
                              ssm_norm_g[e], lam_init, b)
        else:
            x2d = _odd_layer(x2d, (sx1, cx1, gx1), norm_g[i, 0], w_fourier[i // 2], b)
        x2d = moe_layer(x2d, norm_g[i, 1], sx2, cx2, gx2, s, w_router[i], b_router[i],
                        i, w_gate_up, b_gate_up, w_down, b_down)
    return x2d.reshape(b, s, d)
```

```python
import functools
import math

import jax
import jax.numpy as jnp
from jax import lax
from jax.experimental import pallas as pl
from jax.experimental.pallas import tpu as pltpu

F32 = jnp.float32
BF16 = jnp.bfloat16
I32 = jnp.int32

RMS_EPS = 1e-6
GRID_W = 64
ROPE_BASE = 10000.0
DA_HEADS = 8
DA_HEAD_DIM = 64
DA_V_DIM = 128
SSM_HEADS = 16
SSM_HEAD_DIM = 64
SSM_GROUPS = 2
SSM_HPG = 8
SSM_STATE = 128
SSM_CONV = 5
SSM_CHUNK = 128
SSM_D_INNER = 1024
SSM_BC_W = 256
SSM_XB_W = 1280
FOURIER_GROUPS = 4
FOURIER_N1 = 128
N_EXPERTS = 32
TOP_K = 4
SWIGLU_LIMIT = 7.0
SWIGLU_ALPHA = 1.702
MOE_ROWS = 256
LANES = 128
NEG_BIG = -1e30
MIB = 1024 * 1024
HIGHEST = lax.Precision.HIGHEST


def _params(sem, vmem_mib=48):
    return pltpu.CompilerParams(dimension_semantics=sem, vmem_limit_bytes=vmem_mib * MIB)


def _tile(n, pref):
    t = min(n, pref)
    while n % t:
        t //= 2
    return t


def _silu(v):
    return v * jax.nn.sigmoid(v)


def _norm_mod(xf, g, shift, scale):
    r = lax.rsqrt(jnp.mean(xf * xf, axis=-1, keepdims=True) + RMS_EPS)
    return ((xf * r) * g) * (1.0 + scale) + shift


def _ada_kernel(a_ref, w_ref, b_ref, o_ref):
    s = _silu(a_ref[...])
    o_ref[...] = jnp.dot(s.astype(BF16), w_ref[...].astype(BF16),
                         preferred_element_type=F32) + b_ref[...]


def ada_all(cvecs, ada_w, ada_b):
    depth, d, n = ada_w.shape
    tn = _tile(n, 1536)
    return pl.pallas_call(
        _ada_kernel,
        grid=(depth, n // tn),
        in_specs=[pl.BlockSpec((8, d), lambda l, j: (0, 0)),
                  pl.BlockSpec((None, d, tn), lambda l, j: (l, 0, j)),
                  pl.BlockSpec((None, 1, tn), lambda l, j: (l, 0, j))],
        out_specs=pl.BlockSpec((None, 8, tn), lambda l, j: (l, 0, j)),
        out_shape=jax.ShapeDtypeStruct((depth, 8, n), F32),
        compiler_params=_params(("parallel", "parallel")),
        name="ada_mod",
    )(cvecs, ada_w, ada_b.reshape(depth, 1, n))


def _nm_panel_kernel(n_w, x_ref, g_ref, sh_ref, sc_ref, *refs):
    w_refs, o_refs = refs[:n_w], refs[n_w:2 * n_w]
    hb = _norm_mod(x_ref[...], g_ref[...], sh_ref[...], sc_ref[...]).astype(BF16)
    for w_ref, o_ref in zip(w_refs, o_refs):
        n = w_ref.shape[1]
        cw = _tile(n, 512)
        for c0 in range(0, n, cw):
            o_ref[:, c0:c0 + cw] = jnp.dot(hb, w_ref[:, c0:c0 + cw],
                                           preferred_element_type=F32).astype(o_ref.dtype)


def nm_panel(x2d, g, shift, scale, rows_per_mod, weights, out_dtypes, tm_pref=512):
    t, d = x2d.shape
    nb = shift.shape[0]
    tm = _tile(rows_per_mod, tm_pref)
    tpm = rows_per_mod // tm
    n_w = len(weights)
    in_specs = [pl.BlockSpec((tm, d), lambda i: (i, 0)),
                pl.BlockSpec((1, d), lambda i: (0, 0)),
                pl.BlockSpec((None, 1, d), lambda i: (i // tpm, 0, 0)),
                pl.BlockSpec((None, 1, d), lambda i: (i // tpm, 0, 0))]
    in_specs += [pl.BlockSpec(w.shape, lambda i: (0, 0)) for w in weights]
    out_specs = [pl.BlockSpec((tm, w.shape[1]), lambda i: (i, 0)) for w in weights]
    out_shape = [jax.ShapeDtypeStruct((t, w.shape[1]), dt) for w, dt in zip(weights, out_dtypes)]
    return pl.pallas_call(
        functools.partial(_nm_panel_kernel, n_w),
        grid=(t // tm,),
        in_specs=in_specs, out_specs=out_specs, out_shape=out_shape,
        compiler_params=_params(("parallel",), 56),
        name="norm_mod_proj",
    )(x2d, g.reshape(1, d), shift.reshape(nb, 1, d), scale.reshape(nb, 1, d), *weights)


def _qk_prep_kernel(rope, out_scale, u_ref, g_ref, seg_ref, cos_ref, sin_ref, o_ref):
    seg = seg_ref[...]
    g = g_ref[...]
    n_heads = u_ref.shape[1] // LANES
    for c in range(n_heads):
        u = u_ref[:, c * LANES:(c + 1) * LANES].astype(F32)
        u2 = u * u
        hi = u2.astype(BF16)
        lo = (u2 - hi.astype(F32)).astype(BF16)
        ss = (jnp.dot(hi, seg, preferred_element_type=F32)
              + jnp.dot(lo, seg, preferred_element_type=F32))
        nrm = (u * lax.rsqrt(ss * (1.0 / DA_HEAD_DIM) + RMS_EPS)) * g
        if rope:
            lane = lax.broadcasted_iota(I32, nrm.shape, 1)
            first = (lane % 32) < 16
            rot = jnp.where(first, -pltpu.roll(nrm, LANES - 16, 1), pltpu.roll(nrm, 16, 1))
            nrm = nrm * cos_ref[...] + rot * sin_ref[...]
        o_ref[:, c * LANES:(c + 1) * LANES] = (nrm * out_scale).astype(o_ref.dtype)


def qk_prep(u, gain, cos_t, sin_t, seq, rope, out_scale):
    t, w = u.shape
    tm = _tile(seq, 256)
    tps = seq // tm
    seg = (jnp.arange(LANES)[:, None] // DA_HEAD_DIM == jnp.arange(LANES)[None, :] // DA_HEAD_DIM)
    return pl.pallas_call(
        functools.partial(_qk_prep_kernel, rope, out_scale),
        grid=(t // tm,),
        in_specs=[pl.BlockSpec((tm, w), lambda i: (i, 0)),
                  pl.BlockSpec((1, LANES), lambda i: (0, 0)),
                  pl.BlockSpec((LANES, LANES), lambda i: (0, 0)),
                  pl.BlockSpec((tm, LANES), lambda i: (i % tps, 0)),
                  pl.BlockSpec((tm, LANES), lambda i: (i % tps, 0))],
        out_specs=pl.BlockSpec((tm, w), lambda i: (i, 0)),
        out_shape=jax.ShapeDtypeStruct((t, w), BF16),
        compiler_params=_params(("parallel",)),
        name="qk_norm_rope",
    )(u, jnp.tile(gain, 2).reshape(1, LANES), seg.astype(BF16), cos_t, sin_t)


LOG2E = math.log2(math.e)
ATTN_SHIFT_LIMIT = 60.0


def _split_components(q):
    lane = lax.broadcasted_iota(I32, q.shape, 1)
    zero = jnp.zeros_like(q)
    return jnp.where(lane < DA_HEAD_DIM, q, zero), jnp.where(lane >= DA_HEAD_DIM, q, zero)


def _attn_finish(n0, l0, n1, l1, lam, g, out_mult, o_ref):
    o = n0 / l0 - lam * (n1 / l1)
    o = (o * lax.rsqrt(jnp.mean(o * o, axis=-1, keepdims=True) + RMS_EPS)) * g
    o_ref[...] = (o * out_mult).astype(o_ref.dtype)


def _attn_online_kernel(tk, out_mult, sc_ref, q_ref, k_ref, v_ref, g_ref, o_ref):
    qs = _split_components(q_ref[...])
    tq = q_ref.shape[0]
    n_kv = k_ref.shape[0] // tk

    def body(i, carry):
        off = pl.multiple_of(i * tk, tk)
        k = k_ref[pl.ds(off, tk), :]
        v = v_ref[pl.ds(off, tk), :]
        new = []
        for c in range(2):
            m, l, acc = carry[c]
            s = lax.dot_general(qs[c], k, (((1,), (1,)), ((), ())), preferred_element_type=F32)
            m_new = jnp.maximum(m, jnp.max(s, axis=-1, keepdims=True))
            alpha = jnp.exp2(m - m_new)
            p = jnp.exp2(s - m_new)
            l_new = alpha * l + jnp.sum(p, axis=-1, keepdims=True)
            acc_new = alpha * acc + jnp.dot(p.astype(BF16), v, preferred_element_type=F32)
            new.append((m_new, l_new, acc_new))
        return tuple(new)

    init = tuple((jnp.full((tq, 1), -jnp.inf, F32), jnp.zeros((tq, 1), F32),
                  jnp.zeros((tq, DA_V_DIM), F32)) for _ in range(2))
    (_, l0, a0), (_, l1, a1) = lax.fori_loop(0, n_kv, body, init)
    _attn_finish(a0, l0, a1, l1, sc_ref[0], g_ref[...], out_mult, o_ref)


def _attn_shift_kernel(tk, out_mult, sc_ref, q_ref, k_ref, v_ref, g_ref, o_ref):
    qs = _split_components(q_ref[...])
    tq = q_ref.shape[0]
    n_kv = k_ref.shape[0] // tk
    shift = sc_ref[1]
    ones = jnp.ones((tk, LANES), BF16)

    sub = 2 * LANES

    def body(i, acc):
        acc = list(acc)
        for j in range(tk // sub):
            off = pl.multiple_of(i * tk + j * sub, sub)
            k = k_ref[pl.ds(off, sub), :]
            va = jnp.concatenate([v_ref[pl.ds(off, sub), :], ones[:sub]], axis=1)
            for c in range(2):
                s = lax.dot_general(qs[c], k, (((1,), (1,)), ((), ())), preferred_element_type=F32)
                acc[c] = acc[c] + jnp.dot(jnp.exp2(s - shift).astype(BF16), va, preferred_element_type=F32)
        return tuple(acc)

    zero = jnp.zeros((tq, 2 * LANES), F32)
    a0, a1 = lax.fori_loop(0, n_kv, body, (zero, zero))
    _attn_finish(a0[:, :LANES], a0[:, LANES:], a1[:, :LANES], a1[:, LANES:], sc_ref[0], g_ref[...], out_mult,
                 o_ref)


def diff_attention(q, k_all, v_all, lam, score_bound, subln_g, lam_init, batch):
    t, w = q.shape
    s = t // batch
    sk = k_all.shape[1]
    tq = _tile(s, 1024)
    tk = 2816 if sk % 2816 == 0 else _tile(sk, 256)
    nq = s // tq
    scalars = jnp.stack([lam, score_bound]).astype(F32)

    def run(body, tk):
        return pl.pallas_call(
            functools.partial(body, tk, 1.0 - lam_init),
            grid=(batch, DA_HEADS, nq),
            in_specs=[pl.BlockSpec(memory_space=pltpu.SMEM),
                      pl.BlockSpec((tq, LANES), lambda b, h, i: (b * nq + i, h)),
                      pl.BlockSpec((None, sk, LANES), lambda b, h, i: (b, 0, h)),
                      pl.BlockSpec((None, sk, LANES), lambda b, h, i: (b, 0, h)),
                      pl.BlockSpec((1, LANES), lambda b, h, i: (0, 0))],
            out_specs=pl.BlockSpec((tq, LANES), lambda b, h, i: (b * nq + i, h)),
            out_shape=jax.ShapeDtypeStruct((t, w), BF16),
            compiler_params=_params(("parallel", "parallel", "parallel")),
            name="diff_attention",
        )(scalars, q, k_all, v_all, subln_g.reshape(1, LANES))

    return lax.cond(score_bound <= ATTN_SHIFT_LIMIT,
                    lambda: run(_attn_shift_kernel, sk), lambda: run(_attn_online_kernel, tk))


HALO = 16


def _conv_kernel(tps, p_ref, c_ref, n_ref, w_ref, b_ref, o_ref, ext_ref):
    t = pl.program_id(0) % tps
    tm = c_ref.shape[0]
    prev = p_ref[...].astype(F32)
    nxt = n_ref[...].astype(F32)
    ext_ref[0:HALO, :] = jnp.where(t == 0, 0.0, prev)
    ext_ref[HALO:HALO + tm, :] = c_ref[...].astype(F32)
    ext_ref[HALO + tm:2 * HALO + tm, :] = jnp.where(t == tps - 1, 0.0, nxt)
    width = c_ref.shape[1]
    cw = _tile(width, 256)
    pad = (SSM_CONV - 1) // 2
    for c0 in range(0, width, cw):
        acc = jnp.broadcast_to(b_ref[:, c0:c0 + cw], (tm, cw))
        for k in range(SSM_CONV):
            acc = acc + w_ref[k:k + 1, c0:c0 + cw] * ext_ref[HALO - pad + k:HALO - pad + k + tm, c0:c0 + cw]
        o_ref[:, c0:c0 + cw] = _silu(acc).astype(o_ref.dtype)


def conv_silu(u, w, b, seq):
    t, c = u.shape
    tm = _tile(seq, 512)
    tps = seq // tm
    hb = tm // HALO
    last = t // HALO - 1
    return pl.pallas_call(
        functools.partial(_conv_kernel, tps),
        grid=(t // tm,),
        in_specs=[pl.BlockSpec((HALO, c), lambda i: (jnp.maximum(i * hb - 1, 0), 0)),
                  pl.BlockSpec((tm, c), lambda i: (i, 0)),
                  pl.BlockSpec((HALO, c), lambda i: (jnp.minimum((i + 1) * hb, last), 0)),
                  pl.BlockSpec((SSM_CONV, c), lambda i: (0, 0)),
                  pl.BlockSpec((1, c), lambda i: (0, 0))],
        out_specs=pl.BlockSpec((tm, c), lambda i: (i, 0)),
        out_shape=jax.ShapeDtypeStruct((t, c), BF16),
        scratch_shapes=[pltpu.VMEM((tm + 2 * HALO, c), F32)],
        compiler_params=_params(("parallel",)),
        name="dwconv_silu",
    )(u, u, u, w, b.reshape(1, c))


def _softplus(v):
    return jnp.maximum(v, 0.0) + jnp.log1p(jnp.exp(-jnp.abs(v)))


def _pair(lane_lo, a, b):
    return jnp.where(lane_lo, a, b)


def _ssd_kernel(xf_ref, dtf_ref, dtTf_ref, xb_ref, dtb_ref, dtTb_ref, bias_r, a_r, bias_c, a_c,
                dsk_ref, h0f_ref, h0b_ref, yf_ref, yb_ref, hfo_ref, hbo_ref, hf_s, hb_s):
    c = pl.program_id(1)
    q = SSM_CHUNK
    nh = SSM_HEADS

    @pl.when(c == 0)
    def _():
        hf_s[...] = h0f_ref[...]
        hb_s[...] = h0b_ref[...]

    li = lax.broadcasted_iota(I32, (q, q), 0)
    si = lax.broadcasted_iota(I32, (q, q), 1)
    lower = li >= si
    upper = li <= si
    tri_l = lower.astype(F32)
    tri_u = upper.astype(F32)
    lane_lo = lax.broadcasted_iota(I32, (q, LANES), 1) < SSM_HEAD_DIM
    lane_lo1 = lax.broadcasted_iota(I32, (1, LANES), 1) < SSM_HEAD_DIM

    def bc(col):
        return jnp.broadcast_to(col, (q, LANES))

    def state_update(h_s, g, xs32, bg, wq, edge, base):
        parts, decs = [], []
        for pr in range(SSM_HPG // 2):
            h0 = g * SSM_HPG + 2 * pr
            wp = _pair(lane_lo, bc(wq[:, base + h0:base + h0 + 1]), bc(wq[:, base + h0 + 1:base + h0 + 2]))
            parts.append((xs32[:, h0 * 64:h0 * 64 + LANES] * wp).astype(BF16))
            decs.append(_pair(lane_lo1, jnp.broadcast_to(edge[:, base + h0:base + h0 + 1], (1, LANES)),
                              jnp.broadcast_to(edge[:, base + h0 + 1:base + h0 + 2], (1, LANES))))
        xw = jnp.concatenate(parts, axis=1)
        dec = jnp.exp(jnp.concatenate(decs, axis=1))
        upd = lax.dot_general(bg, xw, (((0,), (0,)), ((), ())), preferred_element_type=F32)
        h_s[g] = h_s[g] * dec + upd

    xbc = xf_ref[...]
    xs_b = xbc[:, :SSM_D_INNER]
    xs32 = xs_b.astype(F32)
    dt = _softplus(dtf_ref[...] + bias_r[...])
    da = dt * a_r[...]
    acs = jnp.dot(tri_l, da, precision=HIGHEST, preferred_element_type=F32)
    racs = jnp.dot(tri_u, da, precision=HIGHEST, preferred_element_type=F32)
    dt_t = _softplus(dtTf_ref[...] + bias_c[...])
    da_t = dt_t * a_c[...]
    acs_t = jnp.dot(da_t, tri_u, precision=HIGHEST, preferred_element_type=F32)
    racs_t = jnp.dot(da_t, tri_l, precision=HIGHEST, preferred_element_type=F32)
    last = acs[q - 1:q, :]
    wq = jnp.exp(last - acs) * dt
    for g in range(SSM_GROUPS):
        bg = xbc[:, SSM_D_INNER + g * SSM_STATE:SSM_D_INNER + (g + 1) * SSM_STATE]
        cg = xbc[:, SSM_D_INNER + SSM_BC_W + g * SSM_STATE:SSM_D_INNER + SSM_BC_W + (g + 1) * SSM_STATE]
        cb = lax.dot_general(cg, bg, (((1,), (1,)), ((), ())), preferred_element_type=F32)
        yoff = jnp.dot(cg, hf_s[g].astype(BF16), preferred_element_type=F32)
        for pr in range(SSM_HPG // 2):
            h0 = g * SSM_HPG + 2 * pr
            col0 = h0 * SSM_HEAD_DIM
            xp = xs_b[:, col0:col0 + LANES]
            ys = []
            for hh in (h0, h0 + 1):
                mf = jnp.exp(jnp.where(lower, acs[:, hh:hh + 1] - acs_t[hh:hh + 1, :], NEG_BIG)) \
                    * dt_t[hh:hh + 1, :]
                mb = jnp.exp(jnp.where(upper, racs[:, nh + hh:nh + hh + 1] - racs_t[nh + hh:nh + hh + 1, :],
                                       NEG_BIG)) * dt_t[nh + hh:nh + hh + 1, :]
                ys.append(jnp.dot((cb * (mf + mb)).astype(BF16), xp, preferred_element_type=F32))
            ef = jnp.exp(_pair(lane_lo, bc(acs[:, h0:h0 + 1]), bc(acs[:, h0 + 1:h0 + 2])))
            yf_ref[:, col0:col0 + LANES] = (_pair(lane_lo, ys[0], ys[1])
                                            + yoff[:, 2 * pr * 64:2 * pr * 64 + LANES] * ef
                                            + xs32[:, col0:col0 + LANES] * dsk_ref[:, col0:col0 + LANES])
        state_update(hf_s, g, xs32, bg, wq, last, 0)

    xbc2 = xb_ref[...]
    xs2 = xbc2[:, :SSM_D_INNER].astype(F32)
    dt2 = _softplus(dtb_ref[...] + bias_r[...])
    racs2 = jnp.dot(tri_u, dt2 * a_r[...], precision=HIGHEST, preferred_element_type=F32)
    first = racs2[0:1, :]
    wq2 = jnp.exp(first - racs2) * dt2
    for g in range(SSM_GROUPS):
        bg = xbc2[:, SSM_D_INNER + g * SSM_STATE:SSM_D_INNER + (g + 1) * SSM_STATE]
        cg = xbc2[:, SSM_D_INNER + SSM_BC_W + g * SSM_STATE:SSM_D_INNER + SSM_BC_W + (g + 1) * SSM_STATE]
        yoff = jnp.dot(cg, hb_s[g].astype(BF16), preferred_element_type=F32)
        for pr in range(SSM_HPG // 2):
            h0 = g * SSM_HPG + 2 * pr
            col0 = h0 * SSM_HEAD_DIM
            eb = jnp.exp(_pair(lane_lo, bc(racs2[:, nh + h0:nh + h0 + 1]), bc(racs2[:, nh + h0 + 1:nh + h0 + 2])))
            yb_ref[:, col0:col0 + LANES] = yoff[:, 2 * pr * 64:2 * pr * 64 + LANES] * eb
        state_update(hb_s, g, xs2, bg, wq2, first, nh)

    @pl.when(c == pl.num_programs(1) - 1)
    def _():
        hfo_ref[...] = hf_s[...]
        hbo_ref[...] = hb_s[...]


def ssd_bidir(xbc, dt_raw, dt_bias, a_neg, d_skip, h0f, h0b, batch):
    t = xbc.shape[0]
    l = t // batch
    q = SSM_CHUNK
    nc = l // q
    w = xbc.shape[1]
    dt_t = jnp.transpose(dt_raw[:, :2 * SSM_HEADS].reshape(batch, l, 2 * SSM_HEADS), (0, 2, 1))
    pad = LANES - 2 * SSM_HEADS
    bias_r = jnp.pad(dt_bias.reshape(1, -1), ((0, 0), (0, pad)))
    a_r = jnp.pad(a_neg.reshape(1, -1), ((0, 0), (0, pad)))
    dsk = jnp.repeat(d_skip, SSM_HEAD_DIM).reshape(1, SSM_D_INNER)
    hshape = (batch, SSM_GROUPS, SSM_STATE, SSM_HPG * SSM_HEAD_DIM)
    fwd = lambda b, c: (b * nc + c, 0)
    bwd = lambda b, c: (b * nc + nc - 1 - c, 0)
    const2 = lambda b, c: (0, 0)
    hmap = lambda b, c: (b, 0, 0, 0)
    hspec = pl.BlockSpec((None,) + hshape[1:], hmap)
    return pl.pallas_call(
        _ssd_kernel,
        grid=(batch, nc),
        in_specs=[pl.BlockSpec((q, w), fwd), pl.BlockSpec((q, LANES), fwd),
                  pl.BlockSpec((None, 2 * SSM_HEADS, q), lambda b, c: (b, 0, c)),
                  pl.BlockSpec((q, w), bwd), pl.BlockSpec((q, LANES), bwd),
                  pl.BlockSpec((None, 2 * SSM_HEADS, q), lambda b, c: (b, 0, nc - 1 - c)),
                  pl.BlockSpec((1, LANES), const2), pl.BlockSpec((1, LANES), const2),
                  pl.BlockSpec((2 * SSM_HEADS, 1), const2), pl.BlockSpec((2 * SSM_HEADS, 1), const2),
                  pl.BlockSpec((1, SSM_D_INNER), const2), hspec, hspec],
        out_specs=[pl.BlockSpec((q, SSM_D_INNER), fwd), pl.BlockSpec((q, SSM_D_INNER), bwd), hspec, hspec],
        out_shape=[jax.ShapeDtypeStruct((t, SSM_D_INNER), F32), jax.ShapeDtypeStruct((t, SSM_D_INNER), F32),
                   jax.ShapeDtypeStruct(hshape, F32), jax.ShapeDtypeStruct(hshape, F32)],
        scratch_shapes=[pltpu.VMEM(hshape[1:], F32), pltpu.VMEM(hshape[1:], F32)],
        compiler_params=_params(("parallel", "arbitrary")),
        name="ssd_bidir",
    )(xbc, dt_raw, dt_t, xbc, dt_raw, dt_t, bias_r, a_r, dt_bias.reshape(-1, 1), a_neg.reshape(-1, 1),
      dsk, h0f, h0b)


def _outproj_kernel(oa_ref, yf_ref, yb_ref, z_ref, ng_ref, w1_ref, w2_ref, x_ref, gate_ref, o_ref):
    y = yf_ref[...] + yb_ref[...]
    u = y * _silu(z_ref[...].astype(F32))
    un = (u * lax.rsqrt(jnp.mean(u * u, axis=-1, keepdims=True) + RMS_EPS)) * ng_ref[...]
    acc = (jnp.dot(oa_ref[...], w1_ref[...], preferred_element_type=F32)
           + jnp.dot(un.astype(BF16), w2_ref[...], preferred_element_type=F32))
    o_ref[...] = x_ref[...] + gate_ref[...] * acc


def out_proj(o_att, yf, yb, z, norm_g, w1, w2, x2d, gate, rows_per_mod):
    t, d = x2d.shape
    tm = _tile(rows_per_mod, 512)
    tpm = rows_per_mod // tm
    nb = gate.shape[0]
    row = lambda i: (i, 0)
    const = lambda i: (0, 0)
    return pl.pallas_call(
        _outproj_kernel,
        grid=(t // tm,),
        in_specs=[pl.BlockSpec((tm, d), row), pl.BlockSpec((tm, d), row), pl.BlockSpec((tm, d), row),
                  pl.BlockSpec((tm, d), row), pl.BlockSpec((1, d), const),
                  pl.BlockSpec(w1.shape, const), pl.BlockSpec(w2.shape, const),
                  pl.BlockSpec((tm, d), row), pl.BlockSpec((None, 1, d), lambda i: (i // tpm, 0, 0))],
        out_specs=pl.BlockSpec((tm, d), row),
        out_shape=jax.ShapeDtypeStruct((t, d), F32),
        compiler_params=_params(("parallel",)),
        name="mixer_out_proj",
    )(o_att, yf, yb, z, norm_g.reshape(1, d), w1, w2, x2d, gate.reshape(nb, 1, d))


def _mm_res_kernel(a_ref, w_ref, x_ref, gate_ref, o_ref):
    acc = jnp.dot(a_ref[...], w_ref[...], preferred_element_type=F32)
    o_ref[...] = x_ref[...] + gate_ref[...] * acc


def matmul_residual(a, w, x2d, gate, rows_per_mod):
    t, d = x2d.shape
    tm = _tile(rows_per_mod, 512)
    tpm = rows_per_mod // tm
    nb = gate.shape[0]
    return pl.pallas_call(
        _mm_res_kernel,
        grid=(t // tm,),
        in_specs=[pl.BlockSpec((tm, a.shape[1]), lambda i: (i, 0)),
                  pl.BlockSpec(w.shape, lambda i: (0, 0)),
                  pl.BlockSpec((tm, d), lambda i: (i, 0)),
                  pl.BlockSpec((None, 1, d), lambda i: (i // tpm, 0, 0))],
        out_specs=pl.BlockSpec((tm, d), lambda i: (i, 0)),
        out_shape=jax.ShapeDtypeStruct((t, d), F32),
        compiler_params=_params(("parallel",)),
        name="matmul_gated_residual",
    )(a, w, x2d, gate.reshape(nb, 1, d))


def _dft_stage1_kernel(m_ref, z_ref, o_ref):
    group = m_ref.shape[0]
    d2 = z_ref.shape[1] // group
    half = d2 // 2
    n1 = z_ref.shape[0]
    for j in range(group):
        zs = jnp.concatenate([z_ref[:, j * d2:j * d2 + half], z_ref[:, j * d2 + half:(j + 1) * d2]], axis=0)
        a = jnp.dot(m_ref[j], zs, preferred_element_type=F32)
        o_ref[0, j] = a[:n1].astype(o_ref.dtype)
        o_ref[1, j] = a[n1:].astype(o_ref.dtype)


def _dft_stage2_kernel(m_ref, a_ref, o_ref):
    o_ref[...] = jnp.dot(m_ref[...], a_ref[...], preferred_element_type=F32).astype(o_ref.dtype)


def _dft_tables(l, d):
    n1 = FOURIER_N1
    n2 = l // n1
    gw = d // FOURIER_GROUPS
    two_pi = 2.0 * math.pi
    ch = jnp.arange(gw, dtype=I32)
    ph = (ch[:, None] * ch[None, :]) % gw
    ang = ph.astype(F32) * (two_pi / gw)
    cc, sc = jnp.cos(ang) * gw ** -0.5, jnp.sin(ang) * gw ** -0.5
    eye = jnp.eye(FOURIER_GROUPS, dtype=F32)
    w_cs = jnp.concatenate([jnp.kron(eye, cc), -jnp.kron(eye, sc)], axis=1)
    k1 = jnp.arange(n1, dtype=I32)
    pos = jnp.arange(n2, dtype=I32)[:, None, None] + n2 * jnp.arange(n1, dtype=I32)[None, None, :]
    ang1 = ((k1[None, :, None] * pos) % l).astype(F32) * (two_pi / l)
    gc, gs = jnp.cos(ang1) * l ** -0.5, jnp.sin(ang1) * l ** -0.5
    m1 = jnp.concatenate([jnp.concatenate([gc, gs], axis=2),
                          jnp.concatenate([-gs, gc], axis=2)], axis=1)
    k2 = jnp.arange(n2, dtype=I32)
    ang2 = ((k2[:, None] * k2[None, :]) % n2).astype(F32) * (two_pi / n2)
    m2 = jnp.concatenate([jnp.cos(ang2), jnp.sin(ang2)], axis=1)
    return w_cs.astype(BF16), m1.astype(BF16), m2.astype(BF16)


def fourier_positions(z, m1, m2, batch):
    t, d2 = z.shape
    d = d2 // 2
    l = t // batch
    n1 = FOURIER_N1
    n2 = l // n1
    group = _tile(n2, 8)
    a = pl.pallas_call(
        _dft_stage1_kernel,
        grid=(batch, n2 // group),
        in_specs=[pl.BlockSpec((group, 2 * n1, 2 * n1), lambda b, j: (j, 0, 0)),
                  pl.BlockSpec((None, n1, group * d2), lambda b, j: (b, 0, j))],
        out_specs=pl.BlockSpec((None, 2, group, n1, d), lambda b, j: (b, 0, j, 0, 0)),
        out_shape=jax.ShapeDtypeStruct((batch, 2, n2, n1, d), BF16),
        compiler_params=_params(("parallel", "parallel")),
        name="dft_stage1",
    )(m1, z.reshape(batch, n1, n2 * d2))
    cols = n1 * d
    tn = _tile(cols, 8192)
    f = pl.pallas_call(
        _dft_stage2_kernel,
        grid=(batch, cols // tn),
        in_specs=[pl.BlockSpec((n2, 2 * n2), lambda b, j: (0, 0)),
                  pl.BlockSpec((None, 2 * n2, tn), lambda b, j: (b, 0, j))],
        out_specs=pl.BlockSpec((None, n2, tn), lambda b, j: (b, 0, j)),
        out_shape=jax.ShapeDtypeStruct((batch, n2, cols), BF16),
        compiler_params=_params(("parallel", "parallel")),
        name="dft_stage2",
    )(m2, a.reshape(batch, 2 * n2, cols))
    return f.reshape(t, d)


SUB = 8
ROW_CODE_SHIFT = 14


def _to_row_tiles(ref, val):
    rows = val.shape[0]
    for j in range(SUB):
        ref[pl.ds(j, rows, stride=SUB), :] = val[:, j * LANES:(j + 1) * LANES]


def _from_row_tiles(ref, rows):
    return jnp.concatenate([ref[pl.ds(j, rows, stride=SUB), :] for j in range(SUB)], axis=1)


def _row_tile(ref, row):
    if isinstance(row, int):
        return ref.at[pl.ds(row * SUB, SUB), :]
    return ref.at[pl.ds(pl.multiple_of(row * SUB, SUB), SUB), :]


def _route_kernel(x_ref, g_ref, sh_ref, sc_ref, wr_ref, br_ref, h_ref, ti_ref, tg_ref, rank_ref, cnt_ref, carry):
    @pl.when(pl.program_id(0) == 0)
    def _():
        carry[...] = jnp.zeros(carry.shape, carry.dtype)

    h = _norm_mod(x_ref[...], g_ref[...], sh_ref[...], sc_ref[...])
    _to_row_tiles(h_ref, h)
    logits = jnp.dot(h, wr_ref[...], precision=HIGHEST, preferred_element_type=F32) + br_ref[...]
    lane = lax.broadcasted_iota(I32, logits.shape, 1)
    vals, idxs = [], []
    cur = logits
    for _ in range(TOP_K):
        m = jnp.max(cur, axis=-1, keepdims=True)
        idx = jnp.min(jnp.where(cur == m, lane, LANES), axis=-1, keepdims=True)
        vals.append(m)
        idxs.append(idx)
        cur = jnp.where(lane == idx, -jnp.inf, cur)
    es = [jnp.exp(v - vals[0]) for v in vals]
    tot = es[0] + es[1] + es[2] + es[3]
    ti = jnp.zeros(logits.shape, I32)
    tg = jnp.zeros(logits.shape, F32)
    for k in range(TOP_K):
        ti = jnp.where(lane == k, idxs[k], ti)
        tg = jnp.where(lane == k, es[k] / tot, tg)
    ti_ref[...] = ti
    tg_ref[...] = tg
    tm = logits.shape[0]
    earlier = (lax.broadcasted_iota(I32, (tm, tm), 1) < lax.broadcasted_iota(I32, (tm, tm), 0)).astype(BF16)
    base = carry[...]
    rank = jnp.zeros(logits.shape, I32)
    for k in range(TOP_K):
        hit = lane == idxs[k]
        onehot = jnp.where(hit, 1.0, 0.0)
        before = jnp.dot(earlier, onehot.astype(BF16), preferred_element_type=F32)
        rk = jnp.sum(jnp.where(hit, base + before, 0.0), axis=-1, keepdims=True)
        rank = jnp.where(lane == k, rk.astype(I32), rank)
        base = base + jnp.sum(onehot, axis=0, keepdims=True)
    rank_ref[...] = rank
    carry[...] = base
    cnt_ref[...] = base


def route(x2d, g, shift, scale, rows_per_mod, w_router, b_router):
    t, d = x2d.shape
    assert d == SUB * LANES
    nb = shift.shape[0]
    tm = _tile(rows_per_mod, 512)
    tpm = rows_per_mod // tm
    ne = w_router.shape[1]
    wr = jnp.pad(w_router, ((0, 0), (0, LANES - ne)))
    br = jnp.pad(b_router.reshape(1, ne), ((0, 0), (0, LANES - ne)), constant_values=NEG_BIG)
    return pl.pallas_call(
        _route_kernel,
        grid=(t // tm,),
        in_specs=[pl.BlockSpec((tm, d), lambda i: (i, 0)),
                  pl.BlockSpec((1, d), lambda i: (0, 0)),
                  pl.BlockSpec((None, 1, d), lambda i: (i // tpm, 0, 0)),
                  pl.BlockSpec((None, 1, d), lambda i: (i // tpm, 0, 0)),
                  pl.BlockSpec((d, LANES), lambda i: (0, 0)),
                  pl.BlockSpec((1, LANES), lambda i: (0, 0))],
        out_specs=[pl.BlockSpec((tm * SUB, LANES), lambda i: (i, 0)),
                   pl.BlockSpec((tm, LANES), lambda i: (i, 0)),
                   pl.BlockSpec((tm, LANES), lambda i: (i, 0)),
                   pl.BlockSpec((tm, LANES), lambda i: (i, 0)),
                   pl.BlockSpec((1, LANES), lambda i: (0, 0))],
        out_shape=[jax.ShapeDtypeStruct((t * SUB, LANES), F32), jax.ShapeDtypeStruct((t, LANES), I32),
                   jax.ShapeDtypeStruct((t, LANES), F32), jax.ShapeDtypeStruct((t, LANES), I32),
                   jax.ShapeDtypeStruct((1, LANES), F32)],
        scratch_shapes=[pltpu.VMEM((1, LANES), F32)],
        compiler_params=_params(("arbitrary",)),
        name="moe_route",
    )(x2d, g.reshape(1, d), shift.reshape(nb, 1, d), scale.reshape(nb, 1, d), wr, br)


def _dest_kernel(ti_ref, rank_ref, start_ref, dest_ref):
    ti = ti_ref[...]
    lane = lax.broadcasted_iota(I32, ti.shape, 1)
    dest = jnp.zeros(ti.shape, I32)
    for k in range(TOP_K):
        start = jnp.sum(jnp.where(lane == ti[:, k:k + 1], start_ref[...], 0), axis=-1, keepdims=True)
        dest = jnp.where(lane == k, start + rank_ref[:, k:k + 1], dest)
    dest_ref[...] = dest


def moe_plan(ti, rank, cnt, t):
    tb = MOE_ROWS
    tm = _tile(t, 512)
    counts = cnt[0, :N_EXPERTS].astype(I32)
    padded = (counts + tb - 1) // tb * tb
    pend = jnp.cumsum(padded)
    pstart = jnp.pad(pend - padded, (0, LANES - N_EXPERTS)).reshape(1, LANES)
    dest = pl.pallas_call(
        _dest_kernel,
        grid=(t // tm,),
        in_specs=[pl.BlockSpec((tm, LANES), lambda i: (i, 0)), pl.BlockSpec((tm, LANES), lambda i: (i, 0)),
                  pl.BlockSpec((1, LANES), lambda i: (0, 0))],
        out_specs=pl.BlockSpec((tm, LANES), lambda i: (i, 0)),
        out_shape=jax.ShapeDtypeStruct((t, LANES), I32),
        compiler_params=_params(("parallel",)),
        name="moe_dest",
    )(ti, rank, pstart)
    dest = dest[:, :TOP_K].reshape(-1)
    n_blocks = (t * TOP_K + N_EXPERTS * (tb - 1)) // tb + 1
    assert t <= 1 << ROW_CODE_SHIFT and (TOP_K * t + tb) << ROW_CODE_SHIFT < 1 << 31
    asg = jnp.arange(t * TOP_K, dtype=I32)
    real = (((asg % TOP_K) * t + asg // TOP_K) << ROW_CODE_SHIFT) | (asg // TOP_K)
    spill = (TOP_K * t + jnp.arange(n_blocks * tb, dtype=I32) % tb) << ROW_CODE_SHIFT
    codes = spill.at[dest].set(real, unique_indices=True, mode='promise_in_bounds')
    starts = jnp.arange(n_blocks, dtype=I32) * tb
    blk_e = jnp.minimum(jnp.sum((pend[None, :] <= starts[:, None]).astype(I32), axis=1), N_EXPERTS - 1)
    nact = (pend[-1] // tb).astype(I32).reshape(1)
    return codes, blk_e, nact


def _expert_kernel(tb, n_tok, be_ref, code_ref, nact_ref, h_hbm, wgu_ref, bgu_ref, wdn_ref, bdn_ref, y_hbm,
                   xbuf, ybuf, wgu_s, wdn_perm, wdn_s, gsem, ssem):
    i = pl.program_id(0)
    n_blocks = pl.num_programs(0)
    nact = nact_ref[0]
    de = wdn_ref.shape[0]
    slot = i % 2
    other = 1 - slot

    def start_gather(base, r, dst_slot, priority=0):
        tok = code_ref[base + r] & ((1 << ROW_CODE_SHIFT) - 1)
        pltpu.make_async_copy(_row_tile(h_hbm, tok), _row_tile(xbuf.at[dst_slot], r),
                              gsem.at[dst_slot]).start(priority)

    def start_scatter(base, r, src_slot, priority=0):
        row = code_ref[base + r] >> ROW_CODE_SHIFT
        pltpu.make_async_copy(_row_tile(ybuf.at[src_slot], r), _row_tile(y_hbm, row),
                              ssem.at[src_slot]).start(priority)

    def wait_gather(s):
        pltpu.make_async_copy(h_hbm.at[pl.ds(0, tb * SUB), :], xbuf.at[s], gsem.at[s]).wait()

    def wait_scatter(s):
        pltpu.make_async_copy(ybuf.at[s], y_hbm.at[pl.ds(0, tb * SUB), :], ssem.at[s]).wait()

    @pl.when(i == 0)
    def _():
        ybuf[...] = jnp.zeros(ybuf.shape, ybuf.dtype)

        def first(r, carry):
            start_gather(0, r, 0)
            return carry
        lax.fori_loop(0, tb, first, 0, unroll=8)

    @pl.when((i < nact) & ((i == 0) | (be_ref[i] != be_ref[jnp.maximum(i - 1, 0)])))
    def _():
        wgu_s[...] = wgu_ref[...].astype(BF16)
        half = LANES // 2
        for c in range(wdn_perm.shape[0]):
            for m in range(0, de, LANES):
                wdn_perm[c, pl.ds(m, half, stride=2), :] = wdn_ref[m:m + half, c * LANES:(c + 1) * LANES]
                wdn_perm[c, pl.ds(m + 1, half, stride=2), :] = wdn_ref[m + half:m + LANES, c * LANES:(c + 1) * LANES]
            wdn_s[:, c * LANES:(c + 1) * LANES] = wdn_perm[c].astype(BF16)

    def start_copies(slot, other):
        wait_gather(slot)

        @pl.when(i >= 1)
        def _():
            wait_scatter(slot)

        prev_base = jnp.where(i == 0, n_blocks - 1, i - 1) * tb
        next_base = (i + 1) * tb
        for r in range(tb):
            start_gather(next_base, r, other, r % 2)
            start_scatter(prev_base, r, other, (r + 1) % 2)

    def run_block(slot):
        x = _from_row_tiles(xbuf.at[slot], tb).astype(BF16)
        even = lax.broadcasted_iota(I32, (tb, LANES), 1) % 2 == 0
        cw = 2 * LANES
        n_chunks = wgu_s.shape[1] // cw
        parts = []
        for c in range(n_chunks):
            gu = jnp.dot(x, wgu_s[:, c * cw:(c + 1) * cw], preferred_element_type=F32) + bgu_ref[:, c * cw:(c + 1) * cw]
            acts = []
            for c0 in range(0, cw, LANES):
                v = gu[:, c0:c0 + LANES]
                glu = jnp.minimum(v, SWIGLU_LIMIT)
                lin = jnp.clip(v, -SWIGLU_LIMIT, SWIGLU_LIMIT) + 1.0
                acts.append(glu * jax.nn.sigmoid(SWIGLU_ALPHA * glu) * pltpu.roll(lin, LANES - 1, 1))
            parts.append(jnp.where(even, acts[0], pltpu.roll(acts[1], 1, 1)).astype(BF16))
        y = jnp.dot(jnp.concatenate(parts, axis=1), wdn_s[...], preferred_element_type=F32) + bdn_ref[...]
        _to_row_tiles(ybuf.at[slot], y)

    for parity in range(2):
        pl.when((i < nact) & (slot == parity))(functools.partial(start_copies, parity, 1 - parity))
    for parity in range(2):
        pl.when((i < nact) & (slot == parity))(functools.partial(run_block, parity))

    @pl.when(i == nact)
    def _():
        wait_gather(slot)
        wait_scatter(slot)

        def last(r, carry):
            start_scatter((i - 1) * tb, r, other)
            return carry
        lax.fori_loop(0, tb, last, 0, unroll=8)
        wait_scatter(other)


def expert_ffn(h_rt, codes, blk_e, nact, layer, w_gu, b_gu, w_dn, b_dn):
    r = codes.shape[0]
    n_tok = h_rt.shape[0] // SUB
    tb = MOE_ROWS
    _, ne, d, de2 = w_gu.shape
    de = de2 // 2
    b_gu = b_gu.reshape(-1, ne, 1, de2)
    b_dn = b_dn.reshape(-1, ne, 1, d)
    return pl.pallas_call(
        functools.partial(_expert_kernel, tb, n_tok),
        grid_spec=pltpu.PrefetchScalarGridSpec(
            num_scalar_prefetch=3, grid=(r // tb,),
            in_specs=[pl.BlockSpec(memory_space=pl.ANY),
                      pl.BlockSpec((None, None, d, de2), lambda i, be, cd, na: (layer, be[i], 0, 0)),
                      pl.BlockSpec((None, None, 1, de2), lambda i, be, cd, na: (layer, be[i], 0, 0)),
                      pl.BlockSpec((None, None, de, d), lambda i, be, cd, na: (layer, be[i], 0, 0)),
                      pl.BlockSpec((None, None, 1, d), lambda i, be, cd, na: (layer, be[i], 0, 0))],
            out_specs=pl.BlockSpec(memory_space=pl.ANY),
            scratch_shapes=[pltpu.VMEM((2, tb * SUB, LANES), F32), pltpu.VMEM((2, tb * SUB, LANES), F32),
                            pltpu.VMEM((d, de2), BF16), pltpu.VMEM((d // LANES, de, LANES), F32),
                            pltpu.VMEM((de, d), BF16),
                            pltpu.SemaphoreType.DMA((2,)), pltpu.SemaphoreType.DMA((2,))]),
        out_shape=jax.ShapeDtypeStruct(((TOP_K * n_tok + tb) * SUB, LANES), F32),
        compiler_params=_params(("arbitrary",), 58),
        name="moe_experts",
    )(blk_e, codes, nact, h_rt, w_gu, b_gu, w_dn, b_dn)


def _combine_kernel(y0_ref, y1_ref, y2_ref, y3_ref, x_ref, gate_ref, tg_ref, o_ref):
    tg = tg_ref[...]
    tm = x_ref.shape[0]
    ys = [_from_row_tiles(y_ref, tm) * tg[:, k:k + 1] for k, y_ref in enumerate((y0_ref, y1_ref, y2_ref, y3_ref))]
    o_ref[...] = x_ref[...] + gate_ref[...] * ((ys[0] + ys[1]) + (ys[2] + ys[3]))


def moe_combine(y4, tg, x2d, gate, rows_per_mod):
    t, d = x2d.shape
    nb = gate.shape[0]
    tm = _tile(rows_per_mod, MOE_ROWS)
    tpm = rows_per_mod // tm
    nt = t // tm
    planes = [pl.BlockSpec((tm * SUB, LANES), functools.partial(lambda k, i: (k * nt + i, 0), k))
              for k in range(TOP_K)]
    return pl.pallas_call(
        _combine_kernel,
        grid=(nt,),
        in_specs=planes + [
                  pl.BlockSpec((tm, d), lambda i: (i, 0)),
                  pl.BlockSpec((None, 1, d), lambda i: (i // tpm, 0, 0)),
                  pl.BlockSpec((tm, LANES), lambda i: (i, 0))],
        out_specs=pl.BlockSpec((tm, d), lambda i: (i, 0)),
        out_shape=jax.ShapeDtypeStruct((t, d), F32),
        compiler_params=_params(("parallel",)),
        name="moe_combine",
    )(y4, y4, y4, y4, x2d, gate.reshape(nb, 1, d), tg)


def moe_layer(x2d, norm_g, shift, scale, gate, rows_per_mod, w_router, b_router, layer, w_gu, b_gu, w_dn, b_dn):
    t = x2d.shape[0]
    h_rt, ti, tg, rank, cnt = route(x2d, norm_g, shift, scale, rows_per_mod, w_router, b_router)
    codes, blk_e, nact = moe_plan(ti, rank, cnt, t)
    y4 = expert_ffn(h_rt, codes, blk_e, nact, layer, w_gu, b_gu, w_dn, b_dn)
    return moe_combine(y4, tg, x2d, gate, rows_per_mod)


def _rope_tables(s):
    n_freq = DA_HEAD_DIM // 4
    freqs = ROPE_BASE ** (-jnp.arange(n_freq, dtype=F32) / n_freq)
    pos = jnp.arange(s, dtype=I32)
    ang_r = (pos // GRID_W).astype(F32)[:, None] * freqs
    ang_c = (pos % GRID_W).astype(F32)[:, None] * freqs
    ang = jnp.concatenate([ang_r, ang_r, ang_c, ang_c], axis=-1)
    ang = jnp.concatenate([ang, ang], axis=-1)
    return jnp.cos(ang), jnp.sin(ang)


def _even_layer(x2d, ctx2d, mods, norm_g1, w_in, w_out, q_norm_g, k_norm_g, da_lambda, da_subln_g,
                conv_xb_w, conv_xb_b, conv_c_w, conv_c_b, dt_bias, a_log, d_skip, ssm_norm_g,
                lam_init, batch):
    sx1, cx1, gx1, sc1, cc1 = mods
    t, d = x2d.shape
    s = t // batch
    ctx_len = ctx2d.shape[0] // batch
    qkw = DA_HEADS * 2 * DA_HEAD_DIM
    vw = DA_HEADS * DA_V_DIM
    col_q, col_z = 0, qkw
    col_c = col_z + SSM_D_INNER
    col_k = col_c + SSM_BC_W
    col_v = col_k + qkw
    col_xb = col_v + vw
    col_dt = col_xb + SSM_XB_W
    wb = w_in.astype(BF16)
    w_q, w_k, w_v = wb[:, col_q:col_q + qkw], wb[:, col_k:col_k + qkw], wb[:, col_v:col_v + vw]
    w_z = wb[:, col_z:col_z + SSM_D_INNER]
    w_xbc = jnp.concatenate([wb[:, col_xb:col_xb + SSM_XB_W], wb[:, col_c:col_c + SSM_BC_W]], axis=1)
    w_dt = jnp.pad(wb[:, col_dt:col_dt + 2 * SSM_HEADS], ((0, 0), (0, LANES - 2 * SSM_HEADS)))
    weights = [w_q, w_k, w_v, w_z, w_xbc, w_dt]
    dts = [BF16, BF16, BF16, BF16, BF16, F32]
    q_u, k_u, v_x, z_x, xbc_u, dt_x = nm_panel(x2d, norm_g1, sx1, cx1, s, weights, dts)
    _, kc_u, v_c, _, xbc_cu, dt_c = nm_panel(ctx2d, norm_g1, sc1, cc1, batch * ctx_len, weights, dts)

    lam = (jnp.exp(jnp.sum(da_lambda[0] * da_lambda[1])) -
           jnp.exp(jnp.sum(da_lambda[2] * da_lambda[3]))).astype(F32) + lam_init
    a_neg = -jnp.exp(a_log.astype(F32)).reshape(-1)

    cos_t, sin_t = _rope_tables(s)
    q = qk_prep(q_u, q_norm_g, cos_t, sin_t, s, True, DA_HEAD_DIM ** -0.5 * LOG2E)
    score_bound = (1.01 * DA_HEAD_DIM ** 0.5 * LOG2E) * jnp.max(jnp.abs(q_norm_g)) * jnp.max(jnp.abs(k_norm_g))
    k_x = qk_prep(k_u, k_norm_g, cos_t, sin_t, s, True, 1.0)
    k_c = qk_prep(kc_u, k_norm_g, cos_t, sin_t, ctx_len, False, 1.0)
    k_all = jnp.concatenate([k_x.reshape(batch, s, qkw), k_c.reshape(batch, ctx_len, qkw)], axis=1)
    v_all = jnp.concatenate([v_x.reshape(batch, s, vw), v_c.reshape(batch, ctx_len, vw)], axis=1)
    o_att = diff_attention(q, k_all, v_all, lam, score_bound, da_subln_g, lam_init, batch)

    conv_w = jnp.concatenate([conv_xb_w, conv_c_w], axis=1)
    conv_b = jnp.concatenate([conv_xb_b, conv_c_b], axis=0)
    xbc_x = conv_silu(xbc_u, conv_w, conv_b, s)
    xbc_c = conv_silu(xbc_cu, conv_w, conv_b, ctx_len)
    hshape = (batch, SSM_GROUPS, SSM_STATE, SSM_HPG * SSM_HEAD_DIM)
    zero_h = jnp.zeros(hshape, F32)
    dt_bias_f = dt_bias.reshape(-1).astype(F32)
    _, _, h_f, h_b = ssd_bidir(xbc_c, dt_c, dt_bias_f, a_neg, d_skip, zero_h, zero_h, batch)
    y_f, y_b, _, _ = ssd_bidir(xbc_x, dt_x, dt_bias_f, a_neg, d_skip, h_f, h_b, batch)
    wo = w_out.astype(BF16)
    return out_proj(o_att, y_f, y_b, z_x, ssm_norm_g, wo[:vw], wo[vw:], x2d, gx1, s)


def _odd_layer(x2d, mods, norm_g1, w_fourier, batch):
    sx1, cx1, gx1 = mods
    t, d = x2d.shape
    s = t // batch
    w_cs, m1, m2 = _dft_tables(s, d)
    (z,) = nm_panel(x2d, norm_g1, sx1, cx1, s, [w_cs], [BF16])
    f = fourier_positions(z, m1, m2, batch)
    return matmul_residual(f, w_fourier.astype(BF16), x2d, gx1, s)


def kernel(x, c, ctx, c_ctx, ada_w, ada_b, norm_g, w_in, w_out, q_norm_g, k_norm_g, da_lambda, da_subln_g, conv_xb_w, conv_xb_b, conv_c_w, conv_c_b, dt_bias, a_log, d_skip, ssm_norm_g, w_fourier, w_router, b_router, w_gate_up, b_gate_up, w_down, b_down):
    b, s, d = x.shape
    depth = ada_w.shape[0]
    assert b + 1 <= 8
    cvecs = jnp.concatenate([c, c_ctx[None, :], jnp.zeros((8 - b - 1, d), F32)], axis=0)
    mods = ada_all(cvecs, ada_w, ada_b)
    x2d = x.reshape(b * s, d)
    ctx2d = ctx.reshape(-1, d)
    for i in range(depth):
        m = mods[i].reshape(8, 6, d)
        sx1, cx1, gx1, sx2, cx2, gx2 = [m[:b, j] for j in range(6)]
        if i % 2 == 0:
            e = i // 2
            lam_init = 0.8 - 0.6 * math.exp(-0.3 * i)
            if any(j % 2 == 0 for j in range(i + 1, depth)):
                raise NotImplementedError("context stream output is only needed for depth > 2")
            sc1, cc1 = m[b:b + 1, 0], m[b:b + 1, 1]
            x2d = _even_layer(x2d, ctx2d, (sx1, cx1, gx1, sc1, cc1), norm_g[i, 0], w_in[e], w_out[e],
                              q_norm_g[e], k_norm_g[e], da_lambda[e], da_subln_g[e], conv_xb_w[e],
                              conv_xb_b[e], conv_c_w[e], conv_c_b[e], dt_bias[e], a_log[e], d_skip[e],
                              ssm_norm_g[e], lam_init, b)
        else:
            x2d = _odd_layer(x2d, (sx1, cx1, gx1), norm_g[i, 0], w_fourier[i // 2], b)
        x2d = moe_layer(x2d, norm_g[i, 1], sx2, cx2, gx2, s, w_router[i], b_router[i],
                        i, w_gate_up, b_gate_up, w_down, b_down)
    return x2d.reshape(b, s, d)
```
